```python
import math
import jax
import jax.numpy as jnp
from jax import lax
import numpy as np


D_MODEL = 1024
BATCH = 8
SEQ = 4096
DEPTH = 4

GRID_W = 64
CTX_LEN = 256
HEAD_DIM = 64
DIFF_HEADS = 4
DIFF_V_DIM = 2 * HEAD_DIM
DIFF_WIDTH = DIFF_HEADS * DIFF_V_DIM
POOL_WINDOWS = (2, 4, 8, 16)
POOL_GROUP = 64
POOL_WIDTH = POOL_GROUP * len(POOL_WINDOWS)
RET_HEADS = 4
RET_WIDTH = RET_HEADS * HEAD_DIM
RET_CHUNK = 128
MIX_WIDTH = DIFF_WIDTH + POOL_WIDTH + RET_WIDTH
OFF_DQ = 0
OFF_DK = OFF_DQ + DIFF_HEADS * 2 * HEAD_DIM
OFF_DV = OFF_DK + DIFF_HEADS * 2 * HEAD_DIM
OFF_PU = OFF_DV + DIFF_WIDTH
OFF_RQ = OFF_PU + POOL_WIDTH
OFF_RK = OFF_RQ + RET_WIDTH
OFF_RV = OFF_RK + RET_WIDTH
OFF_RG = OFF_RV + RET_WIDTH
IN_WIDTH = OFF_RG + RET_WIDTH
Q_BLOCK = 128
ROPE_BASE = 10000.0
ROPE_FREQS = HEAD_DIM // 4
N_EXPERTS = 16
N_GROUPS = 4
EXPERTS_PER_GROUP = N_EXPERTS // N_GROUPS
TOP_K = 2
D_EXPERT = 768
MOE_BLOCK = 128
N_MOD = 6
EPS = 1e-6

kernel_name = 'hybrid_diff_pool_retention_moe_dit'


def rmsnorm(x, w):
    xf = x.astype(jnp.float32)
    y = xf * lax.rsqrt(jnp.mean(xf * xf, axis=-1, keepdims=True) + EPS)
    return y.astype(x.dtype) * w


def modulate(h, shift, scale):
    return h * (1 + scale) + shift


def axial_rope_tables(L):
    rows = L // GRID_W
    row = jnp.repeat(jnp.arange(rows), GRID_W).astype(jnp.float32)
    col = jnp.tile(jnp.arange(GRID_W), rows).astype(jnp.float32)
    inv = ROPE_BASE ** (-jnp.arange(ROPE_FREQS, dtype=jnp.float32) / ROPE_FREQS)
    ang_r = row[:, None] * inv
    ang_c = col[:, None] * inv
    ang = jnp.concatenate([ang_r, ang_r, ang_c, ang_c], axis=-1)
    return jnp.cos(ang), jnp.sin(ang)


def rotate_half_axial(x):
    shp = x.shape
    xr = x.reshape(shp[:-1] + (2, 2, ROPE_FREQS))
    x1 = xr[..., 0, :]
    x2 = xr[..., 1, :]
    return jnp.stack([-x2, x1], axis=-2).reshape(shp)


def apply_rope(x, cos, sin):
    extra = x.ndim - 3
    shp = (cos.shape[0],) + (1,) * extra + (HEAD_DIM,)
    return (x * cos.reshape(shp) + rotate_half_axial(x) * sin.reshape(shp)).astype(x.dtype)


def split_proj(p):
    B, L, _ = p.shape
    dq = p[..., OFF_DQ:OFF_DK].reshape(B, L, DIFF_HEADS, 2, HEAD_DIM)
    dk = p[..., OFF_DK:OFF_DV].reshape(B, L, DIFF_HEADS, 2, HEAD_DIM)
    dv = p[..., OFF_DV:OFF_PU].reshape(B, L, DIFF_HEADS, DIFF_V_DIM)
    pu = p[..., OFF_PU:OFF_RQ]
    rq = p[..., OFF_RQ:OFF_RK].reshape(B, L, RET_HEADS, HEAD_DIM)
    rk = p[..., OFF_RK:OFF_RV].reshape(B, L, RET_HEADS, HEAD_DIM)
    rv = p[..., OFF_RV:OFF_RG].reshape(B, L, RET_HEADS, HEAD_DIM)
    rg = p[..., OFF_RG:IN_WIDTH]
    return dq, dk, dv, pu, rq, rk, rv, rg


def diff_softmax_attend(q, k, v, lam):
    s = jnp.einsum('bqhmd,bkhmd->bhmqk', q, k).astype(jnp.float32) * (HEAD_DIM ** -0.5)
    p = jax.nn.softmax(s, axis=-1)
    p = (p[:, :, 0] - lam * p[:, :, 1]).astype(v.dtype)
    return jnp.einsum('bhqk,bkhe->bqhe', p, v)


def diff_attention_latent(q, k, v, lam):
    B, L = q.shape[:2]
    nq = L // Q_BLOCK
    qb = jnp.moveaxis(q.reshape((B, nq, Q_BLOCK) + q.shape[2:]), 1, 0)
    out = lax.map(lambda blk: diff_softmax_attend(blk, k, v, lam), qb)
    return jnp.moveaxis(out, 0, 1).reshape((B, L) + out.shape[3:])


def diff_head_norm(o, w, lam_init):
    o = rmsnorm(o, w) * (1.0 - lam_init)
    return o.reshape(o.shape[0], o.shape[1], DIFF_WIDTH)


def multiscale_pool(u, w, scale):
    B, L, _ = u.shape
    csum = jnp.concatenate([jnp.zeros((B, 1, POOL_WIDTH), jnp.float32),
                            jnp.cumsum(u.astype(jnp.float32), axis=1)], axis=1)
    t = jnp.arange(L)
    outs = []
    for g, win in enumerate(POOL_WINDOWS):
        lo = jnp.clip(t - win // 2, 0, L)
        hi = jnp.clip(t + win - win // 2, 0, L)
        cs = csum[..., g * POOL_GROUP:(g + 1) * POOL_GROUP]
        mean = (cs[:, hi] - cs[:, lo]) / (hi - lo).astype(jnp.float32)[:, None]
        diff = mean.astype(u.dtype) - u[..., g * POOL_GROUP:(g + 1) * POOL_GROUP]
        outs.append(diff @ w[g])
    return jnp.concatenate(outs, axis=-1) * scale


def retention_chunked(q, k, v, log_g, state0, strict):
    B, H, L, dk = q.shape
    dv = v.shape[-1]
    n = L // RET_CHUNK
    qc = q.reshape(B, H, n, RET_CHUNK, dk)
    kc = k.reshape(B, H, n, RET_CHUNK, dk)
    vc = v.reshape(B, H, n, RET_CHUNK, dv)
    i = jnp.arange(RET_CHUNK, dtype=jnp.float32)
    dist = i[:, None] - i[None, :]
    keep = (dist > 0) if strict else (dist >= 0)
    decay = jnp.where(keep, jnp.exp(log_g[:, None, None] * jnp.maximum(dist, 0.0)), 0.0)
    scores = jnp.einsum('bhncd,bhnsd->bhncs', qc, kc) * decay[None, :, None]
    y = jnp.einsum('bhncs,bhnse->bhnce', scores, vc)
    zeta = jnp.exp(log_g[:, None] * (RET_CHUNK - 1 - i))
    xi = jnp.exp(log_g[:, None] * (i + 1))
    kv = jnp.einsum('bhnsd,bhnse->nbhde', kc * zeta[None, :, None, :, None], vc)
    g_chunk = jnp.exp(log_g * RET_CHUNK)[None, :, None, None]

    def step(R, kv_n):
        return g_chunk * R + kv_n, R

    _, r_prev = lax.scan(step, state0, kv)
    cross = jnp.einsum('bhncd,nbhde->bhnce', qc * xi[None, :, None, :, None], r_prev)
    return (y + cross).reshape(B, H, L, dv)


def retention_final_state(k, v, log_g, reverse):
    L = k.shape[2]
    j = jnp.arange(L, dtype=jnp.float32)
    dist = j if reverse else (L - 1 - j)
    w = jnp.exp(log_g[:, None] * dist)
    return jnp.einsum('bhld,bhle->bhde', k * w[None, :, :, None], v)


def bidir_retention(q, k, v, lg_f, lg_b, state_f, state_b):
    y_f = retention_chunked(q, k, v, lg_f, state_f, False)
    rev = lambda a: a[:, :, ::-1]
    y_b = rev(retention_chunked(rev(q), rev(k), rev(v), lg_b, state_b, True))
    return y_f + y_b


def retention_output(y, gate, w):
    y = jnp.moveaxis(y, 1, 2)
    y = y * lax.rsqrt(jnp.mean(y * y, axis=-1, keepdims=True) + EPS)
    y = y.reshape(y.shape[0], y.shape[1], RET_WIDTH).astype(gate.dtype) * w
    return jax.nn.silu(gate) * y


def hybrid_mixer(h, hc, cos, sin, w_in, w_out, diff_lambda, diff_norm_w, lam_init,
                 pool_w, pool_scale, ret_a_f, ret_a_b, ret_norm_w, need_ctx):
    dq, dk, dv, pu, rq, rk, rv, rg = split_proj(h @ w_in)
    cq, ck, cv, cu, crq, crk, crv, crg = split_proj(hc @ w_in)
    lam = (jnp.exp(jnp.sum(diff_lambda[0] * diff_lambda[1])) - jnp.exp(jnp.sum(diff_lambda[2] * diff_lambda[3]))
           + lam_init).astype(jnp.float32)
    dq = apply_rope(dq, cos, sin)
    dk = apply_rope(dk, cos, sin)
    k_all = jnp.concatenate([ck, dk], axis=1)
    v_all = jnp.concatenate([cv, dv], axis=1)
    a = diff_head_norm(diff_attention_latent(dq, k_all, v_all, lam), diff_norm_w, lam_init)
    b = multiscale_pool(pu, pool_w, pool_scale)
    lg_f = -jnp.exp(ret_a_f.astype(jnp.float32))
    lg_b = -jnp.exp(ret_a_b.astype(jnp.float32))
    k_scale = HEAD_DIM ** -0.5
    to_heads = lambda t: jnp.moveaxis(t, 2, 1).astype(jnp.float32)
    rq_h = to_heads(apply_rope(rq, cos, sin))
    rk_h = to_heads(apply_rope(rk, cos, sin)) * k_scale
    rv_h = to_heads(rv)
    crq_h = to_heads(crq)
    crk_h = to_heads(crk) * k_scale
    crv_h = to_heads(crv)
    s_f = retention_final_state(crk_h, crv_h, lg_f, False)
    s_b = retention_final_state(crk_h, crv_h, lg_b, True)
    r = retention_output(bidir_retention(rq_h, rk_h, rv_h, lg_f, lg_b, s_f, s_b), rg, ret_norm_w)
    out = jnp.concatenate([a, b, r], axis=-1) @ w_out
    if not need_ctx:
        return out, None
    zero = jnp.zeros_like(s_f)
    a_c = diff_head_norm(diff_softmax_attend(cq, ck, cv, lam), diff_norm_w, lam_init)
    b_c = multiscale_pool(cu, pool_w, pool_scale)
    r_c = retention_output(bidir_retention(crq_h, crk_h, crv_h, lg_f, lg_b, zero, zero), crg, ret_norm_w)
    out_c = jnp.concatenate([a_c, b_c, r_c], axis=-1) @ w_out
    return out, out_c


def grouped_moe(h, router_w, router_b, w_gate, w_up, w_down):
    T, D = h.shape
    s = jax.nn.sigmoid(h.astype(jnp.float32) @ router_w.astype(jnp.float32))
    s_sel = (s + router_b.astype(jnp.float32)).reshape(T, N_GROUPS, EXPERTS_PER_GROUP)
    grp_score = lax.top_k(s_sel, TOP_K)[0].sum(-1)
    g_idx = jnp.argmax(grp_score, axis=-1)
    in_grp = jnp.take_along_axis(s_sel, g_idx[:, None, None], axis=1)[:, 0]
    loc = lax.top_k(in_grp, TOP_K)[1]
    e_idx = g_idx[:, None] * EXPERTS_PER_GROUP + loc
    s_top = jnp.take_along_axis(s, e_idx, axis=1)
    gates = s_top / jnp.sum(s_top, axis=-1, keepdims=True)
    A = T * TOP_K
    flat_e = e_idx.reshape(A).astype(jnp.int32)
    flat_tok = jnp.repeat(jnp.arange(T, dtype=jnp.int32), TOP_K)
    order = jnp.argsort(flat_e)
    se = flat_e[order]
    stok = flat_tok[order]
    sg = gates.reshape(A)[order]
    counts = jax.ops.segment_sum(jnp.ones((A,), jnp.int32), flat_e, num_segments=N_EXPERTS)
    padded = (counts + MOE_BLOCK - 1) // MOE_BLOCK * MOE_BLOCK
    start = jnp.cumsum(counts) - counts
    pend = jnp.cumsum(padded)
    pstart = pend - padded
    dest = pstart[se] + jnp.arange(A, dtype=jnp.int32) - start[se]
    n_blk = -(-A // MOE_BLOCK) + N_EXPERTS
    xs = jnp.zeros((n_blk * MOE_BLOCK, D), h.dtype).at[dest].set(h[stok])
    blk_e = jnp.minimum(jnp.searchsorted(pend, jnp.arange(n_blk, dtype=jnp.int32) * MOE_BLOCK, side='right'),
                        N_EXPERTS - 1)

    def expert_block(args):
        xb, e = args
        return (jax.nn.silu(xb @ w_gate[e]) * (xb @ w_up[e])) @ w_down[e]

    ys = lax.map(expert_block, (xs.reshape(n_blk, MOE_BLOCK, D), blk_e)).reshape(n_blk * MOE_BLOCK, D)
    return jax.ops.segment_sum(ys[dest] * sg[:, None].astype(ys.dtype), stok, num_segments=T)


def setup_inputs(seed: int = 0) -> dict:
    key = jax.random.key(seed)
    ks = jax.random.split(key, 24)
    D = D_MODEL
    nrm = lambda k, shape, scale: jax.random.normal(k, shape, jnp.float32) * scale
    base_decay = -(5.0 + jnp.arange(RET_HEADS, dtype=jnp.float32)) * math.log(2.0)
    return {
        'x': nrm(ks[0], (BATCH, SEQ, D), 1.0),
        'c': nrm(ks[1], (BATCH, D), 1.0),
        'ctx': nrm(ks[2], (BATCH, CTX_LEN, D), 1.0),
        'c_ctx': nrm(ks[3], (D,), 1.0),
        'w_in': nrm(ks[4], (DEPTH, D, IN_WIDTH), D ** -0.5),
        'w_out': nrm(ks[5], (DEPTH, MIX_WIDTH, D), MIX_WIDTH ** -0.5),
        'ada_w': nrm(ks[6], (DEPTH, D, N_MOD * D), 0.5 * D ** -0.5),
        'ada_b': nrm(ks[7], (DEPTH, N_MOD * D), 0.02),
        'norm1_w': 1.0 + nrm(ks[8], (DEPTH, D), 0.05),
        'norm2_w': 1.0 + nrm(ks[9], (DEPTH, D), 0.05),
        'diff_lambda': nrm(ks[10], (DEPTH, 4, HEAD_DIM), 0.1),
        'diff_norm_w': 1.0 + nrm(ks[11], (DEPTH, DIFF_V_DIM), 0.05),
        'pool_w': nrm(ks[12], (DEPTH, len(POOL_WINDOWS), POOL_GROUP, POOL_GROUP), POOL_GROUP ** -0.5),
        'pool_scale': 1.0 + nrm(ks[13], (DEPTH, POOL_WIDTH), 0.1),
        'ret_a_f': base_decay + nrm(ks[14], (DEPTH, RET_HEADS), 0.1),
        'ret_a_b': base_decay + nrm(ks[15], (DEPTH, RET_HEADS), 0.1),
        'ret_norm_w': 1.0 + nrm(ks[16], (DEPTH, RET_WIDTH), 0.05),
        'router_w': nrm(ks[17], (D, N_EXPERTS), D ** -0.5),
        'router_b': nrm(ks[18], (N_EXPERTS,), 0.01),
        'moe_w_gate': nrm(ks[19], (DEPTH, N_EXPERTS, D, D_EXPERT), D ** -0.5),
        'moe_w_up': nrm(ks[20], (DEPTH, N_EXPERTS, D, D_EXPERT), D ** -0.5),
        'moe_w_down': nrm(ks[21], (DEPTH, N_EXPERTS, D_EXPERT, D), D_EXPERT ** -0.5),
        'final_norm_w': 1.0 + nrm(ks[22], (D,), 0.05),
    }


def reference(x, c, ctx, c_ctx, w_in, w_out, ada_w, ada_b, norm1_w, norm2_w, diff_lambda, diff_norm_w,
              pool_w, pool_scale, ret_a_f, ret_a_b, ret_norm_w, router_w, router_b,
              moe_w_gate, moe_w_up, moe_w_down, final_norm_w):
    B, L, D = x.shape
    Lc = ctx.shape[1]
    cos, sin = axial_rope_tables(L)
    xc = ctx
    for l in range(DEPTH):
        last = l == DEPTH - 1
        lam_init = 0.8 - 0.6 * math.exp(-0.3 * l)
        mod = jax.nn.silu(c) @ ada_w[l] + ada_b[l]
        mod_c = jax.nn.silu(c_ctx) @ ada_w[l] + ada_b[l]
        sh1, sc1, g1, sh2, sc2, g2 = jnp.split(mod[:, None, :], N_MOD, axis=-1)
        csh1, csc1, cg1, csh2, csc2, cg2 = jnp.split(mod_c, N_MOD, axis=-1)
        h = modulate(rmsnorm(x, norm1_w[l]), sh1, sc1)
        hc = modulate(rmsnorm(xc, norm1_w[l]), csh1, csc1)
        y, yc = hybrid_mixer(h, hc, cos, sin, w_in[l], w_out[l], diff_lambda[l], diff_norm_w[l], lam_init,
                             pool_w[l], pool_scale[l], ret_a_f[l], ret_a_b[l], ret_norm_w[l], not last)
        x = x + g1 * y
        h = modulate(rmsnorm(x, norm2_w[l]), sh2, sc2)
        if last:
            f = grouped_moe(h.reshape(B * L, D), router_w, router_b, moe_w_gate[l], moe_w_up[l], moe_w_down[l])
            x = x + g2 * f.reshape(B, L, D)
        else:
            xc = xc + cg1 * yc
            hc = modulate(rmsnorm(xc, norm2_w[l]), csh2, csc2)
            tokens = jnp.concatenate([hc.reshape(B * Lc, D), h.reshape(B * L, D)], axis=0)
            f = grouped_moe(tokens, router_w, router_b, moe_w_gate[l], moe_w_up[l], moe_w_down[l])
            xc = xc + cg2 * f[:B * Lc].reshape(B, Lc, D)
            x = x + g2 * f[B * Lc:].reshape(B, L, D)
    return rmsnorm(x, final_norm_w)
```

```python
import functools
import math

import jax
import jax.numpy as jnp
from jax import lax
from jax.experimental import pallas as pl
from jax.experimental.pallas import tpu as pltpu

F32 = jnp.float32
MXU_DTYPE = jnp.bfloat16
ACT_DTYPE = jnp.bfloat16

GRID_W = 64
HEAD_DIM = 64
DIFF_HEADS = 4
DIFF_V_DIM = 2 * HEAD_DIM
DIFF_WIDTH = DIFF_HEADS * DIFF_V_DIM
POOL_WINDOWS = (2, 4, 8, 16)
POOL_GROUP = 64
POOL_WIDTH = POOL_GROUP * len(POOL_WINDOWS)
RET_HEADS = 4
RET_WIDTH = RET_HEADS * HEAD_DIM
OFF_DQ = 0
OFF_DK = OFF_DQ + DIFF_WIDTH
OFF_DV = OFF_DK + DIFF_WIDTH
OFF_PU = OFF_DV + DIFF_WIDTH
OFF_RQ = OFF_PU + POOL_WIDTH
OFF_RK = OFF_RQ + RET_WIDTH
OFF_RV = OFF_RK + RET_WIDTH
OFF_RG = OFF_RV + RET_WIDTH
IN_WIDTH = OFF_RG + RET_WIDTH
ROPE_BASE = 10000.0
ROPE_FREQS = HEAD_DIM // 4
N_EXPERTS = 16
N_GROUPS = 4
EXPERTS_PER_GROUP = N_EXPERTS // N_GROUPS
PAIRS_PER_GROUP = EXPERTS_PER_GROUP * (EXPERTS_PER_GROUP - 1) // 2
N_CLASSES = N_GROUPS * PAIRS_PER_GROUP
N_MOD = 6
EPS = 1e-6

LANES = 128
MOD_ROWS = 16
TOKEN_TILE = 512
SEQ_TILE = 256
HALO = 16
MOE_TILE = 256
COL = 256
ROUTE_ROWS = 8
VMEM_LIMIT = 56 * 1024 * 1024


def _mm(a, b):
    return jnp.dot(a.astype(MXU_DTYPE), b.astype(MXU_DTYPE), preferred_element_type=F32)


def _mm_nt(a, b):
    return lax.dot_general(a.astype(MXU_DTYPE), b.astype(MXU_DTYPE), (((1,), (1,)), ((), ())),
                           preferred_element_type=F32)


def _mm_tn(a, b):
    return lax.dot_general(a.astype(MXU_DTYPE), b.astype(MXU_DTYPE), (((0,), (0,)), ((), ())),
                           preferred_element_type=F32)


def _split(x):
    hi = x.astype(MXU_DTYPE)
    lo = (x - hi.astype(F32)).astype(MXU_DTYPE)
    return hi, lo


def _mm_hi(a, b):
    ah, al = _split(a)
    bh, bl = _split(b)
    d = lambda u, v: jnp.dot(u, v, preferred_element_type=F32)
    return d(ah, bh) + (d(ah, bl) + d(al, bh))


def _sigmoid(x):
    return 1.0 / (1.0 + jnp.exp(-x))


def _rms(x):
    return x * lax.rsqrt(jnp.mean(x * x, axis=-1, keepdims=True) + EPS)


def _cparams(*sem):
    return pltpu.CompilerParams(dimension_semantics=sem, vmem_limit_bytes=VMEM_LIMIT)


def _ada_body(c_ref, w_ref, b_ref, o_ref):
    c = c_ref[...]
    o_ref[...] = _mm_hi(c * _sigmoid(c), w_ref[...]) + b_ref[...]


def _ada_table(c_all, ada_w, ada_b):
    depth, d, n = ada_w.shape
    nb = n // N_MOD
    return pl.pallas_call(
        _ada_body,
        grid=(depth, n // nb),
        in_specs=[pl.BlockSpec((MOD_ROWS, d), lambda l, j: (0, 0)),
                  pl.BlockSpec((None, d, nb), lambda l, j: (l, 0, j)),
                  pl.BlockSpec((None, 1, nb), lambda l, j: (l, 0, j))],
        out_specs=pl.BlockSpec((None, MOD_ROWS, nb), lambda l, j: (l, 0, j)),
        out_shape=jax.ShapeDtypeStruct((depth, MOD_ROWS, n), F32),
        compiler_params=_cparams("arbitrary", "arbitrary"),
        name="ada_table",
    )(c_all, ada_w, ada_b.reshape(depth, 1, n))


class _Layout:
    def __init__(self, B, L, Lc, D):
        self.B, self.L, self.Lc, self.D = B, L, Lc, D
        self.NL, self.NC = B * L, B * Lc
        self.T = self.NL + self.NC
        assert L % TOKEN_TILE == 0 and self.NC % TOKEN_TILE == 0
        assert L % SEQ_TILE == 0 and Lc == SEQ_TILE and L % GRID_W == 0
        self.n_lat_tiles = self.NL // TOKEN_TILE
        self.tiles_per_seq = L // TOKEN_TILE

    def mod_row(self, i):
        return jnp.where(i < self.n_lat_tiles, i // self.tiles_per_seq, self.B)

    def rope_block(self, i):
        return jnp.where(i < self.n_lat_tiles, i % self.tiles_per_seq, self.tiles_per_seq)

    def mod_spec(self, layer, k):
        base = layer * MOD_ROWS * N_MOD + k
        return pl.BlockSpec((1, 1, self.D), lambda i: (base + self.mod_row(i) * N_MOD, 0, 0))


def _rope_tables(lay):
    L = lay.L
    rows = L // GRID_W
    row = jnp.repeat(jnp.arange(rows), GRID_W).astype(F32)
    col = jnp.tile(jnp.arange(GRID_W), rows).astype(F32)
    inv = ROPE_BASE ** (-jnp.arange(ROPE_FREQS, dtype=F32) / ROPE_FREQS)
    ang_r = row[:, None] * inv
    ang_c = col[:, None] * inv
    ang = jnp.concatenate([ang_r, ang_r, ang_c, ang_c], axis=-1)
    cos, sin = jnp.cos(ang), jnp.sin(ang)
    first_half = (jnp.arange(HEAD_DIM) % (2 * ROPE_FREQS)) < ROPE_FREQS
    sin_a = jnp.where(first_half, -sin, 0.0)
    sin_b = jnp.where(first_half, 0.0, sin)
    rep = LANES // HEAD_DIM
    ident = [jnp.ones((TOKEN_TILE, LANES), F32), jnp.zeros((TOKEN_TILE, LANES), F32),
             jnp.zeros((TOKEN_TILE, LANES), F32)]
    return [jnp.concatenate([jnp.tile(t, (1, rep)), e], axis=0) for t, e in zip((cos, sin_a, sin_b), ident)]


_ROPE_COLS = tuple(range(OFF_DQ // COL, OFF_DV // COL)) + (OFF_RQ // COL, OFF_RK // COL)
_SCALED_COLS = tuple(range(OFF_DQ // COL, OFF_DK // COL)) + (OFF_RK // COL,)


def _inproj_body(*refs, with_f):
    if with_f:
        (x_ref, f_ref, g2_ref, nw_ref, sh_ref, sc_ref, w_ref, cos_ref, sa_ref, sb_ref, p_ref, xo_ref) = refs
        x = x_ref[...] + g2_ref[0] * f_ref[...]
        xo_ref[...] = x
    else:
        (x_ref, nw_ref, sh_ref, sc_ref, w_ref, cos_ref, sa_ref, sb_ref, p_ref) = refs
        x = x_ref[...]
    h = (_rms(x) * nw_ref[...]) * (1.0 + sc_ref[0]) + sh_ref[0]
    hb = h.astype(MXU_DTYPE)
    cos, sa, sb = cos_ref[...], sa_ref[...], sb_ref[...]
    for j in range(IN_WIDTH // COL):
        acc = jnp.dot(hb, w_ref[:, j * COL:(j + 1) * COL], preferred_element_type=F32)
        if j in _ROPE_COLS:
            parts = []
            for t in range(COL // LANES):
                a = acc[:, t * LANES:(t + 1) * LANES]
                parts.append(a * cos + pltpu.roll(a, LANES - ROPE_FREQS, 1) * sa + pltpu.roll(a, ROPE_FREQS, 1) * sb)
            acc = jnp.concatenate(parts, axis=1)
        if j in _SCALED_COLS:
            acc = acc * (HEAD_DIM ** -0.5)
        p_ref[:, j * COL:(j + 1) * COL] = acc.astype(p_ref.dtype)


def _inproj(lay, layer, x, f, mods3, nw, w_in, tables):
    D, TM = lay.D, TOKEN_TILE
    row = lambda i: (i, 0)
    const = lambda i: (0, 0)
    tbl = pl.BlockSpec((TM, LANES), lambda i: (lay.rope_block(i), 0))
    in_specs = [pl.BlockSpec((TM, D), row)]
    args = [x]
    if f is not None:
        in_specs += [pl.BlockSpec((TM, D), row), lay.mod_spec(layer - 1, 5)]
        args += [f, mods3]
    in_specs += [pl.BlockSpec((1, D), const), lay.mod_spec(layer, 0), lay.mod_spec(layer, 1),
                 pl.BlockSpec((D, IN_WIDTH), const), tbl, tbl, tbl]
    args += [nw.reshape(1, D), mods3, mods3, w_in] + list(tables)
    out_specs = [pl.BlockSpec((TM, IN_WIDTH), row)]
    out_shape = [jax.ShapeDtypeStruct((lay.T, IN_WIDTH), ACT_DTYPE)]
    if f is not None:
        out_specs.append(pl.BlockSpec((TM, D), row))
        out_shape.append(jax.ShapeDtypeStruct((lay.T, D), F32))
    outs = pl.pallas_call(
        functools.partial(_inproj_body, with_f=f is not None),
        grid=(lay.T // TM,), in_specs=in_specs, out_specs=out_specs, out_shape=out_shape,
        compiler_params=_cparams("arbitrary"), name="inproj",
    )(*args)
    return (outs[0], outs[1]) if f is not None else (outs[0], x)


def _attend(q, ks, vs, lam, w, post_scale):
    lane = lax.broadcasted_iota(jnp.int32, (1, DIFF_V_DIM), 1)
    zero = jnp.zeros_like(q)
    es, rs = [], []
    for m in range(2):
        qm = jnp.where((lane >= m * HEAD_DIM) & (lane < (m + 1) * HEAD_DIM), q, zero)
        s = [_mm_nt(qm, k) for k in ks]
        mx = functools.reduce(jnp.maximum, [jnp.max(t, axis=-1, keepdims=True) for t in s])
        e = [jnp.exp(t - mx) for t in s]
        es.append(e)
        rs.append(1.0 / functools.reduce(jnp.add, [jnp.sum(t, axis=-1, keepdims=True) for t in e]))
    r1 = rs[0]
    r2 = rs[1] * lam
    o = functools.reduce(jnp.add, [_mm(e1 * r1 - e2 * r2, v) for e1, e2, v in zip(es[0], es[1], vs)])
    return (_rms(o) * w) * post_scale


def _attn_body(lam_ref, q_ref, kc_ref, kl_ref, vc_ref, vl_ref, w_ref, o_ref, *, n_lat, post_scale):
    i = pl.program_id(2)
    lam = lam_ref[0]

    @pl.when(i < n_lat)
    def _():
        o = _attend(q_ref[...], [kc_ref[...], kl_ref[...]], [vc_ref[...], vl_ref[...]], lam, w_ref[...], post_scale)
        o_ref[...] = o.astype(o_ref.dtype)

    @pl.when(i >= n_lat)
    def _():
        o = _attend(q_ref[...], [kc_ref[...]], [vc_ref[...]], lam, w_ref[...], post_scale)
        o_ref[...] = o.astype(o_ref.dtype)


def _diff_attention(lay, p, lam, norm_w, post_scale, need_ctx):
    B, L, Lc = lay.B, lay.L, lay.Lc
    W, tq = DIFF_V_DIM, SEQ_TILE
    nq = L // tq
    k_blk, v_blk = OFF_DK // W, OFF_DV // W
    ctx0 = lay.NL // Lc
    q_map = lambda b, h, i: (jnp.where(i < nq, b * nq + i, ctx0 + b), h)
    in_specs = [pl.BlockSpec(memory_space=pltpu.SMEM),
                pl.BlockSpec((tq, W), q_map),
                pl.BlockSpec((Lc, W), lambda b, h, i: (ctx0 + b, k_blk + h)),
                pl.BlockSpec((L, W), lambda b, h, i: (b, k_blk + h)),
                pl.BlockSpec((Lc, W), lambda b, h, i: (ctx0 + b, v_blk + h)),
                pl.BlockSpec((L, W), lambda b, h, i: (b, v_blk + h)),
                pl.BlockSpec((1, W), lambda b, h, i: (0, 0))]
    return pl.pallas_call(
        functools.partial(_attn_body, n_lat=nq, post_scale=post_scale),
        grid=(B, DIFF_HEADS, nq + (1 if need_ctx else 0)), in_specs=in_specs,
        out_specs=pl.BlockSpec((tq, W), q_map),
        out_shape=jax.ShapeDtypeStruct((lay.T if need_ctx else lay.NL, DIFF_WIDTH), ACT_DTYPE),
        compiler_params=_cparams("arbitrary", "arbitrary", "arbitrary"),
        name="diff_attn",
    )(lam, p, p, p, p, p, norm_w.reshape(1, W))


def _pool_body(prev_ref, cur_ref, next_ref, w_ref, scale_ref, o_ref, *, lay):
    i = pl.program_id(0)
    n_lat = lay.NL // SEQ_TILE
    per_seq = jnp.where(i < n_lat, lay.L // SEQ_TILE, lay.Lc // SEQ_TILE)
    idx = jnp.where(i < n_lat, i, i - n_lat) % per_seq
    seq_len = per_seq * SEQ_TILE
    has_prev = (idx > 0).astype(F32)
    has_next = (idx < per_seq - 1).astype(F32)
    cur = cur_ref[...].astype(F32)
    u = jnp.concatenate([prev_ref[...].astype(F32) * has_prev, cur, next_ref[...].astype(F32) * has_next], axis=0)
    n = SEQ_TILE + 2 * HALO
    sums = {1: u}
    w = 1
    while w < POOL_WINDOWS[-1]:
        s = sums[w]
        m = s.shape[0] - w
        sums[2 * w] = s[:m] + s[w:w + m]
        w *= 2
    pos = idx * SEQ_TILE + lax.broadcasted_iota(jnp.int32, (SEQ_TILE, 1), 0)
    group = lax.broadcasted_iota(jnp.int32, (1, POOL_WIDTH), 1) // POOL_GROUP
    mean = jnp.zeros((SEQ_TILE, POOL_WIDTH), F32)
    for g, win in enumerate(POOL_WINDOWS):
        start = HALO - win // 2
        cnt = jnp.minimum(pos + (win - win // 2), seq_len) - jnp.maximum(pos - win // 2, 0)
        mean = jnp.where(group == g, sums[win][start:start + SEQ_TILE] / cnt.astype(F32), mean)
    o_ref[...] = (_mm(mean - cur, w_ref[...]) * scale_ref[...]).astype(o_ref.dtype)


def _pool(lay, p, w_bd, scale):
    n = lay.T // SEQ_TILE
    col = OFF_PU // POOL_WIDTH
    per = SEQ_TILE // HALO
    last = lay.T // HALO - 1
    return pl.pallas_call(
        functools.partial(_pool_body, lay=lay),
        grid=(n,),
        in_specs=[pl.BlockSpec((HALO, POOL_WIDTH), lambda i: (jnp.maximum(i * per - 1, 0), col)),
                  pl.BlockSpec((SEQ_TILE, POOL_WIDTH), lambda i: (i, col)),
                  pl.BlockSpec((HALO, POOL_WIDTH), lambda i: (jnp.minimum((i + 1) * per, last), col)),
                  pl.BlockSpec((POOL_WIDTH, POOL_WIDTH), lambda i: (0, 0)),
                  pl.BlockSpec((1, POOL_WIDTH), lambda i: (0, 0))],
        out_specs=pl.BlockSpec((SEQ_TILE, POOL_WIDTH), lambda i: (i, 0)),
        out_shape=jax.ShapeDtypeStruct((lay.T, POOL_WIDTH), ACT_DTYPE),
        compiler_params=_cparams("arbitrary"), name="pool",
    )(p, p, p, w_bd, scale.reshape(1, POOL_WIDTH))


def _ret_body(lgs_ref, lgv_ref, q_ref, k_ref, v_ref, *refs, reverse):
    C = SEQ_TILE
    if reverse:
        g_ref, yf_ref, nw_ref, o_ref, s_ref, d_ref, xi_ref, zeta_ref = refs
    else:
        o_ref, s_ref, d_ref, xi_ref, zeta_ref = refs
    b, j = pl.program_id(0), pl.program_id(1)
    lane_head = lax.broadcasted_iota(jnp.int32, (1, RET_WIDTH), 1) // HEAD_DIM
    lgv = lgv_ref[...]

    @pl.when((b == 0) & (j == 0))
    def _():
        ri = lax.broadcasted_iota(jnp.int32, (C, C), 0)
        ci = lax.broadcasted_iota(jnp.int32, (C, C), 1)
        dist = (ci - ri if reverse else ri - ci).astype(F32)
        keep = dist > 0 if reverse else dist >= 0
        for h in range(RET_HEADS):
            d_ref[h * C:(h + 1) * C, :] = jnp.where(keep, jnp.exp(lgs_ref[h] * jnp.maximum(dist, 0.0)), 0.0)
        t = lax.broadcasted_iota(jnp.int32, (C, 1), 0).astype(F32)
        xi_ref[...] = jnp.exp(lgv * ((C - t) if reverse else (t + 1.0)))
        zeta_ref[...] = jnp.exp(lgv * (t if reverse else (C - 1.0 - t)))

    @pl.when(j == 0)
    def _():
        s_ref[...] = jnp.zeros_like(s_ref)

    q, k, v = q_ref[...], k_ref[...], v_ref[...]
    zero = jnp.zeros_like(q)
    qs = jnp.concatenate([jnp.where(lane_head == h, q, zero) for h in range(RET_HEADS)], axis=0)
    sd = _mm_nt(qs, k) * d_ref[...]
    yv = _mm(sd, v)
    y = functools.reduce(jnp.add, [jnp.where(lane_head == h, yv[h * C:(h + 1) * C], 0.0) for h in range(RET_HEADS)])
    state = s_ref[...]
    y = y + _mm(q.astype(F32) * xi_ref[...], state)
    kv = _mm_tn(k.astype(F32) * zeta_ref[...], v)
    row_head = lax.broadcasted_iota(jnp.int32, (RET_WIDTH, 1), 0) // HEAD_DIM
    s_ref[...] = state * jnp.exp(lgv * C) + jnp.where(row_head == lane_head, kv, 0.0)

    if reverse:
        y = y + yf_ref[...]
        same = (row_head == lane_head).astype(F32) * (1.0 / HEAD_DIM)
        ms = _mm_hi(y * y, same)
        yn = y * lax.rsqrt(ms + EPS) * nw_ref[...]
        g = g_ref[...].astype(F32)
        o_ref[...] = ((g * _sigmoid(g)) * yn).astype(o_ref.dtype)
    else:
        o_ref[...] = y


def _retention_pass(lay, p, lgs, lgv, reverse, extra=()):
    B, L = lay.B, lay.L
    C, W = SEQ_TILE, RET_WIDTH
    nch = L // C
    ctx0 = lay.NL // C

    def rows(b, j):
        lat = b * nch + (nch - j if reverse else j - 1)
        return jnp.where(j == 0, ctx0 + b, lat)

    col = lambda c: pl.BlockSpec((C, W), lambda b, j: (rows(b, j), c))
    in_specs = [pl.BlockSpec(memory_space=pltpu.SMEM), pl.BlockSpec((1, W), lambda b, j: (0, 0)),
                col(OFF_RQ // W), col(OFF_RK // W), col(OFF_RV // W)]
    args = [lgs, lgv, p, p, p]
    if reverse:
        yf, nw = extra
        in_specs += [col(OFF_RG // W), col(0), pl.BlockSpec((1, W), lambda b, j: (0, 0))]
        args += [p, yf, nw.reshape(1, W)]
    return pl.pallas_call(
        functools.partial(_ret_body, reverse=reverse),
        grid=(B, nch + 1), in_specs=in_specs, out_specs=col(0),
        out_shape=jax.ShapeDtypeStruct((lay.T, W), ACT_DTYPE if reverse else F32),
        scratch_shapes=[pltpu.VMEM((W, W), F32), pltpu.VMEM((RET_HEADS * C, C), F32),
                        pltpu.VMEM((C, W), F32), pltpu.VMEM((C, W), F32)],
        compiler_params=_cparams("arbitrary", "arbitrary"),
        name="retention_bwd" if reverse else "retention_fwd",
    )(*args)


def _top2(vals):
    n = len(vals)
    v1 = functools.reduce(jnp.maximum, vals)
    i1 = jnp.full_like(v1, n - 1)
    for e in range(n - 2, -1, -1):
        i1 = jnp.where(vals[e] == v1, float(e), i1)
    rest = [jnp.where(i1 == float(e), -jnp.inf, vals[e]) for e in range(n)]
    v2 = functools.reduce(jnp.maximum, rest)
    i2 = jnp.full_like(v1, n - 1)
    for e in range(n - 2, -1, -1):
        i2 = jnp.where(rest[e] == v2, float(e), i2)
    return v1, i1, v2, i2


def _route(logits_t, bias):
    s = _sigmoid(logits_t)
    sel = s + bias
    groups = []
    for g in range(N_GROUPS):
        rows = [sel[g * EXPERTS_PER_GROUP + e:g * EXPERTS_PER_GROUP + e + 1] for e in range(EXPERTS_PER_GROUP)]
        groups.append(_top2(rows))
    score = [v1 + v2 for v1, _, v2, _ in groups]
    best = functools.reduce(jnp.maximum, score)
    gi = jnp.full_like(best, N_GROUPS - 1)
    for g in range(N_GROUPS - 2, -1, -1):
        gi = jnp.where(score[g] == best, float(g), gi)
    pick = lambda k: functools.reduce(
        lambda acc, g: jnp.where(gi == float(g), groups[g][k], acc), range(N_GROUPS - 1), groups[N_GROUPS - 1][k])
    e0 = gi * EXPERTS_PER_GROUP + pick(1)
    e1 = gi * EXPERTS_PER_GROUP + pick(3)
    s0 = jnp.zeros_like(best)
    s1 = jnp.zeros_like(best)
    for e in range(N_EXPERTS):
        s0 = jnp.where(e0 == float(e), s[e:e + 1], s0)
        s1 = jnp.where(e1 == float(e), s[e:e + 1], s1)
    tot = s0 + s1
    return e0, e1, s0 / tot, s1 / tot


def _outproj_body(x_ref, a_ref, b_ref, r_ref, w_ref, g1_ref, nw_ref, sh_ref, sc_ref, rw_ref, rb_ref,
                  xo_ref, h_ref, route_ref):
    y = (_mm(a_ref[...], w_ref[:DIFF_WIDTH, :])
         + _mm(b_ref[...], w_ref[DIFF_WIDTH:DIFF_WIDTH + POOL_WIDTH, :])
         + _mm(r_ref[...], w_ref[DIFF_WIDTH + POOL_WIDTH:, :]))
    x = x_ref[...] + g1_ref[0] * y
    xo_ref[...] = x
    h = (_rms(x) * nw_ref[...]) * (1.0 + sc_ref[0]) + sh_ref[0]
    h_ref[...] = h
    logits_t = jnp.transpose(_mm_hi(h, rw_ref[...]))[:N_EXPERTS]
    e0, e1, g0, g1 = _route(logits_t, rb_ref[...])
    route_ref[...] = jnp.concatenate([e0, e1, g0, g1] + [jnp.zeros_like(e0)] * (ROUTE_ROWS - 4), axis=0)


def _outproj(lay, layer, n_tok, x, a, b, r, w_out, mods3, nw, rw_pad, rb_col):
    D, TM = lay.D, TOKEN_TILE
    row = lambda i: (i, 0)
    const = lambda i: (0, 0)
    tile = lambda w: pl.BlockSpec((TM, w), row)
    in_specs = [tile(D), tile(DIFF_WIDTH), tile(POOL_WIDTH), tile(RET_WIDTH),
                pl.BlockSpec((DIFF_WIDTH + POOL_WIDTH + RET_WIDTH, D), const),
                lay.mod_spec(layer, 2), pl.BlockSpec((1, D), const), lay.mod_spec(layer, 3), lay.mod_spec(layer, 4),
                pl.BlockSpec((D, LANES), const), pl.BlockSpec((N_EXPERTS, 1), const)]
    return pl.pallas_call(
        _outproj_body,
        grid=(n_tok // TM,), in_specs=in_specs,
        out_specs=[tile(D), tile(D), pl.BlockSpec((ROUTE_ROWS, TM), lambda i: (0, i))],
        out_shape=[jax.ShapeDtypeStruct((n_tok, D), F32), jax.ShapeDtypeStruct((n_tok, D), F32),
                   jax.ShapeDtypeStruct((ROUTE_ROWS, n_tok), F32)],
        compiler_params=_cparams("arbitrary"), name="outproj_router",
    )(x, a, b, r, w_out, mods3, nw.reshape(1, D), mods3, mods3, rw_pad, rb_col)


def _moe_body(elo_ref, ehi_ref, nvalid_ref, src_ref, src_next_ref, dst_ref, glo_ref, ghi_ref,
              wg_lo, wu_lo, wd_lo, wg_hi, wu_hi, wd_hi, h_hbm, f_hbm, xbuf, ybuf, sem_in, sem_out, *, n_tok):
    TB = MOE_TILE
    j = pl.program_id(0)
    nb = pl.num_programs(0)
    slot = j % 2
    nvalid = nvalid_ref[0]

    def gather_copy(row, s, i):
        return pltpu.make_async_copy(h_hbm.at[pl.ds(row, 1)], xbuf.at[s, pl.ds(i, 1)], sem_in.at[s])

    def scatter_copy(row, s, i):
        return pltpu.make_async_copy(ybuf.at[s, pl.ds(i, 1)], f_hbm.at[pl.ds(row, 1)], sem_out.at[s])

    def wait_scatter(s):
        for i in range(TB):
            scatter_copy(0, s, i).wait()

    @pl.when(j == 0)
    def _():
        ybuf[...] = jnp.zeros_like(ybuf)
        fills = [pltpu.make_async_copy(ybuf.at[s], f_hbm.at[pl.ds(n_tok + s * TB, TB)], sem_out.at[s]) for s in range(2)]
        for cp in fills:
            cp.start()
        for cp in fills:
            cp.wait()
        for i in range(TB):
            gather_copy(src_ref[0, 0, i], 0, i).start()

    @pl.when(j + 1 < nb)
    def _():
        for i in range(TB):
            gather_copy(src_next_ref[0, 0, i], 1 - slot, i).start()

    for i in range(TB):
        gather_copy(0, slot, i).wait()

    @pl.when(j < nvalid)
    def _():
        @pl.when(j >= 2)
        def _():
            wait_scatter(slot)

        xb = xbuf[slot].astype(MXU_DTYPE)

        def ffn(wg, wu, wd):
            hg = jnp.dot(xb, wg[...], preferred_element_type=F32)
            hu = jnp.dot(xb, wu[...], preferred_element_type=F32)
            return _mm((hg * _sigmoid(hg)) * hu, wd[...])

        ybuf[slot] = glo_ref[...] * ffn(wg_lo, wu_lo, wd_lo) + ghi_ref[...] * ffn(wg_hi, wu_hi, wd_hi)
        for i in range(TB):
            scatter_copy(dst_ref[0, 0, i], slot, i).start()

    @pl.when(j == nb - 1)
    def _():
        @pl.when(nvalid >= 1)
        def _():
            wait_scatter((nvalid - 1) % 2)

        @pl.when(nvalid >= 2)
        def _():
            wait_scatter(nvalid % 2)


def _moe(n_tok, D, h2, route, wg, wu, wd):
    TB = MOE_TILE
    DE = wg.shape[-1]
    e0, e1 = route[0].astype(jnp.int32), route[1].astype(jnp.int32)
    lo, hi = jnp.minimum(e0, e1), jnp.maximum(e0, e1)
    g_lo = jnp.where(e0 <= e1, route[2], route[3])
    g_hi = jnp.where(e0 <= e1, route[3], route[2])
    grp = lo // EXPERTS_PER_GROUP
    a, b = lo % EXPERTS_PER_GROUP, hi % EXPERTS_PER_GROUP
    cls = grp * PAIRS_PER_GROUP + a * (2 * EXPERTS_PER_GROUP - 1 - a) // 2 + (b - a - 1)
    onehot = (cls[:, None] == jnp.arange(N_CLASSES)[None, :]).astype(jnp.int32)
    csum = jnp.cumsum(onehot, axis=0)
    counts = csum[-1]
    rank = jnp.sum(onehot * csum, axis=1) - 1
    padded = (counts + TB - 1) // TB * TB
    pend = jnp.cumsum(padded)
    pstart = pend - padded
    pos = pstart[cls] + rank
    n_blk = -(-n_tok // TB) + N_CLASSES
    P = n_blk * TB
    tok = jnp.full((P,), -1, jnp.int32).at[pos].set(jnp.arange(n_tok, dtype=jnp.int32))
    slot_row = jnp.arange(P, dtype=jnp.int32)
    dump = n_tok + ((slot_row // TB) % 2) * TB + slot_row % TB
    src = jnp.where(tok >= 0, tok, 0).reshape(n_blk, 1, TB)
    dst = jnp.where(tok >= 0, tok, dump).reshape(n_blk, 1, TB)
    glo_s = jnp.zeros((P,), F32).at[pos].set(g_lo).reshape(P, 1)
    ghi_s = jnp.zeros((P,), F32).at[pos].set(g_hi).reshape(P, 1)
    nvalid = (pend[-1] // TB).astype(jnp.int32)
    blk = jnp.minimum(jnp.arange(n_blk, dtype=jnp.int32), nvalid - 1)
    blk_cls = jnp.minimum(jnp.searchsorted(pend, blk * TB, side='right'), N_CLASSES - 1).astype(jnp.int32)
    pair_lo = jnp.array([x for x in range(EXPERTS_PER_GROUP) for y in range(x + 1, EXPERTS_PER_GROUP)], jnp.int32)
    pair_hi = jnp.array([y for x in range(EXPERTS_PER_GROUP) for y in range(x + 1, EXPERTS_PER_GROUP)], jnp.int32)
    base = (blk_cls // PAIRS_PER_GROUP) * EXPERTS_PER_GROUP
    blk_lo = base + pair_lo[blk_cls % PAIRS_PER_GROUP]
    blk_hi = base + pair_hi[blk_cls % PAIRS_PER_GROUP]

    smem_blk = lambda f: pl.BlockSpec((1, 1, TB), f, memory_space=pltpu.SMEM)
    w_spec = lambda shape, which: pl.BlockSpec(
        (None,) + shape, (lambda j, elo, ehi, nv: (elo[j], 0, 0)) if which == 0 else (lambda j, elo, ehi, nv: (ehi[j], 0, 0)))
    gate_spec = pl.BlockSpec((TB, 1), lambda j, elo, ehi, nv: (j, 0))
    grid_spec = pltpu.PrefetchScalarGridSpec(
        num_scalar_prefetch=3, grid=(n_blk,),
        in_specs=[smem_blk(lambda j, elo, ehi, nv: (j, 0, 0)),
                  smem_blk(lambda j, elo, ehi, nv: (jnp.minimum(j + 1, n_blk - 1), 0, 0)),
                  smem_blk(lambda j, elo, ehi, nv: (j, 0, 0)),
                  gate_spec, gate_spec,
                  w_spec((D, DE), 0), w_spec((D, DE), 0), w_spec((DE, D), 0),
                  w_spec((D, DE), 1), w_spec((D, DE), 1), w_spec((DE, D), 1),
                  pl.BlockSpec(memory_space=pl.ANY)],
        out_specs=pl.BlockSpec(memory_space=pl.ANY),
        scratch_shapes=[pltpu.VMEM((2, TB, D), F32), pltpu.VMEM((2, TB, D), F32),
                        pltpu.SemaphoreType.DMA((2,)), pltpu.SemaphoreType.DMA((2,))])
    return pl.pallas_call(
        functools.partial(_moe_body, n_tok=n_tok), grid_spec=grid_spec,
        out_shape=jax.ShapeDtypeStruct((n_tok + 2 * TB, D), F32),
        compiler_params=_cparams("arbitrary"), name="moe_experts",
    )(blk_lo, blk_hi, nvalid.reshape(1), src, src, dst, glo_s, ghi_s, wg, wu, wd, wg, wu, wd, h2)


def _final_body(x_ref, f_ref, g2_ref, nw_ref, o_ref):
    o_ref[...] = _rms(x_ref[...] + g2_ref[0] * f_ref[...]) * nw_ref[...]


def _final(lay, layer, x, f, mods3, nw):
    D, TM = lay.D, TOKEN_TILE
    row = lambda i: (i, 0)
    return pl.pallas_call(
        _final_body, grid=(lay.NL // TM,),
        in_specs=[pl.BlockSpec((TM, D), row), pl.BlockSpec((TM, D), row), lay.mod_spec(layer, 5),
                  pl.BlockSpec((1, D), lambda i: (0, 0))],
        out_specs=pl.BlockSpec((TM, D), row),
        out_shape=jax.ShapeDtypeStruct((lay.NL, D), F32),
        compiler_params=_cparams("arbitrary"), name="final_norm",
    )(x, f, mods3, nw.reshape(1, D))


def kernel(x, c, ctx, c_ctx, w_in, w_out, ada_w, ada_b, norm1_w, norm2_w, diff_lambda, diff_norm_w, pool_w,
           pool_scale, ret_a_f, ret_a_b, ret_norm_w, router_w, router_b, moe_w_gate, moe_w_up, moe_w_down,
           final_norm_w):
    B, L, D = x.shape
    Lc = ctx.shape[1]
    depth = w_in.shape[0]
    lay = _Layout(B, L, Lc, D)
    assert B + 1 <= MOD_ROWS

    xa = jnp.concatenate([x.reshape(B * L, D), ctx.reshape(B * Lc, D)], axis=0)
    c_all = jnp.zeros((MOD_ROWS, D), F32).at[:B].set(c).at[B].set(c_ctx)
    mods3 = _ada_table(c_all, ada_w, ada_b).reshape(depth * MOD_ROWS * N_MOD, 1, D)
    tables = _rope_tables(lay)
    rw_pad = jnp.zeros((D, LANES), F32).at[:, :N_EXPERTS].set(router_w.astype(F32))
    rb_col = router_b.astype(F32).reshape(N_EXPERTS, 1)
    w_in_b, w_out_b = w_in.astype(MXU_DTYPE), w_out.astype(MXU_DTYPE)
    wg_b, wu_b, wd_b = moe_w_gate.astype(MXU_DTYPE), moe_w_up.astype(MXU_DTYPE), moe_w_down.astype(MXU_DTYPE)

    f = None
    for l in range(depth):
        last = l == depth - 1
        lam_init = 0.8 - 0.6 * math.exp(-0.3 * l)
        dl = diff_lambda[l].astype(F32)
        lam = (jnp.exp(jnp.sum(dl[0] * dl[1])) - jnp.exp(jnp.sum(dl[2] * dl[3])) + lam_init).reshape(1)
        p, xa = _inproj(lay, l, xa, f, mods3, norm1_w[l], w_in_b[l], tables)
        a = _diff_attention(lay, p, lam, diff_norm_w[l], 1.0 - lam_init, not last)
        eye = jnp.eye(len(POOL_WINDOWS), dtype=F32)
        w_bd = (eye[:, None, :, None] * pool_w[l][:, :, None, :]).reshape(POOL_WIDTH, POOL_WIDTH).astype(MXU_DTYPE)
        bp = _pool(lay, p, w_bd, pool_scale[l])
        lg_f = -jnp.exp(ret_a_f[l].astype(F32))
        lg_b = -jnp.exp(ret_a_b[l].astype(F32))
        lanes = lambda lg: jnp.repeat(lg, HEAD_DIM).reshape(1, RET_WIDTH)
        yf = _retention_pass(lay, p, lg_f, lanes(lg_f), False)
        r = _retention_pass(lay, p, lg_b, lanes(lg_b), True, (yf, ret_norm_w[l]))
        n_tok = lay.NL if last else lay.T
        xa, h2, route = _outproj(lay, l, n_tok, xa, a, bp, r, w_out_b[l], mods3, norm2_w[l], rw_pad, rb_col)
        f = _moe(n_tok, D, h2, route, wg_b[l], wu_b[l], wd_b[l])
    out = _final(lay, depth - 1, xa, f, mods3, final_norm_w)
    return out.reshape(B, L, D)
```

```python
import functools
import math

import jax
import jax.numpy as jnp
from jax import lax
from jax.experimental import pallas as pl
from jax.experimental.pallas import tpu as pltpu

F32 = jnp.float32
MXU_DTYPE = jnp.bfloat16
ACT_DTYPE = jnp.bfloat16

GRID_W = 64
HEAD_DIM = 64
DIFF_HEADS = 4
DIFF_V_DIM = 2 * HEAD_DIM
DIFF_WIDTH = DIFF_HEADS * DIFF_V_DIM
POOL_WINDOWS = (2, 4, 8, 16)
POOL_GROUP = 64
POOL_WIDTH = POOL_GROUP * len(POOL_WINDOWS)
RET_HEADS = 4
RET_WIDTH = RET_HEADS * HEAD_DIM
OFF_DQ = 0
OFF_DK = OFF_DQ + DIFF_WIDTH
OFF_DV = OFF_DK + DIFF_WIDTH
OFF_PU = OFF_DV + DIFF_WIDTH
OFF_RQ = OFF_PU + POOL_WIDTH
OFF_RK = OFF_RQ + RET_WIDTH
OFF_RV = OFF_RK + RET_WIDTH
OFF_RG = OFF_RV + RET_WIDTH
IN_WIDTH = OFF_RG + RET_WIDTH
ROPE_BASE = 10000.0
ROPE_FREQS = HEAD_DIM // 4
N_EXPERTS = 16
N_GROUPS = 4
EXPERTS_PER_GROUP = N_EXPERTS // N_GROUPS
PAIRS_PER_GROUP = EXPERTS_PER_GROUP * (EXPERTS_PER_GROUP - 1) // 2
N_CLASSES = N_GROUPS * PAIRS_PER_GROUP
N_MOD = 6
EPS = 1e-6

LANES = 128
MOD_ROWS = 16
TOKEN_TILE = 512
SEQ_TILE = 256
HALO = 16
MOE_TILE = 256
COL = 256
PERM_CHUNK = 512
REC_E0, REC_E1, REC_G0, REC_G1, REC_CLS, REC_RANK = range(6)
VMEM_LIMIT = 56 * 1024 * 1024


def _mm(a, b):
    return jnp.dot(a.astype(MXU_DTYPE), b.astype(MXU_DTYPE), preferred_element_type=F32)


def _mm_nt(a, b):
    return lax.dot_general(a.astype(MXU_DTYPE), b.astype(MXU_DTYPE), (((1,), (1,)), ((), ())),
                           preferred_element_type=F32)


def _mm_tn(a, b):
    return lax.dot_general(a.astype(MXU_DTYPE), b.astype(MXU_DTYPE), (((0,), (0,)), ((), ())),
                           preferred_element_type=F32)


def _split(x):
    hi = x.astype(MXU_DTYPE)
    lo = (x - hi.astype(F32)).astype(MXU_DTYPE)
    return hi, lo


def _mm_hi(a, b):
    ah, al = _split(a)
    bh, bl = _split(b)
    d = lambda u, v: jnp.dot(u, v, preferred_element_type=F32)
    return d(ah, bh) + (d(ah, bl) + d(al, bh))


def _sigmoid(x):
    return 1.0 / (1.0 + jnp.exp(-x))


def _rms(x):
    return x * lax.rsqrt(jnp.mean(x * x, axis=-1, keepdims=True) + EPS)


def _cparams(*sem):
    return pltpu.CompilerParams(dimension_semantics=sem, vmem_limit_bytes=VMEM_LIMIT)


def _ada_body(c_ref, w_ref, b_ref, o_ref):
    c = c_ref[...]
    o_ref[...] = _mm_hi(c * _sigmoid(c), w_ref[...]) + b_ref[...]


def _ada_table(c_all, ada_w, ada_b):
    depth, d, n = ada_w.shape
    nb = n // N_MOD
    return pl.pallas_call(
        _ada_body,
        grid=(depth, n // nb),
        in_specs=[pl.BlockSpec((MOD_ROWS, d), lambda l, j: (0, 0)),
                  pl.BlockSpec((None, d, nb), lambda l, j: (l, 0, j)),
                  pl.BlockSpec((None, 1, nb), lambda l, j: (l, 0, j))],
        out_specs=pl.BlockSpec((None, MOD_ROWS, nb), lambda l, j: (l, 0, j)),
        out_shape=jax.ShapeDtypeStruct((depth, MOD_ROWS, n), F32),
        compiler_params=_cparams("arbitrary", "arbitrary"),
        name="ada_table",
    )(c_all, ada_w, ada_b.reshape(depth, 1, n))


class _Layout:
    def __init__(self, B, L, Lc, D):
        self.B, self.L, self.Lc, self.D = B, L, Lc, D
        self.NL, self.NC = B * L, B * Lc
        self.T = self.NL + self.NC
        assert L % TOKEN_TILE == 0 and self.NC % TOKEN_TILE == 0
        assert L % SEQ_TILE == 0 and Lc == SEQ_TILE and L % GRID_W == 0
        self.n_lat_tiles = self.NL // TOKEN_TILE
        self.tiles_per_seq = L // TOKEN_TILE

    def mod_row(self, i):
        return jnp.where(i < self.n_lat_tiles, i // self.tiles_per_seq, self.B)

    def rope_block(self, i):
        return jnp.where(i < self.n_lat_tiles, i % self.tiles_per_seq, self.tiles_per_seq)

    def mod_spec(self, layer, k):
        base = layer * MOD_ROWS * N_MOD + k
        return pl.BlockSpec((1, 1, self.D), lambda i: (base + self.mod_row(i) * N_MOD, 0, 0))


def _rope_tables(lay):
    L = lay.L
    rows = L // GRID_W
    row = jnp.repeat(jnp.arange(rows), GRID_W).astype(F32)
    col = jnp.tile(jnp.arange(GRID_W), rows).astype(F32)
    inv = ROPE_BASE ** (-jnp.arange(ROPE_FREQS, dtype=F32) / ROPE_FREQS)
    ang_r = row[:, None] * inv
    ang_c = col[:, None] * inv
    ang = jnp.concatenate([ang_r, ang_r, ang_c, ang_c], axis=-1)
    cos, sin = jnp.cos(ang), jnp.sin(ang)
    first_half = (jnp.arange(HEAD_DIM) % (2 * ROPE_FREQS)) < ROPE_FREQS
    sin_a = jnp.where(first_half, -sin, 0.0)
    sin_b = jnp.where(first_half, 0.0, sin)
    rep = LANES // HEAD_DIM
    ident = [jnp.ones((TOKEN_TILE, LANES), F32), jnp.zeros((TOKEN_TILE, LANES), F32),
             jnp.zeros((TOKEN_TILE, LANES), F32)]
    return [jnp.concatenate([jnp.tile(t, (1, rep)), e], axis=0) for t, e in zip((cos, sin_a, sin_b), ident)]


_ROPE_COLS = tuple(range(OFF_DQ // COL, OFF_DV // COL)) + (OFF_RQ // COL, OFF_RK // COL)
LOG2E = 1.4426950408889634
_COL_SCALE = {j: HEAD_DIM ** -0.5 * LOG2E for j in range(OFF_DQ // COL, OFF_DK // COL)}
_COL_SCALE[OFF_RK // COL] = HEAD_DIM ** -0.5


def _inproj_body(*refs, with_f):
    if with_f:
        (x_ref, f_ref, g2_ref, nw_ref, sh_ref, sc_ref, w_ref, cos_ref, sa_ref, sb_ref, p_ref, xo_ref) = refs
        x = x_ref[...] + g2_ref[0] * f_ref[...]
        xo_ref[...] = x
    else:
        (x_ref, nw_ref, sh_ref, sc_ref, w_ref, cos_ref, sa_ref, sb_ref, p_ref) = refs
        x = x_ref[...]
    h = (_rms(x) * nw_ref[...]) * (1.0 + sc_ref[0]) + sh_ref[0]
    hb = h.astype(MXU_DTYPE)
    cos, sa, sb = cos_ref[...], sa_ref[...], sb_ref[...]
    for j in range(IN_WIDTH // COL):
        acc = jnp.dot(hb, w_ref[:, j * COL:(j + 1) * COL], preferred_element_type=F32)
        if j in _ROPE_COLS:
            parts = []
            for t in range(COL // LANES):
                a = acc[:, t * LANES:(t + 1) * LANES]
                parts.append(a * cos + pltpu.roll(a, LANES - ROPE_FREQS, 1) * sa + pltpu.roll(a, ROPE_FREQS, 1) * sb)
            acc = jnp.concatenate(parts, axis=1)
        if j in _COL_SCALE:
            acc = acc * _COL_SCALE[j]
        p_ref[:, j * COL:(j + 1) * COL] = acc.astype(p_ref.dtype)


def _inproj(lay, layer, x, f, mods3, nw, w_in, tables):
    D, TM = lay.D, TOKEN_TILE
    row = lambda i: (i, 0)
    const = lambda i: (0, 0)
    tbl = pl.BlockSpec((TM, LANES), lambda i: (lay.rope_block(i), 0))
    in_specs = [pl.BlockSpec((TM, D), row)]
    args = [x]
    if f is not None:
        in_specs += [pl.BlockSpec((TM, D), row), lay.mod_spec(layer - 1, 5)]
        args += [f, mods3]
    in_specs += [pl.BlockSpec((1, D), const), lay.mod_spec(layer, 0), lay.mod_spec(layer, 1),
                 pl.BlockSpec((D, IN_WIDTH), const), tbl, tbl, tbl]
    args += [nw.reshape(1, D), mods3, mods3, w_in] + list(tables)
    out_specs = [pl.BlockSpec((TM, IN_WIDTH), row)]
    out_shape = [jax.ShapeDtypeStruct((lay.T, IN_WIDTH), ACT_DTYPE)]
    if f is not None:
        out_specs.append(pl.BlockSpec((TM, D), row))
        out_shape.append(jax.ShapeDtypeStruct((lay.T, D), F32))
    outs = pl.pallas_call(
        functools.partial(_inproj_body, with_f=f is not None),
        grid=(lay.T // TM,), in_specs=in_specs, out_specs=out_specs, out_shape=out_shape,
        compiler_params=_cparams("arbitrary"), name="inproj",
    )(*args)
    return (outs[0], outs[1]) if f is not None else (outs[0], x)


def _map_masks(q):
    lane = lax.broadcasted_iota(jnp.int32, (1, DIFF_V_DIM), 1)
    zero = jnp.zeros_like(q)
    return [jnp.where(lane < HEAD_DIM, q, zero), jnp.where(lane >= HEAD_DIM, q, zero)]


def _softmax_diff(s1, s2, lam):
    es, ls = [], []
    for s in (s1, s2):
        e = jnp.exp2(s - jnp.max(s, axis=-1, keepdims=True))
        ls.append(jnp.sum(e, axis=-1, keepdims=True))
        es.append(e.astype(MXU_DTYPE))
    c = (lam * ls[0] / ls[1]).astype(MXU_DTYPE)
    return es[0] - es[1] * c, 1.0 / ls[0]


def _head_norm(o, w, post_scale):
    return (_rms(o) * w) * post_scale


def _attn_lat_body(lam_ref, q_ref, kc_ref, kl_ref, vc_ref, vl_ref, w_ref, o_ref, s_a, s_b, *, Lc, post_scale):
    t = pl.program_id(0)

    @pl.when(t == 0)
    def _():
        s_b[...] = jnp.zeros_like(s_b)

    def step(s_w, s_r):
        for m, qm in enumerate(_map_masks(q_ref[...])):
            s_w[m, :, :Lc] = _mm_nt(qm, kc_ref[...])
            s_w[m, :, Lc:] = _mm_nt(qm, kl_ref[...])
        p, r1 = _softmax_diff(s_r[0], s_r[1], lam_ref[0])
        o = (jnp.dot(p[:, :Lc], vc_ref[...], preferred_element_type=F32)
             + jnp.dot(p[:, Lc:], vl_ref[...], preferred_element_type=F32)) * r1
        o_ref[...] = _head_norm(o, w_ref[...], post_scale).astype(o_ref.dtype)

    @pl.when(t % 2 == 0)
    def _():
        step(s_a, s_b)

    @pl.when(t % 2 == 1)
    def _():
        step(s_b, s_a)


def _attn_ctx_body(lam_ref, q_ref, k_ref, v_ref, w_ref, o_ref, *, post_scale):
    q1, q2 = _map_masks(q_ref[...])
    p, r1 = _softmax_diff(_mm_nt(q1, k_ref[...]), _mm_nt(q2, k_ref[...]), lam_ref[0])
    o = jnp.dot(p, v_ref[...], preferred_element_type=F32) * r1
    o_ref[...] = _head_norm(o, w_ref[...], post_scale).astype(o_ref.dtype)


def _diff_attention(lay, p, lam, norm_w, post_scale, need_ctx):
    B, L, Lc, H = lay.B, lay.L, lay.Lc, DIFF_HEADS
    W, tq = DIFF_V_DIM, SEQ_TILE
    nq = L // tq
    n = B * H * nq
    k_blk, v_blk = OFF_DK // W, OFF_DV // W
    ctx0 = lay.NL // Lc
    cur = lambda t: jnp.minimum(t, n - 1)
    prev = lambda t: jnp.maximum(t - 1, 0)
    bat = lambda u: u // (H * nq)
    head = lambda u: (u // nq) % H
    tile = lambda u: (bat(u) * nq + u % nq, head(u))
    in_specs = [pl.BlockSpec(memory_space=pltpu.SMEM),
                pl.BlockSpec((tq, W), lambda t: tile(cur(t))),
                pl.BlockSpec((Lc, W), lambda t: (ctx0 + bat(cur(t)), k_blk + head(cur(t)))),
                pl.BlockSpec((L, W), lambda t: (bat(cur(t)), k_blk + head(cur(t)))),
                pl.BlockSpec((Lc, W), lambda t: (ctx0 + bat(prev(t)), v_blk + head(prev(t)))),
                pl.BlockSpec((L, W), lambda t: (bat(prev(t)), v_blk + head(prev(t)))),
                pl.BlockSpec((1, W), lambda t: (0, 0))]
    a_lat = pl.pallas_call(
        functools.partial(_attn_lat_body, Lc=Lc, post_scale=post_scale),
        grid=(n + 1,), in_specs=in_specs,
        out_specs=pl.BlockSpec((tq, W), lambda t: tile(prev(t))),
        out_shape=jax.ShapeDtypeStruct((lay.NL, DIFF_WIDTH), ACT_DTYPE),
        scratch_shapes=[pltpu.VMEM((2, tq, Lc + L), F32), pltpu.VMEM((2, tq, Lc + L), F32)],
        compiler_params=_cparams("arbitrary"),
        name="diff_attn",
    )(lam, p, p, p, p, p, norm_w.reshape(1, W))
    if not need_ctx:
        return a_lat, None
    a_ctx = pl.pallas_call(
        functools.partial(_attn_ctx_body, post_scale=post_scale),
        grid=(B, H),
        in_specs=[pl.BlockSpec(memory_space=pltpu.SMEM),
                  pl.BlockSpec((Lc, W), lambda b, h: (ctx0 + b, h)),
                  pl.BlockSpec((Lc, W), lambda b, h: (ctx0 + b, k_blk + h)),
                  pl.BlockSpec((Lc, W), lambda b, h: (ctx0 + b, v_blk + h)),
                  pl.BlockSpec((1, W), lambda b, h: (0, 0))],
        out_specs=pl.BlockSpec((Lc, W), lambda b, h: (b, h)),
        out_shape=jax.ShapeDtypeStruct((lay.NC, DIFF_WIDTH), ACT_DTYPE),
        compiler_params=_cparams("arbitrary", "arbitrary"),
        name="diff_attn_ctx",
    )(lam, p, p, p, norm_w.reshape(1, W))
    return a_lat, a_ctx


def _pool_body(prev_ref, cur_ref, next_ref, w_ref, scale_ref, o_ref, *, lay):
    i = pl.program_id(0)
    n_lat = lay.NL // SEQ_TILE
    per_seq = jnp.where(i < n_lat, lay.L // SEQ_TILE, lay.Lc // SEQ_TILE)
    idx = jnp.where(i < n_lat, i, i - n_lat) % per_seq
    seq_len = per_seq * SEQ_TILE
    has_prev = (idx > 0).astype(F32)
    has_next = (idx < per_seq - 1).astype(F32)
    cur = cur_ref[...].astype(F32)
    u = jnp.concatenate([prev_ref[...].astype(F32) * has_prev, cur, next_ref[...].astype(F32) * has_next], axis=0)
    sums = {1: u}
    w = 1
    while w < POOL_WINDOWS[-1]:
        s = sums[w]
        m = s.shape[0] - w
        sums[2 * w] = s[:m] + s[w:w + m]
        w *= 2
    pos = idx * SEQ_TILE + lax.broadcasted_iota(jnp.int32, (SEQ_TILE, 1), 0)
    group = lax.broadcasted_iota(jnp.int32, (1, POOL_WIDTH), 1) // POOL_GROUP
    mean = jnp.zeros((SEQ_TILE, POOL_WIDTH), F32)
    for g, win in enumerate(POOL_WINDOWS):
        start = HALO - win // 2
        cnt = jnp.minimum(pos + (win - win // 2), seq_len) - jnp.maximum(pos - win // 2, 0)
        mean = jnp.where(group == g, sums[win][start:start + SEQ_TILE] / cnt.astype(F32), mean)
    o_ref[...] = (_mm(mean - cur, w_ref[...]) * scale_ref[...]).astype(o_ref.dtype)


def _pool(lay, p, w_bd, scale):
    n = lay.T // SEQ_TILE
    col = OFF_PU // POOL_WIDTH
    per = SEQ_TILE // HALO
    last = lay.T // HALO - 1
    return pl.pallas_call(
        functools.partial(_pool_body, lay=lay),
        grid=(n,),
        in_specs=[pl.BlockSpec((HALO, POOL_WIDTH), lambda i: (jnp.maximum(i * per - 1, 0), col)),
                  pl.BlockSpec((SEQ_TILE, POOL_WIDTH), lambda i: (i, col)),
                  pl.BlockSpec((HALO, POOL_WIDTH), lambda i: (jnp.minimum((i + 1) * per, last), col)),
                  pl.BlockSpec((POOL_WIDTH, POOL_WIDTH), lambda i: (0, 0)),
                  pl.BlockSpec((1, POOL_WIDTH), lambda i: (0, 0))],
        out_specs=pl.BlockSpec((SEQ_TILE, POOL_WIDTH), lambda i: (i, 0)),
        out_shape=jax.ShapeDtypeStruct((lay.T, POOL_WIDTH), ACT_DTYPE),
        compiler_params=_cparams("arbitrary"), name="pool",
    )(p, p, p, w_bd, scale.reshape(1, POOL_WIDTH))


def _ret_body(lgs_ref, lgv_ref, q_ref, k_ref, v_ref, *refs, reverse):
    C = SEQ_TILE
    if reverse:
        g_ref, yf_ref, nw_ref, o_ref, s_ref, d_ref, xi_ref, zeta_ref = refs
    else:
        o_ref, s_ref, d_ref, xi_ref, zeta_ref = refs
    b, j = pl.program_id(0), pl.program_id(1)
    lane_head = lax.broadcasted_iota(jnp.int32, (1, RET_WIDTH), 1) // HEAD_DIM
    lgv = lgv_ref[...]

    @pl.when((b == 0) & (j == 0))
    def _():
        ri = lax.broadcasted_iota(jnp.int32, (C, C), 0)
        ci = lax.broadcasted_iota(jnp.int32, (C, C), 1)
        dist = (ci - ri if reverse else ri - ci).astype(F32)
        keep = dist > 0 if reverse else dist >= 0
        for h in range(RET_HEADS):
            d_ref[h * C:(h + 1) * C, :] = jnp.where(keep, jnp.exp(lgs_ref[h] * jnp.maximum(dist, 0.0)), 0.0)
        t = lax.broadcasted_iota(jnp.int32, (C, 1), 0).astype(F32)
        xi_ref[...] = jnp.exp(lgv * ((C - t) if reverse else (t + 1.0)))
        zeta_ref[...] = jnp.exp(lgv * (t if reverse else (C - 1.0 - t)))

    @pl.when(j == 0)
    def _():
        s_ref[...] = jnp.zeros_like(s_ref)

    q, k, v = q_ref[...], k_ref[...], v_ref[...]
    zero = jnp.zeros_like(q)
    qs = jnp.concatenate([jnp.where(lane_head == h, q, zero) for h in range(RET_HEADS)], axis=0)
    sd = _mm_nt(qs, k) * d_ref[...]
    yv = _mm(sd, v)
    y = functools.reduce(jnp.add, [jnp.where(lane_head == h, yv[h * C:(h + 1) * C], 0.0) for h in range(RET_HEADS)])
    state = s_ref[...]
    y = y + _mm(q.astype(F32) * xi_ref[...], state)
    kv = _mm_tn(k.astype(F32) * zeta_ref[...], v)
    row_head = lax.broadcasted_iota(jnp.int32, (RET_WIDTH, 1), 0) // HEAD_DIM
    s_ref[...] = state * jnp.exp(lgv * C) + jnp.where(row_head == lane_head, kv, 0.0)

    if reverse:
        y = y + yf_ref[...]
        same = (row_head == lane_head).astype(F32) * (1.0 / HEAD_DIM)
        ms = _mm_hi(y * y, same)
        yn = y * lax.rsqrt(ms + EPS) * nw_ref[...]
        g = g_ref[...].astype(F32)
        o_ref[...] = ((g * _sigmoid(g)) * yn).astype(o_ref.dtype)
    else:
        o_ref[...] = y


def _retention_pass(lay, p, lgs, lgv, reverse, extra=()):
    B, L = lay.B, lay.L
    C, W = SEQ_TILE, RET_WIDTH
    nch = L // C
    ctx0 = lay.NL // C

    def rows(b, j):
        lat = b * nch + (nch - j if reverse else j - 1)
        return jnp.where(j == 0, ctx0 + b, lat)

    col = lambda c: pl.BlockSpec((C, W), lambda b, j: (rows(b, j), c))
    in_specs = [pl.BlockSpec(memory_space=pltpu.SMEM), pl.BlockSpec((1, W), lambda b, j: (0, 0)),
                col(OFF_RQ // W), col(OFF_RK // W), col(OFF_RV // W)]
    args = [lgs, lgv, p, p, p]
    if reverse:
        yf, nw = extra
        in_specs += [col(OFF_RG // W), col(0), pl.BlockSpec((1, W), lambda b, j: (0, 0))]
        args += [p, yf, nw.reshape(1, W)]
    return pl.pallas_call(
        functools.partial(_ret_body, reverse=reverse),
        grid=(B, nch + 1), in_specs=in_specs, out_specs=col(0),
        out_shape=jax.ShapeDtypeStruct((lay.T, W), ACT_DTYPE if reverse else F32),
        scratch_shapes=[pltpu.VMEM((W, W), F32), pltpu.VMEM((RET_HEADS * C, C), F32),
                        pltpu.VMEM((C, W), F32), pltpu.VMEM((C, W), F32)],
        compiler_params=_cparams("arbitrary", "arbitrary"),
        name="retention_bwd" if reverse else "retention_fwd",
    )(*args)


def _top2(vals):
    n = len(vals)
    v1 = functools.reduce(jnp.maximum, vals)
    i1 = jnp.full_like(v1, n - 1)
    for e in range(n - 2, -1, -1):
        i1 = jnp.where(vals[e] == v1, float(e), i1)
    rest = [jnp.where(i1 == float(e), -jnp.inf, vals[e]) for e in range(n)]
    v2 = functools.reduce(jnp.maximum, rest)
    i2 = jnp.full_like(v1, n - 1)
    for e in range(n - 2, -1, -1):
        i2 = jnp.where(rest[e] == v2, float(e), i2)
    return v1, i1, v2, i2


def _route(logits_t, bias):
    s = _sigmoid(logits_t)
    sel = s + bias
    groups = []
    for g in range(N_GROUPS):
        rows = [sel[g * EXPERTS_PER_GROUP + e:g * EXPERTS_PER_GROUP + e + 1] for e in range(EXPERTS_PER_GROUP)]
        groups.append(_top2(rows))
    score = [v1 + v2 for v1, _, v2, _ in groups]
    best = functools.reduce(jnp.maximum, score)
    gi = jnp.full_like(best, N_GROUPS - 1)
    for g in range(N_GROUPS - 2, -1, -1):
        gi = jnp.where(score[g] == best, float(g), gi)
    pick = lambda k: functools.reduce(
        lambda acc, g: jnp.where(gi == float(g), groups[g][k], acc), range(N_GROUPS - 1), groups[N_GROUPS - 1][k])
    e0 = gi * EXPERTS_PER_GROUP + pick(1)
    e1 = gi * EXPERTS_PER_GROUP + pick(3)
    s0 = jnp.zeros_like(best)
    s1 = jnp.zeros_like(best)
    for e in range(N_EXPERTS):
        s0 = jnp.where(e0 == float(e), s[e:e + 1], s0)
        s1 = jnp.where(e1 == float(e), s[e:e + 1], s1)
    tot = s0 + s1
    return e0, e1, s0 / tot, s1 / tot


def _outproj_body(*refs, lay, with_ctx):
    if with_ctx:
        (x_ref, al_ref, ac_ref, b_ref, r_ref, w_ref, g1_ref, nw_ref, sh_ref, sc_ref, rw_ref, rb_ref, tri_ref,
         xo_ref, hx_ref, cnt_ref, run_ref) = refs
        a = jnp.where(pl.program_id(0) < lay.n_lat_tiles, al_ref[...], ac_ref[...])
    else:
        (x_ref, al_ref, b_ref, r_ref, w_ref, g1_ref, nw_ref, sh_ref, sc_ref, rw_ref, rb_ref, tri_ref,
         xo_ref, hx_ref, cnt_ref, run_ref) = refs
        a = al_ref[...]
    D = lay.D
    y = (_mm(a, w_ref[:DIFF_WIDTH, :])
         + _mm(b_ref[...], w_ref[DIFF_WIDTH:DIFF_WIDTH + POOL_WIDTH, :])
         + _mm(r_ref[...], w_ref[DIFF_WIDTH + POOL_WIDTH:, :]))
    x = x_ref[...] + g1_ref[0] * y
    xo_ref[...] = x
    h = (_rms(x) * nw_ref[...]) * (1.0 + sc_ref[0]) + sh_ref[0]
    hx_ref[:, :D] = h
    logits_t = jnp.transpose(_mm_hi(h, rw_ref[...]))[:N_EXPERTS]
    e0, e1, g0, g1 = _route(logits_t, rb_ref[...])
    lo, hi = jnp.minimum(e0, e1), jnp.maximum(e0, e1)
    grp = jnp.floor(lo * (1.0 / EXPERTS_PER_GROUP))
    pa, pb = lo - grp * EXPERTS_PER_GROUP, hi - grp * EXPERTS_PER_GROUP
    cls = grp * PAIRS_PER_GROUP + pa * ((2 * EXPERTS_PER_GROUP - 1) - pa) * 0.5 + (pb - pa - 1.0)
    rec = jnp.concatenate([e0, e1, g0, g1, cls, jnp.zeros((LANES - 5, TOKEN_TILE), F32)], axis=0)
    rec_t = jnp.transpose(rec)
    lane = lax.broadcasted_iota(jnp.int32, (1, LANES), 1)
    onehot = jnp.where(rec_t[:, REC_CLS:REC_CLS + 1] == lane.astype(F32), 1.0, 0.0)
    incl = jnp.dot(tri_ref[...], onehot.astype(MXU_DTYPE), preferred_element_type=F32)

    @pl.when(pl.program_id(0) == 0)
    def _():
        run_ref[...] = jnp.zeros_like(run_ref)

    run = run_ref[...]
    rank = jnp.sum(onehot * (incl + run), axis=-1, keepdims=True) - 1.0
    run = run + incl[TOKEN_TILE - 1:TOKEN_TILE, :]
    run_ref[...] = run
    cnt_ref[...] = jnp.broadcast_to(run, cnt_ref.shape)
    hx_ref[:, D:] = jnp.where(lane == REC_RANK, rank, rec_t)


def _outproj(lay, layer, n_tok, x, a_lat, a_ctx, b, r, w_out, mods3, nw, rw_pad, rb_col, tri):
    D, TM = lay.D, TOKEN_TILE
    row = lambda i: (i, 0)
    const = lambda i: (0, 0)
    tile = lambda w: pl.BlockSpec((TM, w), row)
    n_lat = lay.n_lat_tiles
    in_specs = [tile(D), pl.BlockSpec((TM, DIFF_WIDTH), lambda i: (jnp.minimum(i, n_lat - 1), 0))]
    args = [x, a_lat]
    if a_ctx is not None:
        in_specs.append(pl.BlockSpec((TM, DIFF_WIDTH), lambda i: (jnp.maximum(i - n_lat, 0), 0)))
        args.append(a_ctx)
    in_specs += [tile(POOL_WIDTH), tile(RET_WIDTH),
                 pl.BlockSpec((DIFF_WIDTH + POOL_WIDTH + RET_WIDTH, D), const),
                 lay.mod_spec(layer, 2), pl.BlockSpec((1, D), const), lay.mod_spec(layer, 3), lay.mod_spec(layer, 4),
                 pl.BlockSpec((D, LANES), const), pl.BlockSpec((N_EXPERTS, 1), const), pl.BlockSpec((TM, TM), const)]
    args += [b, r, w_out, mods3, nw.reshape(1, D), mods3, mods3, rw_pad, rb_col, tri]
    return pl.pallas_call(
        functools.partial(_outproj_body, lay=lay, with_ctx=a_ctx is not None),
        grid=(n_tok // TM,), in_specs=in_specs,
        out_specs=[tile(D), tile(D + LANES), pl.BlockSpec((8, LANES), const)],
        out_shape=[jax.ShapeDtypeStruct((n_tok, D), F32), jax.ShapeDtypeStruct((n_tok, D + LANES), F32),
                   jax.ShapeDtypeStruct((8, LANES), F32)],
        scratch_shapes=[pltpu.VMEM((1, LANES), F32)],
        compiler_params=_cparams("arbitrary"), name="outproj_router",
    )(*args)


def _row_copies(idx_ref, src, dst, sem, base, scatter):
    def body(u, carry):
        k = idx_ref[0, 0, u]
        s_row, d_row = (base + u, k) if scatter else (k, base + u)
        pltpu.make_async_copy(src.at[pl.ds(s_row, 1)], dst.at[pl.ds(d_row, 1)], sem).start()
        return carry
    lax.fori_loop(0, PERM_CHUNK, body, 0, unroll=8)


def _row_waits(src, dst, sem):
    def body(u, carry):
        pltpu.make_async_copy(src.at[pl.ds(0, 1)], dst.at[pl.ds(0, 1)], sem).wait()
        return carry
    lax.fori_loop(0, PERM_CHUNK, body, 0, unroll=8)


def _dispatch_body(pend_ref, padded_ref, nvalid_ref, pos_ref, hx_hbm, xs_hbm, zbuf, zsem, sems, *, n_blk):
    TB = MOE_TILE
    i = pl.program_id(0)
    n = pl.num_programs(0)

    @pl.when(i == 0)
    def _():
        zbuf[...] = jnp.zeros_like(zbuf)
        fill = lambda row: pltpu.make_async_copy(zbuf, xs_hbm.at[pl.ds(row, TB)], zsem.at[0])
        for k in range(N_CLASSES):
            @pl.when(padded_ref[k] > 0)
            def _():
                fill(pl.multiple_of(pend_ref[k] - TB, TB)).start()
        for j in range(n_blk):
            @pl.when(j >= nvalid_ref[0])
            def _():
                fill(j * TB).start()
        for k in range(N_CLASSES):
            @pl.when(padded_ref[k] > 0)
            def _():
                fill(0).wait()
        for j in range(n_blk):
            @pl.when(j >= nvalid_ref[0])
            def _():
                fill(0).wait()

    _row_copies(pos_ref, hx_hbm, xs_hbm, sems.at[i % 2], i * PERM_CHUNK, True)

    @pl.when(i > 0)
    def _():
        _row_waits(hx_hbm, xs_hbm, sems.at[(i - 1) % 2])

    @pl.when(i == n - 1)
    def _():
        _row_waits(hx_hbm, xs_hbm, sems.at[i % 2])


def _combine_body(pos_ref, ys_hbm, f_hbm, sems):
    i = pl.program_id(0)
    n = pl.num_programs(0)
    _row_copies(pos_ref, ys_hbm, f_hbm, sems.at[i % 2], i * PERM_CHUNK, False)

    @pl.when(i > 0)
    def _():
        _row_waits(ys_hbm, f_hbm, sems.at[(i - 1) % 2])

    @pl.when(i == n - 1)
    def _():
        _row_waits(ys_hbm, f_hbm, sems.at[i % 2])


def _moe_body(elo_ref, ehi_ref, nvalid_ref, xs_ref, wg_lo, wu_lo, wd_lo, wg_hi, wu_hi, wd_hi, ys_ref, *, D):
    j = pl.program_id(0)

    @pl.when(j < nvalid_ref[0])
    def _():
        xb = xs_ref[:, :D].astype(MXU_DTYPE)
        info = xs_ref[:, D:]
        first_is_lo = info[:, REC_E0:REC_E0 + 1] <= info[:, REC_E1:REC_E1 + 1]
        g0, g1 = info[:, REC_G0:REC_G0 + 1], info[:, REC_G1:REC_G1 + 1]
        g_lo = jnp.where(first_is_lo, g0, g1)
        g_hi = jnp.where(first_is_lo, g1, g0)

        def ffn(wg, wu, wd):
            hg = jnp.dot(xb, wg[...], preferred_element_type=F32)
            hu = jnp.dot(xb, wu[...], preferred_element_type=F32)
            return _mm((hg * _sigmoid(hg)) * hu, wd[...])

        ys_ref[...] = g_lo * ffn(wg_lo, wu_lo, wd_lo) + g_hi * ffn(wg_hi, wu_hi, wd_hi)

    @pl.when(j >= nvalid_ref[0])
    def _():
        ys_ref[...] = jnp.zeros_like(ys_ref)


def _moe(n_tok, D, hx, counts, wg, wu, wd):
    TB = MOE_TILE
    DE = wg.shape[-1]
    assert n_tok % PERM_CHUNK == 0
    cls = hx[:, D + REC_CLS].astype(jnp.int32)
    rank = hx[:, D + REC_RANK].astype(jnp.int32)
    cnt = counts[0, :N_CLASSES].astype(jnp.int32)
    padded = (cnt + TB - 1) // TB * TB
    pend = jnp.cumsum(padded)
    pstart = pend - padded
    onehot = cls[:, None] == jnp.arange(N_CLASSES, dtype=jnp.int32)[None, :]
    pos = jnp.sum(jnp.where(onehot, pstart[None, :], 0), axis=1) + rank
    n_blk = -(-n_tok // TB) + N_CLASSES
    P = n_blk * TB
    nvalid = (pend[-1] // TB).astype(jnp.int32).reshape(1)
    blk = jnp.minimum(jnp.arange(n_blk, dtype=jnp.int32), nvalid - 1)
    blk_cls = jnp.minimum(jnp.searchsorted(pend, blk * TB, side='right'), N_CLASSES - 1).astype(jnp.int32)
    pairs = [(u, v) for u in range(EXPERTS_PER_GROUP) for v in range(u + 1, EXPERTS_PER_GROUP)]
    pair_lo = jnp.array([u for u, _ in pairs], jnp.int32)
    pair_hi = jnp.array([v for _, v in pairs], jnp.int32)
    base = (blk_cls // PAIRS_PER_GROUP) * EXPERTS_PER_GROUP
    blk_lo = base + pair_lo[blk_cls % PAIRS_PER_GROUP]
    blk_hi = base + pair_hi[blk_cls % PAIRS_PER_GROUP]
    pos3 = pos.reshape(n_tok // PERM_CHUNK, 1, PERM_CHUNK)
    pos_spec = pl.BlockSpec((1, 1, PERM_CHUNK), lambda i, *_: (i, 0, 0), memory_space=pltpu.SMEM)
    any_spec = pl.BlockSpec(memory_space=pl.ANY)

    xs = pl.pallas_call(
        functools.partial(_dispatch_body, n_blk=n_blk),
        grid_spec=pltpu.PrefetchScalarGridSpec(
            num_scalar_prefetch=3, grid=(n_tok // PERM_CHUNK,), in_specs=[pos_spec, any_spec], out_specs=any_spec,
            scratch_shapes=[pltpu.VMEM((TB, D + LANES), F32), pltpu.SemaphoreType.DMA((1,)),
                            pltpu.SemaphoreType.DMA((2,))]),
        out_shape=jax.ShapeDtypeStruct((P, D + LANES), F32),
        compiler_params=_cparams("arbitrary"), name="moe_dispatch",
    )(pend.astype(jnp.int32), padded.astype(jnp.int32), nvalid, pos3, hx)

    w_spec = lambda shape, which: pl.BlockSpec(
        (None,) + shape, (lambda j, elo, ehi, nv: (elo[j], 0, 0)) if which == 0 else (lambda j, elo, ehi, nv: (ehi[j], 0, 0)))
    ys = pl.pallas_call(
        functools.partial(_moe_body, D=D),
        grid_spec=pltpu.PrefetchScalarGridSpec(
            num_scalar_prefetch=3, grid=(n_blk,),
            in_specs=[pl.BlockSpec((TB, D + LANES), lambda j, elo, ehi, nv: (jnp.minimum(j, nv[0] - 1), 0)),
                      w_spec((D, DE), 0), w_spec((D, DE), 0), w_spec((DE, D), 0),
                      w_spec((D, DE), 1), w_spec((D, DE), 1), w_spec((DE, D), 1)],
            out_specs=pl.BlockSpec((TB, D), lambda j, elo, ehi, nv: (j, 0))),
        out_shape=jax.ShapeDtypeStruct((P, D), F32),
        compiler_params=_cparams("arbitrary"), name="moe_experts",
    )(blk_lo, blk_hi, nvalid, xs, wg, wu, wd, wg, wu, wd)

    return pl.pallas_call(
        _combine_body,
        grid_spec=pltpu.PrefetchScalarGridSpec(
            num_scalar_prefetch=0, grid=(n_tok // PERM_CHUNK,), in_specs=[pos_spec, any_spec], out_specs=any_spec,
            scratch_shapes=[pltpu.SemaphoreType.DMA((2,))]),
        out_shape=jax.ShapeDtypeStruct((n_tok, D), F32),
        compiler_params=_cparams("arbitrary"), name="moe_combine",
    )(pos3, ys)


def _final_body(x_ref, f_ref, g2_ref, nw_ref, o_ref):
    o_ref[...] = _rms(x_ref[...] + g2_ref[0] * f_ref[...]) * nw_ref[...]


def _final(lay, layer, x, f, mods3, nw):
    D, TM = lay.D, TOKEN_TILE
    row = lambda i: (i, 0)
    return pl.pallas_call(
        _final_body, grid=(lay.NL // TM,),
        in_specs=[pl.BlockSpec((TM, D), row), pl.BlockSpec((TM, D), row), lay.mod_spec(layer, 5),
                  pl.BlockSpec((1, D), lambda i: (0, 0))],
        out_specs=pl.BlockSpec((TM, D), row),
        out_shape=jax.ShapeDtypeStruct((lay.NL, D), F32),
        compiler_params=_cparams("arbitrary"), name="final_norm",
    )(x, f, mods3, nw.reshape(1, D))


def kernel(x, c, ctx, c_ctx, w_in, w_out, ada_w, ada_b, norm1_w, norm2_w, diff_lambda, diff_norm_w, pool_w,
           pool_scale, ret_a_f, ret_a_b, ret_norm_w, router_w, router_b, moe_w_gate, moe_w_up, moe_w_down,
           final_norm_w):
    B, L, D = x.shape
    Lc = ctx.shape[1]
    depth = w_in.shape[0]
    lay = _Layout(B, L, Lc, D)
    assert B + 1 <= MOD_ROWS

    xa = jnp.concatenate([x.reshape(B * L, D), ctx.reshape(B * Lc, D)], axis=0)
    c_all = jnp.zeros((MOD_ROWS, D), F32).at[:B].set(c).at[B].set(c_ctx)
    mods3 = _ada_table(c_all, ada_w, ada_b).reshape(depth * MOD_ROWS * N_MOD, 1, D)
    tables = _rope_tables(lay)
    rw_pad = jnp.zeros((D, LANES), F32).at[:, :N_EXPERTS].set(router_w.astype(F32))
    rb_col = router_b.astype(F32).reshape(N_EXPERTS, 1)
    tri = jnp.tril(jnp.ones((TOKEN_TILE, TOKEN_TILE), F32)).astype(MXU_DTYPE)
    w_in_b, w_out_b = w_in.astype(MXU_DTYPE), w_out.astype(MXU_DTYPE)
    wg_b, wu_b, wd_b = moe_w_gate.astype(MXU_DTYPE), moe_w_up.astype(MXU_DTYPE), moe_w_down.astype(MXU_DTYPE)

    f = None
    for l in range(depth):
        last = l == depth - 1
        lam_init = 0.8 - 0.6 * math.exp(-0.3 * l)
        dl = diff_lambda[l].astype(F32)
        lam = (jnp.exp(jnp.sum(dl[0] * dl[1])) - jnp.exp(jnp.sum(dl[2] * dl[3])) + lam_init).reshape(1)
        p, xa = _inproj(lay, l, xa, f, mods3, norm1_w[l], w_in_b[l], tables)
        a_lat, a_ctx = _diff_attention(lay, p, lam, diff_norm_w[l], 1.0 - lam_init, not last)
        eye = jnp.eye(len(POOL_WINDOWS), dtype=F32)
        w_bd = (eye[:, None, :, None] * pool_w[l][:, :, None, :]).reshape(POOL_WIDTH, POOL_WIDTH).astype(MXU_DTYPE)
        bp = _pool(lay, p, w_bd, pool_scale[l])
        lg_f = -jnp.exp(ret_a_f[l].astype(F32))
        lg_b = -jnp.exp(ret_a_b[l].astype(F32))
        lanes = lambda lg: jnp.repeat(lg, HEAD_DIM).reshape(1, RET_WIDTH)
        yf = _retention_pass(lay, p, lg_f, lanes(lg_f), False)
        r = _retention_pass(lay, p, lg_b, lanes(lg_b), True, (yf, ret_norm_w[l]))
        n_tok = lay.NL if last else lay.T
        xa, hx, counts = _outproj(lay, l, n_tok, xa, a_lat, a_ctx, bp, r, w_out_b[l], mods3, norm2_w[l],
                                  rw_pad, rb_col, tri)
        f = _moe(n_tok, D, hx, counts, wg_b[l], wu_b[l], wd_b[l])
    out = _final(lay, depth - 1, xa, f, mods3, final_norm_w)
    return out.reshape(B, L, D)
```

```python
import functools
import math

import jax
import jax.numpy as jnp
from jax import lax
from jax.experimental import pallas as pl
from jax.experimental.pallas import tpu as pltpu

F32 = jnp.float32
MXU_DTYPE = jnp.bfloat16
ACT_DTYPE = jnp.bfloat16

GRID_W = 64
HEAD_DIM = 64
DIFF_HEADS = 4
DIFF_V_DIM = 2 * HEAD_DIM
DIFF_WIDTH = DIFF_HEADS * DIFF_V_DIM
POOL_WINDOWS = (2, 4, 8, 16)
POOL_GROUP = 64
POOL_WIDTH = POOL_GROUP * len(POOL_WINDOWS)
RET_HEADS = 4
RET_WIDTH = RET_HEADS * HEAD_DIM
OFF_DQ = 0
OFF_DK = OFF_DQ + DIFF_WIDTH
OFF_DV = OFF_DK + DIFF_WIDTH
OFF_PU = OFF_DV + DIFF_WIDTH
OFF_RQ = OFF_PU + POOL_WIDTH
OFF_RK = OFF_RQ + RET_WIDTH
OFF_RV = OFF_RK + RET_WIDTH
OFF_RG = OFF_RV + RET_WIDTH
IN_WIDTH = OFF_RG + RET_WIDTH
ROPE_BASE = 10000.0
ROPE_FREQS = HEAD_DIM // 4
N_EXPERTS = 16
N_GROUPS = 4
EXPERTS_PER_GROUP = N_EXPERTS // N_GROUPS
PAIRS_PER_GROUP = EXPERTS_PER_GROUP * (EXPERTS_PER_GROUP - 1) // 2
N_CLASSES = N_GROUPS * PAIRS_PER_GROUP
N_MOD = 6
EPS = 1e-6

LANES = 128
MOD_ROWS = 16
TOKEN_TILE = 512
SEQ_TILE = 256
ATTN_SUBTILES = 4
ATTN_SUB_ROWS = 128
HALO = 16
MOE_TILE = 256
COL = 256
PERM_CHUNK = 512
REC_E0, REC_E1, REC_G0, REC_G1, REC_CLS, REC_RANK = range(6)
VMEM_LIMIT = 56 * 1024 * 1024


def _mm(a, b):
    return jnp.dot(a.astype(MXU_DTYPE), b.astype(MXU_DTYPE), preferred_element_type=F32)


def _mm_nt(a, b):
    return lax.dot_general(a.astype(MXU_DTYPE), b.astype(MXU_DTYPE), (((1,), (1,)), ((), ())),
                           preferred_element_type=F32)


def _mm_tn(a, b):
    return lax.dot_general(a.astype(MXU_DTYPE), b.astype(MXU_DTYPE), (((0,), (0,)), ((), ())),
                           preferred_element_type=F32)


def _split(x):
    hi = x.astype(MXU_DTYPE)
    lo = (x - hi.astype(F32)).astype(MXU_DTYPE)
    return hi, lo


def _mm_hi(a, b):
    ah, al = _split(a)
    bh, bl = _split(b)
    d = lambda u, v: jnp.dot(u, v, preferred_element_type=F32)
    return d(ah, bh) + (d(ah, bl) + d(al, bh))


def _sigmoid(x):
    return 1.0 / (1.0 + jnp.exp(-x))


def _rms(x):
    return x * lax.rsqrt(jnp.mean(x * x, axis=-1, keepdims=True) + EPS)


def _cparams(*sem):
    return pltpu.CompilerParams(dimension_semantics=sem, vmem_limit_bytes=VMEM_LIMIT)


def _ada_body(c_ref, w_ref, b_ref, o_ref):
    c = c_ref[...]
    o_ref[...] = _mm_hi(c * _sigmoid(c), w_ref[...]) + b_ref[...]


def _ada_table(c_all, ada_w, ada_b):
    depth, d, n = ada_w.shape
    nb = n // N_MOD
    return pl.pallas_call(
        _ada_body,
        grid=(depth, n // nb),
        in_specs=[pl.BlockSpec((MOD_ROWS, d), lambda l, j: (0, 0)),
                  pl.BlockSpec((None, d, nb), lambda l, j: (l, 0, j)),
                  pl.BlockSpec((None, 1, nb), lambda l, j: (l, 0, j))],
        out_specs=pl.BlockSpec((None, MOD_ROWS, nb), lambda l, j: (l, 0, j)),
        out_shape=jax.ShapeDtypeStruct((depth, MOD_ROWS, n), F32),
        compiler_params=_cparams("arbitrary", "arbitrary"),
        name="ada_table",
    )(c_all, ada_w, ada_b.reshape(depth, 1, n))


class _Layout:
    def __init__(self, B, L, Lc, D):
        self.B, self.L, self.Lc, self.D = B, L, Lc, D
        self.NL, self.NC = B * L, B * Lc
        self.T = self.NL + self.NC
        assert L % TOKEN_TILE == 0 and self.NC % TOKEN_TILE == 0
        assert L % SEQ_TILE == 0 and Lc == SEQ_TILE and L % GRID_W == 0
        self.n_lat_tiles = self.NL // TOKEN_TILE
        self.tiles_per_seq = L // TOKEN_TILE

    def mod_row(self, i):
        return jnp.where(i < self.n_lat_tiles, i // self.tiles_per_seq, self.B)

    def rope_block(self, i):
        return jnp.where(i < self.n_lat_tiles, i % self.tiles_per_seq, self.tiles_per_seq)

    def mod_spec(self, layer, k):
        base = layer * MOD_ROWS * N_MOD + k
        return pl.BlockSpec((1, 1, self.D), lambda i: (base + self.mod_row(i) * N_MOD, 0, 0))


def _rope_tables(lay):
    L = lay.L
    rows = L // GRID_W
    row = jnp.repeat(jnp.arange(rows), GRID_W).astype(F32)
    col = jnp.tile(jnp.arange(GRID_W), rows).astype(F32)
    inv = ROPE_BASE ** (-jnp.arange(ROPE_FREQS, dtype=F32) / ROPE_FREQS)
    ang_r = row[:, None] * inv
    ang_c = col[:, None] * inv
    ang = jnp.concatenate([ang_r, ang_r, ang_c, ang_c], axis=-1)
    cos, sin = jnp.cos(ang), jnp.sin(ang)
    first_half = (jnp.arange(HEAD_DIM) % (2 * ROPE_FREQS)) < ROPE_FREQS
    sin_a = jnp.where(first_half, -sin, 0.0)
    sin_b = jnp.where(first_half, 0.0, sin)
    rep = LANES // HEAD_DIM
    ident = [jnp.ones((TOKEN_TILE, LANES), F32), jnp.zeros((TOKEN_TILE, LANES), F32),
             jnp.zeros((TOKEN_TILE, LANES), F32)]
    return [jnp.concatenate([jnp.tile(t, (1, rep)), e], axis=0) for t, e in zip((cos, sin_a, sin_b), ident)]


_ROPE_COLS = tuple(range(OFF_DQ // COL, OFF_DV // COL)) + (OFF_RQ // COL, OFF_RK // COL)
LOG2E = 1.4426950408889634
_COL_SCALE = {j: HEAD_DIM ** -0.5 * LOG2E for j in range(OFF_DQ // COL, OFF_DK // COL)}
_COL_SCALE[OFF_RK // COL] = HEAD_DIM ** -0.5


def _inproj_body(*refs, with_f):
    if with_f:
        (x_ref, f_ref, g2_ref, nw_ref, sh_ref, sc_ref, w_ref, cos_ref, sa_ref, sb_ref, p_ref, xo_ref) = refs
        x = x_ref[...] + g2_ref[0] * f_ref[...]
        xo_ref[...] = x
    else:
        (x_ref, nw_ref, sh_ref, sc_ref, w_ref, cos_ref, sa_ref, sb_ref, p_ref) = refs
        x = x_ref[...]
    h = (_rms(x) * nw_ref[...]) * (1.0 + sc_ref[0]) + sh_ref[0]
    hb = h.astype(MXU_DTYPE)
    cos, sa, sb = cos_ref[...], sa_ref[...], sb_ref[...]
    for j in range(IN_WIDTH // COL):
        acc = jnp.dot(hb, w_ref[:, j * COL:(j + 1) * COL], preferred_element_type=F32)
        if j in _ROPE_COLS:
            parts = []
            for t in range(COL // LANES):
                a = acc[:, t * LANES:(t + 1) * LANES]
                parts.append(a * cos + pltpu.roll(a, LANES - ROPE_FREQS, 1) * sa + pltpu.roll(a, ROPE_FREQS, 1) * sb)
            acc = jnp.concatenate(parts, axis=1)
        if j in _COL_SCALE:
            acc = acc * _COL_SCALE[j]
        p_ref[:, j * COL:(j + 1) * COL] = acc.astype(p_ref.dtype)


def _inproj(lay, layer, x, f, mods3, nw, w_in, tables):
    D, TM = lay.D, TOKEN_TILE
    row = lambda i: (i, 0)
    const = lambda i: (0, 0)
    tbl = pl.BlockSpec((TM, LANES), lambda i: (lay.rope_block(i), 0))
    in_specs = [pl.BlockSpec((TM, D), row)]
    args = [x]
    if f is not None:
        in_specs += [pl.BlockSpec((TM, D), row), lay.mod_spec(layer - 1, 5)]
        args += [f, mods3]
    in_specs += [pl.BlockSpec((1, D), const), lay.mod_spec(layer, 0), lay.mod_spec(layer, 1),
                 pl.BlockSpec((D, IN_WIDTH), const), tbl, tbl, tbl]
    args += [nw.reshape(1, D), mods3, mods3, w_in] + list(tables)
    out_specs = [pl.BlockSpec((TM, IN_WIDTH), row)]
    out_shape = [jax.ShapeDtypeStruct((lay.T, IN_WIDTH), ACT_DTYPE)]
    if f is not None:
        out_specs.append(pl.BlockSpec((TM, D), row))
        out_shape.append(jax.ShapeDtypeStruct((lay.T, D), F32))
    outs = pl.pallas_call(
        functools.partial(_inproj_body, with_f=f is not None),
        grid=(lay.T // TM,), in_specs=in_specs, out_specs=out_specs, out_shape=out_shape,
        compiler_params=_cparams("arbitrary"), name="inproj",
    )(*args)
    return (outs[0], outs[1]) if f is not None else (outs[0], x)


def _map_masks(q):
    lane = lax.broadcasted_iota(jnp.int32, (1, DIFF_V_DIM), 1)
    zero = jnp.zeros_like(q)
    return [jnp.where(lane < HEAD_DIM, q, zero), jnp.where(lane >= HEAD_DIM, q, zero)]


def _softmax_diff(s1, s2, lam):
    es, ls = [], []
    for segs in (s1, s2):
        mx = functools.reduce(jnp.maximum, [jnp.max(t, axis=-1, keepdims=True) for t in segs])
        e = [jnp.exp2(t - mx) for t in segs]
        ls.append(functools.reduce(jnp.add, [jnp.sum(t, axis=-1, keepdims=True) for t in e]))
        es.append([t.astype(MXU_DTYPE) for t in e])
    c = (lam * ls[0] / ls[1]).astype(MXU_DTYPE)
    return [e1 - e2 * c for e1, e2 in zip(es[0], es[1])], 1.0 / ls[0]


def _head_norm(o, w, post_scale):
    return (_rms(o) * w) * post_scale


def _attn_body(lam_ref, q_ref, *refs, n_seg, n_sub, post_scale):
    k_refs, v_refs = refs[:n_seg], refs[n_seg:2 * n_seg]
    w_ref, o_ref = refs[2 * n_seg], refs[2 * n_seg + 1]
    rows = q_ref.shape[0] // n_sub
    scores = []
    for u in range(n_sub):
        q1, q2 = _map_masks(q_ref[u * rows:(u + 1) * rows, :])
        scores.append(([_mm_nt(q1, k[...]) for k in k_refs], [_mm_nt(q2, k[...]) for k in k_refs]))
    for u, (s1, s2) in enumerate(scores):
        p, r1 = _softmax_diff(s1, s2, lam_ref[0])
        o = functools.reduce(jnp.add, [jnp.dot(t, v[...], preferred_element_type=F32) for t, v in zip(p, v_refs)]) * r1
        o_ref[u * rows:(u + 1) * rows, :] = _head_norm(o, w_ref[...], post_scale).astype(o_ref.dtype)


def _diff_attention(lay, p, lam, norm_w, post_scale, need_ctx):
    B, L, Lc, H = lay.B, lay.L, lay.Lc, DIFF_HEADS
    W = DIFF_V_DIM
    tq = ATTN_SUBTILES * ATTN_SUB_ROWS
    assert L % tq == 0
    nq = L // tq
    k_blk, v_blk = OFF_DK // W, OFF_DV // W
    ctx0 = lay.NL // Lc
    lam_spec = pl.BlockSpec(memory_space=pltpu.SMEM)
    w_spec = pl.BlockSpec((1, W), lambda *_: (0, 0))
    a_lat = pl.pallas_call(
        functools.partial(_attn_body, n_seg=2, n_sub=ATTN_SUBTILES, post_scale=post_scale),
        grid=(B, H, nq),
        in_specs=[lam_spec,
                  pl.BlockSpec((tq, W), lambda b, h, i: (b * nq + i, h)),
                  pl.BlockSpec((Lc, W), lambda b, h, i: (ctx0 + b, k_blk + h)),
                  pl.BlockSpec((L, W), lambda b, h, i: (b, k_blk + h)),
                  pl.BlockSpec((Lc, W), lambda b, h, i: (ctx0 + b, v_blk + h)),
                  pl.BlockSpec((L, W), lambda b, h, i: (b, v_blk + h)),
                  w_spec],
        out_specs=pl.BlockSpec((tq, W), lambda b, h, i: (b * nq + i, h)),
        out_shape=jax.ShapeDtypeStruct((lay.NL, DIFF_WIDTH), ACT_DTYPE),
        compiler_params=_cparams("arbitrary", "arbitrary", "arbitrary"),
        name="diff_attn",
    )(lam, p, p, p, p, p, norm_w.reshape(1, W))
    if not need_ctx:
        return a_lat, None
    a_ctx = pl.pallas_call(
        functools.partial(_attn_body, n_seg=1, n_sub=1, post_scale=post_scale),
        grid=(B, H),
        in_specs=[lam_spec,
                  pl.BlockSpec((Lc, W), lambda b, h: (ctx0 + b, h)),
                  pl.BlockSpec((Lc, W), lambda b, h: (ctx0 + b, k_blk + h)),
                  pl.BlockSpec((Lc, W), lambda b, h: (ctx0 + b, v_blk + h)),
                  w_spec],
        out_specs=pl.BlockSpec((Lc, W), lambda b, h: (b, h)),
        out_shape=jax.ShapeDtypeStruct((lay.NC, DIFF_WIDTH), ACT_DTYPE),
        compiler_params=_cparams("arbitrary", "arbitrary"),
        name="diff_attn_ctx",
    )(lam, p, p, p, norm_w.reshape(1, W))
    return a_lat, a_ctx


def _pool_body(prev_ref, cur_ref, next_ref, w_ref, scale_ref, o_ref, *, lay):
    i = pl.program_id(0)
    n_lat = lay.NL // SEQ_TILE
    per_seq = jnp.where(i < n_lat, lay.L // SEQ_TILE, lay.Lc // SEQ_TILE)
    idx = jnp.where(i < n_lat, i, i - n_lat) % per_seq
    seq_len = per_seq * SEQ_TILE
    has_prev = (idx > 0).astype(F32)
    has_next = (idx < per_seq - 1).astype(F32)
    cur = cur_ref[...].astype(F32)
    u = jnp.concatenate([prev_ref[...].astype(F32) * has_prev, cur, next_ref[...].astype(F32) * has_next], axis=0)
    sums = {1: u}
    w = 1
    while w < POOL_WINDOWS[-1]:
        s = sums[w]
        m = s.shape[0] - w
        sums[2 * w] = s[:m] + s[w:w + m]
        w *= 2
    pos = idx * SEQ_TILE + lax.broadcasted_iota(jnp.int32, (SEQ_TILE, 1), 0)
    group = lax.broadcasted_iota(jnp.int32, (1, POOL_WIDTH), 1) // POOL_GROUP
    mean = jnp.zeros((SEQ_TILE, POOL_WIDTH), F32)
    for g, win in enumerate(POOL_WINDOWS):
        start = HALO - win // 2
        cnt = jnp.minimum(pos + (win - win // 2), seq_len) - jnp.maximum(pos - win // 2, 0)
        mean = jnp.where(group == g, sums[win][start:start + SEQ_TILE] / cnt.astype(F32), mean)
    o_ref[...] = (_mm(mean - cur, w_ref[...]) * scale_ref[...]).astype(o_ref.dtype)


def _pool(lay, p, w_bd, scale):
    n = lay.T // SEQ_TILE
    col = OFF_PU // POOL_WIDTH
    per = SEQ_TILE // HALO
    last = lay.T // HALO - 1
    return pl.pallas_call(
        functools.partial(_pool_body, lay=lay),
        grid=(n,),
        in_specs=[pl.BlockSpec((HALO, POOL_WIDTH), lambda i: (jnp.maximum(i * per - 1, 0), col)),
                  pl.BlockSpec((SEQ_TILE, POOL_WIDTH), lambda i: (i, col)),
                  pl.BlockSpec((HALO, POOL_WIDTH), lambda i: (jnp.minimum((i + 1) * per, last), col)),
                  pl.BlockSpec((POOL_WIDTH, POOL_WIDTH), lambda i: (0, 0)),
                  pl.BlockSpec((1, POOL_WIDTH), lambda i: (0, 0))],
        out_specs=pl.BlockSpec((SEQ_TILE, POOL_WIDTH), lambda i: (i, 0)),
        out_shape=jax.ShapeDtypeStruct((lay.T, POOL_WIDTH), ACT_DTYPE),
        compiler_params=_cparams("arbitrary"), name="pool",
    )(p, p, p, w_bd, scale.reshape(1, POOL_WIDTH))


def _ret_body(lgs_ref, lgv_ref, q_ref, k_ref, v_ref, *refs, reverse):
    C = SEQ_TILE
    if reverse:
        g_ref, yf_ref, nw_ref, o_ref, s_ref, d_ref, xi_ref, zeta_ref = refs
    else:
        o_ref, s_ref, d_ref, xi_ref, zeta_ref = refs
    b, j = pl.program_id(0), pl.program_id(1)
    lane_head = lax.broadcasted_iota(jnp.int32, (1, RET_WIDTH), 1) // HEAD_DIM
    lgv = lgv_ref[...]

    @pl.when((b == 0) & (j == 0))
    def _():
        ri = lax.broadcasted_iota(jnp.int32, (C, C), 0)
        ci = lax.broadcasted_iota(jnp.int32, (C, C), 1)
        dist = (ci - ri if reverse else ri - ci).astype(F32)
        keep = dist > 0 if reverse else dist >= 0
        for h in range(RET_HEADS):
            d_ref[h * C:(h + 1) * C, :] = jnp.where(keep, jnp.exp(lgs_ref[h] * jnp.maximum(dist, 0.0)), 0.0)
        t = lax.broadcasted_iota(jnp.int32, (C, 1), 0).astype(F32)
        xi_ref[...] = jnp.exp(lgv * ((C - t) if reverse else (t + 1.0)))
        zeta_ref[...] = jnp.exp(lgv * (t if reverse else (C - 1.0 - t)))

    @pl.when(j == 0)
    def _():
        s_ref[...] = jnp.zeros_like(s_ref)

    q, k, v = q_ref[...], k_ref[...], v_ref[...]
    zero = jnp.zeros_like(q)
    qs = jnp.concatenate([jnp.where(lane_head == h, q, zero) for h in range(RET_HEADS)], axis=0)
    sd = _mm_nt(qs, k) * d_ref[...]
    yv = _mm(sd, v)
    y = functools.reduce(jnp.add, [jnp.where(lane_head == h, yv[h * C:(h + 1) * C], 0.0) for h in range(RET_HEADS)])
    state = s_ref[...]
    y = y + _mm(q.astype(F32) * xi_ref[...], state)
    kv = _mm_tn(k.astype(F32) * zeta_ref[...], v)
    row_head = lax.broadcasted_iota(jnp.int32, (RET_WIDTH, 1), 0) // HEAD_DIM
    s_ref[...] = state * jnp.exp(lgv * C) + jnp.where(row_head == lane_head, kv, 0.0)

    if reverse:
        y = y + yf_ref[...]
        same = (row_head == lane_head).astype(F32) * (1.0 / HEAD_DIM)
        ms = _mm_hi(y * y, same)
        yn = y * lax.rsqrt(ms + EPS) * nw_ref[...]
        g = g_ref[...].astype(F32)
        o_ref[...] = ((g * _sigmoid(g)) * yn).astype(o_ref.dtype)
    else:
        o_ref[...] = y


def _retention_pass(lay, p, lgs, lgv, reverse, extra=()):
    B, L = lay.B, lay.L
    C, W = SEQ_TILE, RET_WIDTH
    nch = L // C
    ctx0 = lay.NL // C

    def rows(b, j):
        lat = b * nch + (nch - j if reverse else j - 1)
        return jnp.where(j == 0, ctx0 + b, lat)

    col = lambda c: pl.BlockSpec((C, W), lambda b, j: (rows(b, j), c))
    in_specs = [pl.BlockSpec(memory_space=pltpu.SMEM), pl.BlockSpec((1, W), lambda b, j: (0, 0)),
                col(OFF_RQ // W), col(OFF_RK // W), col(OFF_RV // W)]
    args = [lgs, lgv, p, p, p]
    if reverse:
        yf, nw = extra
        in_specs += [col(OFF_RG // W), col(0), pl.BlockSpec((1, W), lambda b, j: (0, 0))]
        args += [p, yf, nw.reshape(1, W)]
    return pl.pallas_call(
        functools.partial(_ret_body, reverse=reverse),
        grid=(B, nch + 1), in_specs=in_specs, out_specs=col(0),
        out_shape=jax.ShapeDtypeStruct((lay.T, W), ACT_DTYPE if reverse else F32),
        scratch_shapes=[pltpu.VMEM((W, W), F32), pltpu.VMEM((RET_HEADS * C, C), F32),
                        pltpu.VMEM((C, W), F32), pltpu.VMEM((C, W), F32)],
        compiler_params=_cparams("arbitrary", "arbitrary"),
        name="retention_bwd" if reverse else "retention_fwd",
    )(*args)


def _top2(vals):
    n = len(vals)
    v1 = functools.reduce(jnp.maximum, vals)
    i1 = jnp.full_like(v1, n - 1)
    for e in range(n - 2, -1, -1):
        i1 = jnp.where(vals[e] == v1, float(e), i1)
    rest = [jnp.where(i1 == float(e), -jnp.inf, vals[e]) for e in range(n)]
    v2 = functools.reduce(jnp.maximum, rest)
    i2 = jnp.full_like(v1, n - 1)
    for e in range(n - 2, -1, -1):
        i2 = jnp.where(rest[e] == v2, float(e), i2)
    return v1, i1, v2, i2


def _route(logits_t, bias):
    s = _sigmoid(logits_t)
    sel = s + bias
    groups = []
    for g in range(N_GROUPS):
        rows = [sel[g * EXPERTS_PER_GROUP + e:g * EXPERTS_PER_GROUP + e + 1] for e in range(EXPERTS_PER_GROUP)]
        groups.append(_top2(rows))
    score = [v1 + v2 for v1, _, v2, _ in groups]
    best = functools.reduce(jnp.maximum, score)
    gi = jnp.full_like(best, N_GROUPS - 1)
    for g in range(N_GROUPS - 2, -1, -1):
        gi = jnp.where(score[g] == best, float(g), gi)
    pick = lambda k: functools.reduce(
        lambda acc, g: jnp.where(gi == float(g), groups[g][k], acc), range(N_GROUPS - 1), groups[N_GROUPS - 1][k])
    e0 = gi * EXPERTS_PER_GROUP + pick(1)
    e1 = gi * EXPERTS_PER_GROUP + pick(3)
    s0 = jnp.zeros_like(best)
    s1 = jnp.zeros_like(best)
    for e in range(N_EXPERTS):
        s0 = jnp.where(e0 == float(e), s[e:e + 1], s0)
        s1 = jnp.where(e1 == float(e), s[e:e + 1], s1)
    tot = s0 + s1
    return e0, e1, s0 / tot, s1 / tot


def _outproj_body(*refs, lay, with_ctx):
    if with_ctx:
        (x_ref, al_ref, ac_ref, b_ref, r_ref, w_ref, g1_ref, nw_ref, sh_ref, sc_ref, rw_ref, rb_ref, tri_ref,
         xo_ref, hx_ref, cnt_ref, run_ref) = refs
        a = jnp.where(pl.program_id(0) < lay.n_lat_tiles, al_ref[...], ac_ref[...])
    else:
        (x_ref, al_ref, b_ref, r_ref, w_ref, g1_ref, nw_ref, sh_ref, sc_ref, rw_ref, rb_ref, tri_ref,
         xo_ref, hx_ref, cnt_ref, run_ref) = refs
        a = al_ref[...]
    D = lay.D
    y = (_mm(a, w_ref[:DIFF_WIDTH, :])
         + _mm(b_ref[...], w_ref[DIFF_WIDTH:DIFF_WIDTH + POOL_WIDTH, :])
         + _mm(r_ref[...], w_ref[DIFF_WIDTH + POOL_WIDTH:, :]))
    x = x_ref[...] + g1_ref[0] * y
    xo_ref[...] = x
    h = (_rms(x) * nw_ref[...]) * (1.0 + sc_ref[0]) + sh_ref[0]
    hx_ref[:, :D] = h
    logits_t = jnp.transpose(_mm_hi(h, rw_ref[...]))[:N_EXPERTS]
    e0, e1, g0, g1 = _route(logits_t, rb_ref[...])
    lo, hi = jnp.minimum(e0, e1), jnp.maximum(e0, e1)
    grp = jnp.floor(lo * (1.0 / EXPERTS_PER_GROUP))
    pa, pb = lo - grp * EXPERTS_PER_GROUP, hi - grp * EXPERTS_PER_GROUP
    cls = grp * PAIRS_PER_GROUP + pa * ((2 * EXPERTS_PER_GROUP - 1) - pa) * 0.5 + (pb - pa - 1.0)
    rec = jnp.concatenate([e0, e1, g0, g1, cls, jnp.zeros((LANES - 5, TOKEN_TILE), F32)], axis=0)
    rec_t = jnp.transpose(rec)
    lane = lax.broadcasted_iota(jnp.int32, (1, LANES), 1)
    onehot = jnp.where(rec_t[:, REC_CLS:REC_CLS + 1] == lane.astype(F32), 1.0, 0.0)
    incl = jnp.dot(tri_ref[...], onehot.astype(MXU_DTYPE), preferred_element_type=F32)

    @pl.when(pl.program_id(0) == 0)
    def _():
        run_ref[...] = jnp.zeros_like(run_ref)

    run = run_ref[...]
    rank = jnp.sum(onehot * (incl + run), axis=-1, keepdims=True) - 1.0
    run = run + incl[TOKEN_TILE - 1:TOKEN_TILE, :]
    run_ref[...] = run
    cnt_ref[...] = jnp.broadcast_to(run, cnt_ref.shape)
    hx_ref[:, D:] = jnp.where(lane == REC_RANK, rank, rec_t)


def _outproj(lay, layer, n_tok, x, a_lat, a_ctx, b, r, w_out, mods3, nw, rw_pad, rb_col, tri):
    D, TM = lay.D, TOKEN_TILE
    row = lambda i: (i, 0)
    const = lambda i: (0, 0)
    tile = lambda w: pl.BlockSpec((TM, w), row)
    n_lat = lay.n_lat_tiles
    in_specs = [tile(D), pl.BlockSpec((TM, DIFF_WIDTH), lambda i: (jnp.minimum(i, n_lat - 1), 0))]
    args = [x, a_lat]
    if a_ctx is not None:
        in_specs.append(pl.BlockSpec((TM, DIFF_WIDTH), lambda i: (jnp.maximum(i - n_lat, 0), 0)))
        args.append(a_ctx)
    in_specs += [tile(POOL_WIDTH), tile(RET_WIDTH),
                 pl.BlockSpec((DIFF_WIDTH + POOL_WIDTH + RET_WIDTH, D), const),
                 lay.mod_spec(layer, 2), pl.BlockSpec((1, D), const), lay.mod_spec(layer, 3), lay.mod_spec(layer, 4),
                 pl.BlockSpec((D, LANES), const), pl.BlockSpec((N_EXPERTS, 1), const), pl.BlockSpec((TM, TM), const)]
    args += [b, r, w_out, mods3, nw.reshape(1, D), mods3, mods3, rw_pad, rb_col, tri]
    return pl.pallas_call(
        functools.partial(_outproj_body, lay=lay, with_ctx=a_ctx is not None),
        grid=(n_tok // TM,), in_specs=in_specs,
        out_specs=[tile(D), tile(D + LANES), pl.BlockSpec((8, LANES), const)],
        out_shape=[jax.ShapeDtypeStruct((n_tok, D), F32), jax.ShapeDtypeStruct((n_tok, D + LANES), F32),
                   jax.ShapeDtypeStruct((8, LANES), F32)],
        scratch_shapes=[pltpu.VMEM((1, LANES), F32)],
        compiler_params=_cparams("arbitrary"), name="outproj_router",
    )(*args)


def _row_copies(idx_ref, hbm, vmem, sem, scatter):
    def copy(u, row):
        src, dst = (vmem.at[pl.ds(u, 1)], hbm.at[pl.ds(row, 1)])
        return pltpu.make_async_copy(src, dst, sem) if scatter else pltpu.make_async_copy(dst, src, sem)

    def start(u, carry):
        copy(u, idx_ref[0, 0, u]).start()
        return carry

    def wait(u, carry):
        copy(0, 0).wait()
        return carry

    lax.fori_loop(0, PERM_CHUNK, start, 0, unroll=8)
    lax.fori_loop(0, PERM_CHUNK, wait, 0, unroll=8)


def _dispatch_body(pend_ref, padded_ref, nvalid_ref, pos_ref, hx_ref, xs_hbm, zbuf, zsem, sem, *, n_blk):
    TB = MOE_TILE
    i = pl.program_id(0)

    @pl.when(i == 0)
    def _():
        zbuf[...] = jnp.zeros_like(zbuf)
        fill = lambda row: pltpu.make_async_copy(zbuf, xs_hbm.at[pl.ds(row, TB)], zsem.at[0])
        for k in range(N_CLASSES):
            @pl.when(padded_ref[k] > 0)
            def _():
                fill(pl.multiple_of(pend_ref[k] - TB, TB)).start()
        for j in range(n_blk):
            @pl.when(j >= nvalid_ref[0])
            def _():
                fill(j * TB).start()
        for k in range(N_CLASSES):
            @pl.when(padded_ref[k] > 0)
            def _():
                fill(0).wait()
        for j in range(n_blk):
            @pl.when(j >= nvalid_ref[0])
            def _():
                fill(0).wait()

    _row_copies(pos_ref, xs_hbm, hx_ref, sem.at[0], True)


def _combine_body(pos_ref, ys_hbm, f_ref, sem):
    _row_copies(pos_ref, ys_hbm, f_ref, sem.at[0], False)


def _moe_body(elo_ref, ehi_ref, nvalid_ref, xs_ref, wg_lo, wu_lo, wd_lo, wg_hi, wu_hi, wd_hi, ys_ref, *, D):
    j = pl.program_id(0)

    @pl.when(j < nvalid_ref[0])
    def _():
        xb = xs_ref[:, :D].astype(MXU_DTYPE)
        info = xs_ref[:, D:]
        first_is_lo = info[:, REC_E0:REC_E0 + 1] <= info[:, REC_E1:REC_E1 + 1]
        g0, g1 = info[:, REC_G0:REC_G0 + 1], info[:, REC_G1:REC_G1 + 1]
        g_lo = jnp.where(first_is_lo, g0, g1)
        g_hi = jnp.where(first_is_lo, g1, g0)

        def ffn(wg, wu, wd):
            hg = jnp.dot(xb, wg[...], preferred_element_type=F32)
            hu = jnp.dot(xb, wu[...], preferred_element_type=F32)
            return _mm((hg * _sigmoid(hg)) * hu, wd[...])

        ys_ref[...] = g_lo * ffn(wg_lo, wu_lo, wd_lo) + g_hi * ffn(wg_hi, wu_hi, wd_hi)

    @pl.when(j >= nvalid_ref[0])
    def _():
        ys_ref[...] = jnp.zeros_like(ys_ref)


def _moe(n_tok, D, hx, counts, wg, wu, wd):
    TB = MOE_TILE
    DE = wg.shape[-1]
    assert n_tok % PERM_CHUNK == 0
    cls = hx[:, D + REC_CLS].astype(jnp.int32)
    rank = hx[:, D + REC_RANK].astype(jnp.int32)
    cnt = counts[0, :N_CLASSES].astype(jnp.int32)
    padded = (cnt + TB - 1) // TB * TB
    pend = jnp.cumsum(padded)
    pstart = pend - padded
    onehot = cls[:, None] == jnp.arange(N_CLASSES, dtype=jnp.int32)[None, :]
    pos = jnp.sum(jnp.where(onehot, pstart[None, :], 0), axis=1) + rank
    n_blk = -(-n_tok // TB) + N_CLASSES
    P = n_blk * TB
    nvalid = (pend[-1] // TB).astype(jnp.int32).reshape(1)
    blk = jnp.minimum(jnp.arange(n_blk, dtype=jnp.int32), nvalid - 1)
    blk_cls = jnp.minimum(jnp.sum(pend[None, :] <= (blk * TB)[:, None], axis=1), N_CLASSES - 1).astype(jnp.int32)
    pairs = [(u, v) for u in range(EXPERTS_PER_GROUP) for v in range(u + 1, EXPERTS_PER_GROUP)]
    pair_lo = jnp.array([u for u, _ in pairs], jnp.int32)
    pair_hi = jnp.array([v for _, v in pairs], jnp.int32)
    base = (blk_cls // PAIRS_PER_GROUP) * EXPERTS_PER_GROUP
    blk_lo = base + pair_lo[blk_cls % PAIRS_PER_GROUP]
    blk_hi = base + pair_hi[blk_cls % PAIRS_PER_GROUP]
    pos3 = pos.reshape(n_tok // PERM_CHUNK, 1, PERM_CHUNK)
    pos_spec = pl.BlockSpec((1, 1, PERM_CHUNK), lambda i, *_: (i, 0, 0), memory_space=pltpu.SMEM)
    any_spec = pl.BlockSpec(memory_space=pl.ANY)

    xs = pl.pallas_call(
        functools.partial(_dispatch_body, n_blk=n_blk),
        grid_spec=pltpu.PrefetchScalarGridSpec(
            num_scalar_prefetch=3, grid=(n_tok // PERM_CHUNK,),
            in_specs=[pos_spec, pl.BlockSpec((PERM_CHUNK, D + LANES), lambda i, *_: (i, 0))], out_specs=any_spec,
            scratch_shapes=[pltpu.VMEM((TB, D + LANES), F32), pltpu.SemaphoreType.DMA((1,)),
                            pltpu.SemaphoreType.DMA((1,))]),
        out_shape=jax.ShapeDtypeStruct((P, D + LANES), F32),
        compiler_params=_cparams("arbitrary"), name="moe_dispatch",
    )(pend.astype(jnp.int32), padded.astype(jnp.int32), nvalid, pos3, hx)

    w_spec = lambda shape, which: pl.BlockSpec(
        (None,) + shape, (lambda j, elo, ehi, nv: (elo[j], 0, 0)) if which == 0 else (lambda j, elo, ehi, nv: (ehi[j], 0, 0)))
    ys = pl.pallas_call(
        functools.partial(_moe_body, D=D),
        grid_spec=pltpu.PrefetchScalarGridSpec(
            num_scalar_prefetch=3, grid=(n_blk,),
            in_specs=[pl.BlockSpec((TB, D + LANES), lambda j, elo, ehi, nv: (jnp.minimum(j, nv[0] - 1), 0)),
                      w_spec((D, DE), 0), w_spec((D, DE), 0), w_spec((DE, D), 0),
                      w_spec((D, DE), 1), w_spec((D, DE), 1), w_spec((DE, D), 1)],
            out_specs=pl.BlockSpec((TB, D), lambda j, elo, ehi, nv: (j, 0))),
        out_shape=jax.ShapeDtypeStruct((P, D), F32),
        compiler_params=_cparams("arbitrary"), name="moe_experts",
    )(blk_lo, blk_hi, nvalid, xs, wg, wu, wd, wg, wu, wd)

    return pl.pallas_call(
        _combine_body,
        grid_spec=pltpu.PrefetchScalarGridSpec(
            num_scalar_prefetch=0, grid=(n_tok // PERM_CHUNK,), in_specs=[pos_spec, any_spec],
            out_specs=pl.BlockSpec((PERM_CHUNK, D), lambda i: (i, 0)),
            scratch_shapes=[pltpu.SemaphoreType.DMA((1,))]),
        out_shape=jax.ShapeDtypeStruct((n_tok, D), F32),
        compiler_params=_cparams("arbitrary"), name="moe_combine",
    )(pos3, ys)


def _final_body(x_ref, f_ref, g2_ref, nw_ref, o_ref):
    o_ref[...] = _rms(x_ref[...] + g2_ref[0] * f_ref[...]) * nw_ref[...]


def _final(lay, layer, x, f, mods3, nw):
    D, TM = lay.D, TOKEN_TILE
    row = lambda i: (i, 0)
    return pl.pallas_call(
        _final_body, grid=(lay.NL // TM,),
        in_specs=[pl.BlockSpec((TM, D), row), pl.BlockSpec((TM, D), row), lay.mod_spec(layer, 5),
                  pl.BlockSpec((1, D), lambda i: (0, 0))],
        out_specs=pl.BlockSpec((TM, D), row),
        out_shape=jax.ShapeDtypeStruct((lay.NL, D), F32),
        compiler_params=_cparams("arbitrary"), name="final_norm",
    )(x, f, mods3, nw.reshape(1, D))


def kernel(x, c, ctx, c_ctx, w_in, w_out, ada_w, ada_b, norm1_w, norm2_w, diff_lambda, diff_norm_w, pool_w,
           pool_scale, ret_a_f, ret_a_b, ret_norm_w, router_w, router_b, moe_w_gate, moe_w_up, moe_w_down,
           final_norm_w):
    B, L, D = x.shape
    Lc = ctx.shape[1]
    depth = w_in.shape[0]
    lay = _Layout(B, L, Lc, D)
    assert B + 1 <= MOD_ROWS

    xa = jnp.concatenate([x.reshape(B * L, D), ctx.reshape(B * Lc, D)], axis=0)
    c_all = jnp.zeros((MOD_ROWS, D), F32).at[:B].set(c).at[B].set(c_ctx)
    mods3 = _ada_table(c_all, ada_w, ada_b).reshape(depth * MOD_ROWS * N_MOD, 1, D)
    tables = _rope_tables(lay)
    rw_pad = jnp.zeros((D, LANES), F32).at[:, :N_EXPERTS].set(router_w.astype(F32))
    rb_col = router_b.astype(F32).reshape(N_EXPERTS, 1)
    tri = jnp.tril(jnp.ones((TOKEN_TILE, TOKEN_TILE), F32)).astype(MXU_DTYPE)
    w_in_b, w_out_b = w_in.astype(MXU_DTYPE), w_out.astype(MXU_DTYPE)
    wg_b, wu_b, wd_b = moe_w_gate.astype(MXU_DTYPE), moe_w_up.astype(MXU_DTYPE), moe_w_down.astype(MXU_DTYPE)

    f = None
    for l in range(depth):
        last = l == depth - 1
        lam_init = 0.8 - 0.6 * math.exp(-0.3 * l)
        dl = diff_lambda[l].astype(F32)
        lam = (jnp.exp(jnp.sum(dl[0] * dl[1])) - jnp.exp(jnp.sum(dl[2] * dl[3])) + lam_init).reshape(1)
        p, xa = _inproj(lay, l, xa, f, mods3, norm1_w[l], w_in_b[l], tables)
        a_lat, a_ctx = _diff_attention(lay, p, lam, diff_norm_w[l], 1.0 - lam_init, not last)
        eye = jnp.eye(len(POOL_WINDOWS), dtype=F32)
        w_bd = (eye[:, None, :, None] * pool_w[l][:, :, None, :]).reshape(POOL_WIDTH, POOL_WIDTH).astype(MXU_DTYPE)
        bp = _pool(lay, p, w_bd, pool_scale[l])
        lg_f = -jnp.exp(ret_a_f[l].astype(F32))
        lg_b = -jnp.exp(ret_a_b[l].astype(F32))
        lanes = lambda lg: jnp.repeat(lg, HEAD_DIM).reshape(1, RET_WIDTH)
        yf = _retention_pass(lay, p, lg_f, lanes(lg_f), False)
        r = _retention_pass(lay, p, lg_b, lanes(lg_b), True, (yf, ret_norm_w[l]))
        n_tok = lay.NL if last else lay.T
        xa, hx, counts = _outproj(lay, l, n_tok, xa, a_lat, a_ctx, bp, r, w_out_b[l], mods3, norm2_w[l],
                                  rw_pad, rb_col, tri)
        f = _moe(n_tok, D, hx, counts, wg_b[l], wu_b[l], wd_b[l])
    out = _final(lay, depth - 1, xa, f, mods3, final_norm_w)
    return out.reshape(B, L, D)
```

```python
import functools
import math

import jax
import jax.numpy as jnp
from jax import lax
from jax.experimental import pallas as pl
from jax.experimental.pallas import tpu as pltpu

F32 = jnp.float32
MXU_DTYPE = jnp.bfloat16
ACT_DTYPE = jnp.bfloat16

GRID_W = 64
HEAD_DIM = 64
DIFF_HEADS = 4
DIFF_V_DIM = 2 * HEAD_DIM
DIFF_WIDTH = DIFF_HEADS * DIFF_V_DIM
POOL_WINDOWS = (2, 4, 8, 16)
POOL_GROUP = 64
POOL_WIDTH = POOL_GROUP * len(POOL_WINDOWS)
RET_HEADS = 4
RET_WIDTH = RET_HEADS * HEAD_DIM
OFF_DQ = 0
OFF_DK = OFF_DQ + DIFF_WIDTH
OFF_DV = OFF_DK + DIFF_WIDTH
OFF_PU = OFF_DV + DIFF_WIDTH
OFF_RQ = OFF_PU + POOL_WIDTH
OFF_RK = OFF_RQ + RET_WIDTH
OFF_RV = OFF_RK + RET_WIDTH
OFF_RG = OFF_RV + RET_WIDTH
IN_WIDTH = OFF_RG + RET_WIDTH
ROPE_BASE = 10000.0
ROPE_FREQS = HEAD_DIM // 4
N_EXPERTS = 16
N_GROUPS = 4
EXPERTS_PER_GROUP = N_EXPERTS // N_GROUPS
PAIRS_PER_GROUP = EXPERTS_PER_GROUP * (EXPERTS_PER_GROUP - 1) // 2
N_CLASSES = N_GROUPS * PAIRS_PER_GROUP
N_MOD = 6
EPS = 1e-6

LANES = 128
MOD_ROWS = 16
TOKEN_TILE = 512
SEQ_TILE = 256
ATTN_SUBTILES = 2
ATTN_SUB_ROWS = 256
ATTN_KEY_CHUNK = 512
HALO = 16
MOE_TILE = 256
COL = 256
PERM_CHUNK = 2048
REC_E0, REC_E1, REC_G0, REC_G1, REC_CLS, REC_RANK = range(6)
VMEM_LIMIT = 56 * 1024 * 1024


def _mm(a, b):
    return jnp.dot(a.astype(MXU_DTYPE), b.astype(MXU_DTYPE), preferred_element_type=F32)


def _mm_nt(a, b):
    return lax.dot_general(a.astype(MXU_DTYPE), b.astype(MXU_DTYPE), (((1,), (1,)), ((), ())),
                           preferred_element_type=F32)


def _mm_tn(a, b):
    return lax.dot_general(a.astype(MXU_DTYPE), b.astype(MXU_DTYPE), (((0,), (0,)), ((), ())),
                           preferred_element_type=F32)


def _split(x):
    hi = x.astype(MXU_DTYPE)
    lo = (x - hi.astype(F32)).astype(MXU_DTYPE)
    return hi, lo


def _mm_hi(a, b):
    ah, al = _split(a)
    bh, bl = _split(b)
    d = lambda u, v: jnp.dot(u, v, preferred_element_type=F32)
    return d(ah, bh) + (d(ah, bl) + d(al, bh))


def _sigmoid(x):
    return 1.0 / (1.0 + jnp.exp(-x))


def _rms(x):
    return x * lax.rsqrt(jnp.mean(x * x, axis=-1, keepdims=True) + EPS)


def _cparams(*sem):
    return pltpu.CompilerParams(dimension_semantics=sem, vmem_limit_bytes=VMEM_LIMIT)


def _ada_body(c_ref, w_ref, b_ref, o_ref):
    c = c_ref[...]
    o_ref[...] = _mm_hi(c * _sigmoid(c), w_ref[...]) + b_ref[...]


def _ada_table(c_all, ada_w, ada_b):
    depth, d, n = ada_w.shape
    nb = n // N_MOD
    return pl.pallas_call(
        _ada_body,
        grid=(depth, n // nb),
        in_specs=[pl.BlockSpec((MOD_ROWS, d), lambda l, j: (0, 0)),
                  pl.BlockSpec((None, d, nb), lambda l, j: (l, 0, j)),
                  pl.BlockSpec((None, 1, nb), lambda l, j: (l, 0, j))],
        out_specs=pl.BlockSpec((None, MOD_ROWS, nb), lambda l, j: (l, 0, j)),
        out_shape=jax.ShapeDtypeStruct((depth, MOD_ROWS, n), F32),
        compiler_params=_cparams("arbitrary", "arbitrary"),
        name="ada_table",
    )(c_all, ada_w, ada_b.reshape(depth, 1, n))


class _Layout:
    def __init__(self, B, L, Lc, D):
        self.B, self.L, self.Lc, self.D = B, L, Lc, D
        self.NL, self.NC = B * L, B * Lc
        self.T = self.NL + self.NC
        assert L % TOKEN_TILE == 0 and self.NC % TOKEN_TILE == 0
        assert L % SEQ_TILE == 0 and Lc == SEQ_TILE and L % GRID_W == 0
        self.n_lat_tiles = self.NL // TOKEN_TILE
        self.tiles_per_seq = L // TOKEN_TILE

    def mod_row(self, i):
        return jnp.where(i < self.n_lat_tiles, i // self.tiles_per_seq, self.B)

    def rope_block(self, i):
        return jnp.where(i < self.n_lat_tiles, i % self.tiles_per_seq, self.tiles_per_seq)

    def mod_spec(self, layer, k):
        base = layer * MOD_ROWS * N_MOD + k
        return pl.BlockSpec((1, 1, self.D), lambda i: (base + self.mod_row(i) * N_MOD, 0, 0))


def _rope_tables(lay):
    L = lay.L
    rows = L // GRID_W
    row = jnp.repeat(jnp.arange(rows), GRID_W).astype(F32)
    col = jnp.tile(jnp.arange(GRID_W), rows).astype(F32)
    inv = ROPE_BASE ** (-jnp.arange(ROPE_FREQS, dtype=F32) / ROPE_FREQS)
    ang_r = row[:, None] * inv
    ang_c = col[:, None] * inv
    ang = jnp.concatenate([ang_r, ang_r, ang_c, ang_c], axis=-1)
    cos, sin = jnp.cos(ang), jnp.sin(ang)
    first_half = (jnp.arange(HEAD_DIM) % (2 * ROPE_FREQS)) < ROPE_FREQS
    sin_a = jnp.where(first_half, -sin, 0.0)
    sin_b = jnp.where(first_half, 0.0, sin)
    rep = LANES // HEAD_DIM
    ident = [jnp.ones((TOKEN_TILE, LANES), F32), jnp.zeros((TOKEN_TILE, LANES), F32),
             jnp.zeros((TOKEN_TILE, LANES), F32)]
    return [jnp.concatenate([jnp.tile(t, (1, rep)), e], axis=0) for t, e in zip((cos, sin_a, sin_b), ident)]


_ROPE_COLS = tuple(range(OFF_DQ // COL, OFF_DV // COL)) + (OFF_RQ // COL, OFF_RK // COL)
LOG2E = 1.4426950408889634
_COL_SCALE = {j: HEAD_DIM ** -0.5 * LOG2E for j in range(OFF_DQ // COL, OFF_DK // COL)}
_COL_SCALE[OFF_RK // COL] = HEAD_DIM ** -0.5


def _inproj_body(*refs, with_f):
    if with_f:
        (x_ref, f_ref, g2_ref, nw_ref, sh_ref, sc_ref, w_ref, cos_ref, sa_ref, sb_ref, p_ref, xo_ref) = refs
        x = x_ref[...] + g2_ref[0] * f_ref[...]
        xo_ref[...] = x
    else:
        (x_ref, nw_ref, sh_ref, sc_ref, w_ref, cos_ref, sa_ref, sb_ref, p_ref) = refs
        x = x_ref[...]
    h = (_rms(x) * nw_ref[...]) * (1.0 + sc_ref[0]) + sh_ref[0]
    hb = h.astype(MXU_DTYPE)
    cos, sa, sb = cos_ref[...], sa_ref[...], sb_ref[...]
    for j in range(IN_WIDTH // COL):
        acc = jnp.dot(hb, w_ref[:, j * COL:(j + 1) * COL], preferred_element_type=F32)
        if j in _ROPE_COLS:
            parts = []
            for t in range(COL // LANES):
                a = acc[:, t * LANES:(t + 1) * LANES]
                parts.append(a * cos + pltpu.roll(a, LANES - ROPE_FREQS, 1) * sa + pltpu.roll(a, ROPE_FREQS, 1) * sb)
            acc = jnp.concatenate(parts, axis=1)
        if j in _COL_SCALE:
            acc = acc * _COL_SCALE[j]
        p_ref[:, j * COL:(j + 1) * COL] = acc.astype(p_ref.dtype)


def _inproj(lay, layer, x, f, mods3, nw, w_in, tables):
    D, TM = lay.D, TOKEN_TILE
    row = lambda i: (i, 0)
    const = lambda i: (0, 0)
    tbl = pl.BlockSpec((TM, LANES), lambda i: (lay.rope_block(i), 0))
    in_specs = [pl.BlockSpec((TM, D), row)]
    args = [x]
    if f is not None:
        in_specs += [pl.BlockSpec((TM, D), row), lay.mod_spec(layer - 1, 5)]
        args += [f, mods3]
    in_specs += [pl.BlockSpec((1, D), const), lay.mod_spec(layer, 0), lay.mod_spec(layer, 1),
                 pl.BlockSpec((D, IN_WIDTH), const), tbl, tbl, tbl]
    args += [nw.reshape(1, D), mods3, mods3, w_in] + list(tables)
    out_specs = [pl.BlockSpec((TM, IN_WIDTH), row)]
    out_shape = [jax.ShapeDtypeStruct((lay.T, IN_WIDTH), ACT_DTYPE)]
    if f is not None:
        out_specs.append(pl.BlockSpec((TM, D), row))
        out_shape.append(jax.ShapeDtypeStruct((lay.T, D), F32))
    outs = pl.pallas_call(
        functools.partial(_inproj_body, with_f=f is not None),
        grid=(lay.T // TM,), in_specs=in_specs, out_specs=out_specs, out_shape=out_shape,
        compiler_params=_cparams("arbitrary"), name="inproj",
    )(*args)
    return (outs[0], outs[1]) if f is not None else (outs[0], x)


def _map_masks(q):
    lane = lax.broadcasted_iota(jnp.int32, (1, DIFF_V_DIM), 1)
    zero = jnp.zeros_like(q)
    return [jnp.where(lane < HEAD_DIM, q, zero), jnp.where(lane >= HEAD_DIM, q, zero)]


def _scores(qm, k_refs):
    chunks, run = [], None
    for k in k_refs:
        for c0 in range(0, k.shape[0], ATTN_KEY_CHUNK):
            s = _mm_nt(qm, k[c0:c0 + ATTN_KEY_CHUNK, :])
            chunks.append(s)
            for t in range(s.shape[1] // LANES):
                piece = s[:, t * LANES:(t + 1) * LANES]
                run = piece if run is None else jnp.maximum(run, piece)
    return chunks, jnp.max(run, axis=-1, keepdims=True)


def _softmax_diff(maps, lam):
    es, ls = [], []
    for chunks, mx in maps:
        e = [jnp.exp2(t - mx) for t in chunks]
        ls.append(functools.reduce(jnp.add, [jnp.sum(t, axis=-1, keepdims=True) for t in e]))
        es.append([t.astype(MXU_DTYPE) for t in e])
    c = (lam * ls[0] / ls[1]).astype(MXU_DTYPE)
    return [e1 - e2 * c for e1, e2 in zip(es[0], es[1])], 1.0 / ls[0]


def _head_norm(o, w, post_scale):
    return (_rms(o) * w) * post_scale


def _attn_body(lam_ref, q_ref, *refs, n_seg, n_sub, post_scale):
    k_refs, v_refs = refs[:n_seg], refs[n_seg:2 * n_seg]
    w_ref, o_ref = refs[2 * n_seg], refs[2 * n_seg + 1]
    rows = q_ref.shape[0] // n_sub

    def finish(u, maps):
        p, r1 = _softmax_diff(maps, lam_ref[0])
        o, at = None, 0
        for v in v_refs:
            n_c = -(-v.shape[0] // ATTN_KEY_CHUNK)
            part = jnp.dot(jnp.concatenate(p[at:at + n_c], axis=1), v[...], preferred_element_type=F32)
            o = part if o is None else o + part
            at += n_c
        o_ref[u * rows:(u + 1) * rows, :] = _head_norm(o * r1, w_ref[...], post_scale).astype(o_ref.dtype)

    scores = [[_scores(qm, k_refs) for qm in _map_masks(q_ref[u * rows:(u + 1) * rows, :])] for u in range(n_sub)]
    for u, maps in enumerate(scores):
        finish(u, maps)


def _diff_attention(lay, p, lam, norm_w, post_scale, need_ctx):
    B, L, Lc, H = lay.B, lay.L, lay.Lc, DIFF_HEADS
    W = DIFF_V_DIM
    tq = ATTN_SUBTILES * ATTN_SUB_ROWS
    assert L % tq == 0
    nq = L // tq
    k_blk, v_blk = OFF_DK // W, OFF_DV // W
    ctx0 = lay.NL // Lc
    lam_spec = pl.BlockSpec(memory_space=pltpu.SMEM)
    w_spec = pl.BlockSpec((1, W), lambda *_: (0, 0))
    a_lat = pl.pallas_call(
        functools.partial(_attn_body, n_seg=2, n_sub=ATTN_SUBTILES, post_scale=post_scale),
        grid=(B, H, nq),
        in_specs=[lam_spec,
                  pl.BlockSpec((tq, W), lambda b, h, i: (b * nq + i, h)),
                  pl.BlockSpec((Lc, W), lambda b, h, i: (ctx0 + b, k_blk + h)),
                  pl.BlockSpec((L, W), lambda b, h, i: (b, k_blk + h)),
                  pl.BlockSpec((Lc, W), lambda b, h, i: (ctx0 + b, v_blk + h)),
                  pl.BlockSpec((L, W), lambda b, h, i: (b, v_blk + h)),
                  w_spec],
        out_specs=pl.BlockSpec((tq, W), lambda b, h, i: (b * nq + i, h)),
        out_shape=jax.ShapeDtypeStruct((lay.NL, DIFF_WIDTH), ACT_DTYPE),
        compiler_params=_cparams("arbitrary", "arbitrary", "arbitrary"),
        name="diff_attn",
    )(lam, p, p, p, p, p, norm_w.reshape(1, W))
    if not need_ctx:
        return a_lat, None
    a_ctx = pl.pallas_call(
        functools.partial(_attn_body, n_seg=1, n_sub=1, post_scale=post_scale),
        grid=(B, H),
        in_specs=[lam_spec,
                  pl.BlockSpec((Lc, W), lambda b, h: (ctx0 + b, h)),
                  pl.BlockSpec((Lc, W), lambda b, h: (ctx0 + b, k_blk + h)),
                  pl.BlockSpec((Lc, W), lambda b, h: (ctx0 + b, v_blk + h)),
                  w_spec],
        out_specs=pl.BlockSpec((Lc, W), lambda b, h: (b, h)),
        out_shape=jax.ShapeDtypeStruct((lay.NC, DIFF_WIDTH), ACT_DTYPE),
        compiler_params=_cparams("arbitrary", "arbitrary"),
        name="diff_attn_ctx",
    )(lam, p, p, p, norm_w.reshape(1, W))
    return a_lat, a_ctx


def _pool_body(prev_ref, cur_ref, next_ref, w_ref, scale_ref, o_ref, *, lay):
    i = pl.program_id(0)
    n_lat = lay.NL // SEQ_TILE
    per_seq = jnp.where(i < n_lat, lay.L // SEQ_TILE, lay.Lc // SEQ_TILE)
    idx = jnp.where(i < n_lat, i, i - n_lat) % per_seq
    seq_len = per_seq * SEQ_TILE
    has_prev = (idx > 0).astype(F32)
    has_next = (idx < per_seq - 1).astype(F32)
    cur = cur_ref[...].astype(F32)
    u = jnp.concatenate([prev_ref[...].astype(F32) * has_prev, cur, next_ref[...].astype(F32) * has_next], axis=0)
    sums = {1: u}
    w = 1
    while w < POOL_WINDOWS[-1]:
        s = sums[w]
        m = s.shape[0] - w
        sums[2 * w] = s[:m] + s[w:w + m]
        w *= 2
    pos = idx * SEQ_TILE + lax.broadcasted_iota(jnp.int32, (SEQ_TILE, 1), 0)
    group = lax.broadcasted_iota(jnp.int32, (1, POOL_WIDTH), 1) // POOL_GROUP
    mean = jnp.zeros((SEQ_TILE, POOL_WIDTH), F32)
    for g, win in enumerate(POOL_WINDOWS):
        start = HALO - win // 2
        cnt = jnp.minimum(pos + (win - win // 2), seq_len) - jnp.maximum(pos - win // 2, 0)
        mean = jnp.where(group == g, sums[win][start:start + SEQ_TILE] / cnt.astype(F32), mean)
    o_ref[...] = (_mm(mean - cur, w_ref[...]) * scale_ref[...]).astype(o_ref.dtype)


def _pool(lay, p, w_bd, scale):
    n = lay.T // SEQ_TILE
    col = OFF_PU // POOL_WIDTH
    per = SEQ_TILE // HALO
    last = lay.T // HALO - 1
    return pl.pallas_call(
        functools.partial(_pool_body, lay=lay),
        grid=(n,),
        in_specs=[pl.BlockSpec((HALO, POOL_WIDTH), lambda i: (jnp.maximum(i * per - 1, 0), col)),
                  pl.BlockSpec((SEQ_TILE, POOL_WIDTH), lambda i: (i, col)),
                  pl.BlockSpec((HALO, POOL_WIDTH), lambda i: (jnp.minimum((i + 1) * per, last), col)),
                  pl.BlockSpec((POOL_WIDTH, POOL_WIDTH), lambda i: (0, 0)),
                  pl.BlockSpec((1, POOL_WIDTH), lambda i: (0, 0))],
        out_specs=pl.BlockSpec((SEQ_TILE, POOL_WIDTH), lambda i: (i, 0)),
        out_shape=jax.ShapeDtypeStruct((lay.T, POOL_WIDTH), ACT_DTYPE),
        compiler_params=_cparams("arbitrary"), name="pool",
    )(p, p, p, w_bd, scale.reshape(1, POOL_WIDTH))


def _ret_body(lgs_ref, lgv_ref, q_ref, k_ref, v_ref, *refs, reverse):
    C = SEQ_TILE
    if reverse:
        g_ref, yf_ref, nw_ref, o_ref, s_ref, d_ref, xi_ref, zeta_ref = refs
    else:
        o_ref, s_ref, d_ref, xi_ref, zeta_ref = refs
    b, j = pl.program_id(0), pl.program_id(1)
    lane_head = lax.broadcasted_iota(jnp.int32, (1, RET_WIDTH), 1) // HEAD_DIM
    lgv = lgv_ref[...]

    @pl.when((b == 0) & (j == 0))
    def _():
        ri = lax.broadcasted_iota(jnp.int32, (C, C), 0)
        ci = lax.broadcasted_iota(jnp.int32, (C, C), 1)
        dist = (ci - ri if reverse else ri - ci).astype(F32)
        keep = dist > 0 if reverse else dist >= 0
        for h in range(RET_HEADS):
            d_ref[h * C:(h + 1) * C, :] = jnp.where(keep, jnp.exp(lgs_ref[h] * jnp.maximum(dist, 0.0)), 0.0)
        t = lax.broadcasted_iota(jnp.int32, (C, 1), 0).astype(F32)
        xi_ref[...] = jnp.exp(lgv * ((C - t) if reverse else (t + 1.0)))
        zeta_ref[...] = jnp.exp(lgv * (t if reverse else (C - 1.0 - t)))

    @pl.when(j == 0)
    def _():
        s_ref[...] = jnp.zeros_like(s_ref)

    q, k, v = q_ref[...], k_ref[...], v_ref[...]
    zero = jnp.zeros_like(q)
    qs = jnp.concatenate([jnp.where(lane_head == h, q, zero) for h in range(RET_HEADS)], axis=0)
    sd = _mm_nt(qs, k) * d_ref[...]
    yv = _mm(sd, v)
    y = functools.reduce(jnp.add, [jnp.where(lane_head == h, yv[h * C:(h + 1) * C], 0.0) for h in range(RET_HEADS)])
    state = s_ref[...]
    y = y + _mm(q.astype(F32) * xi_ref[...], state)
    kv = _mm_tn(k.astype(F32) * zeta_ref[...], v)
    row_head = lax.broadcasted_iota(jnp.int32, (RET_WIDTH, 1), 0) // HEAD_DIM
    s_ref[...] = state * jnp.exp(lgv * C) + jnp.where(row_head == lane_head, kv, 0.0)

    if reverse:
        y = y + yf_ref[...]
        same = (row_head == lane_head).astype(F32) * (1.0 / HEAD_DIM)
        ms = _mm_hi(y * y, same)
        yn = y * lax.rsqrt(ms + EPS) * nw_ref[...]
        g = g_ref[...].astype(F32)
        o_ref[...] = ((g * _sigmoid(g)) * yn).astype(o_ref.dtype)
    else:
        o_ref[...] = y


def _retention_pass(lay, p, lgs, lgv, reverse, extra=()):
    B, L = lay.B, lay.L
    C, W = SEQ_TILE, RET_WIDTH
    nch = L // C
    ctx0 = lay.NL // C

    def rows(b, j):
        lat = b * nch + (nch - j if reverse else j - 1)
        return jnp.where(j == 0, ctx0 + b, lat)

    col = lambda c: pl.BlockSpec((C, W), lambda b, j: (rows(b, j), c))
    in_specs = [pl.BlockSpec(memory_space=pltpu.SMEM), pl.BlockSpec((1, W), lambda b, j: (0, 0)),
                col(OFF_RQ // W), col(OFF_RK // W), col(OFF_RV // W)]
    args = [lgs, lgv, p, p, p]
    if reverse:
        yf, nw = extra
        in_specs += [col(OFF_RG // W), col(0), pl.BlockSpec((1, W), lambda b, j: (0, 0))]
        args += [p, yf, nw.reshape(1, W)]
    return pl.pallas_call(
        functools.partial(_ret_body, reverse=reverse),
        grid=(B, nch + 1), in_specs=in_specs, out_specs=col(0),
        out_shape=jax.ShapeDtypeStruct((lay.T, W), ACT_DTYPE if reverse else F32),
        scratch_shapes=[pltpu.VMEM((W, W), F32), pltpu.VMEM((RET_HEADS * C, C), F32),
                        pltpu.VMEM((C, W), F32), pltpu.VMEM((C, W), F32)],
        compiler_params=_cparams("arbitrary", "arbitrary"),
        name="retention_bwd" if reverse else "retention_fwd",
    )(*args)


def _top2(vals):
    n = len(vals)
    v1 = functools.reduce(jnp.maximum, vals)
    i1 = jnp.full_like(v1, n - 1)
    for e in range(n - 2, -1, -1):
        i1 = jnp.where(vals[e] == v1, float(e), i1)
    rest = [jnp.where(i1 == float(e), -jnp.inf, vals[e]) for e in range(n)]
    v2 = functools.reduce(jnp.maximum, rest)
    i2 = jnp.full_like(v1, n - 1)
    for e in range(n - 2, -1, -1):
        i2 = jnp.where(rest[e] == v2, float(e), i2)
    return v1, i1, v2, i2


def _route(logits_t, bias):
    s = _sigmoid(logits_t)
    sel = s + bias
    groups = []
    for g in range(N_GROUPS):
        rows = [sel[g * EXPERTS_PER_GROUP + e:g * EXPERTS_PER_GROUP + e + 1] for e in range(EXPERTS_PER_GROUP)]
        groups.append(_top2(rows))
    score = [v1 + v2 for v1, _, v2, _ in groups]
    best = functools.reduce(jnp.maximum, score)
    gi = jnp.full_like(best, N_GROUPS - 1)
    for g in range(N_GROUPS - 2, -1, -1):
        gi = jnp.where(score[g] == best, float(g), gi)
    pick = lambda k: functools.reduce(
        lambda acc, g: jnp.where(gi == float(g), groups[g][k], acc), range(N_GROUPS - 1), groups[N_GROUPS - 1][k])
    e0 = gi * EXPERTS_PER_GROUP + pick(1)
    e1 = gi * EXPERTS_PER_GROUP + pick(3)
    s0 = jnp.zeros_like(best)
    s1 = jnp.zeros_like(best)
    for e in range(N_EXPERTS):
        s0 = jnp.where(e0 == float(e), s[e:e + 1], s0)
        s1 = jnp.where(e1 == float(e), s[e:e + 1], s1)
    tot = s0 + s1
    return e0, e1, s0 / tot, s1 / tot


def _outproj_body(*refs, lay, with_ctx):
    if with_ctx:
        (x_ref, al_ref, ac_ref, b_ref, r_ref, w_ref, g1_ref, nw_ref, sh_ref, sc_ref, rw_ref, rb_ref, tri_ref,
         xo_ref, hx_ref, cnt_ref, run_ref) = refs
        a = jnp.where(pl.program_id(0) < lay.n_lat_tiles, al_ref[...], ac_ref[...])
    else:
        (x_ref, al_ref, b_ref, r_ref, w_ref, g1_ref, nw_ref, sh_ref, sc_ref, rw_ref, rb_ref, tri_ref,
         xo_ref, hx_ref, cnt_ref, run_ref) = refs
        a = al_ref[...]
    D = lay.D
    y = (_mm(a, w_ref[:DIFF_WIDTH, :])
         + _mm(b_ref[...], w_ref[DIFF_WIDTH:DIFF_WIDTH + POOL_WIDTH, :])
         + _mm(r_ref[...], w_ref[DIFF_WIDTH + POOL_WIDTH:, :]))
    x = x_ref[...] + g1_ref[0] * y
    xo_ref[...] = x
    h = (_rms(x) * nw_ref[...]) * (1.0 + sc_ref[0]) + sh_ref[0]
    hx_ref[:, :D] = h
    logits_t = jnp.transpose(_mm_hi(h, rw_ref[...]))[:N_EXPERTS]
    e0, e1, g0, g1 = _route(logits_t, rb_ref[...])
    lo, hi = jnp.minimum(e0, e1), jnp.maximum(e0, e1)
    grp = jnp.floor(lo * (1.0 / EXPERTS_PER_GROUP))
    pa, pb = lo - grp * EXPERTS_PER_GROUP, hi - grp * EXPERTS_PER_GROUP
    cls = grp * PAIRS_PER_GROUP + pa * ((2 * EXPERTS_PER_GROUP - 1) - pa) * 0.5 + (pb - pa - 1.0)
    rec = jnp.concatenate([e0, e1, g0, g1, cls, jnp.zeros((LANES - 5, TOKEN_TILE), F32)], axis=0)
    rec_t = jnp.transpose(rec)
    lane = lax.broadcasted_iota(jnp.int32, (1, LANES), 1)
    onehot = jnp.where(rec_t[:, REC_CLS:REC_CLS + 1] == lane.astype(F32), 1.0, 0.0)
    incl = jnp.dot(tri_ref[...], onehot.astype(MXU_DTYPE), preferred_element_type=F32)

    @pl.when(pl.program_id(0) == 0)
    def _():
        run_ref[...] = jnp.zeros_like(run_ref)

    run = run_ref[...]
    rank = jnp.sum(onehot * (incl + run), axis=-1, keepdims=True) - 1.0
    run = run + incl[TOKEN_TILE - 1:TOKEN_TILE, :]
    run_ref[...] = run
    cnt_ref[...] = jnp.broadcast_to(run, cnt_ref.shape)
    hx_ref[:, D:] = jnp.where(lane == REC_RANK, rank, rec_t)


def _outproj(lay, layer, n_tok, x, a_lat, a_ctx, b, r, w_out, mods3, nw, rw_pad, rb_col, tri):
    D, TM = lay.D, TOKEN_TILE
    row = lambda i: (i, 0)
    const = lambda i: (0, 0)
    tile = lambda w: pl.BlockSpec((TM, w), row)
    n_lat = lay.n_lat_tiles
    in_specs = [tile(D), pl.BlockSpec((TM, DIFF_WIDTH), lambda i: (jnp.minimum(i, n_lat - 1), 0))]
    args = [x, a_lat]
    if a_ctx is not None:
        in_specs.append(pl.BlockSpec((TM, DIFF_WIDTH), lambda i: (jnp.maximum(i - n_lat, 0), 0)))
        args.append(a_ctx)
    in_specs += [tile(POOL_WIDTH), tile(RET_WIDTH),
                 pl.BlockSpec((DIFF_WIDTH + POOL_WIDTH + RET_WIDTH, D), const),
                 lay.mod_spec(layer, 2), pl.BlockSpec((1, D), const), lay.mod_spec(layer, 3), lay.mod_spec(layer, 4),
                 pl.BlockSpec((D, LANES), const), pl.BlockSpec((N_EXPERTS, 1), const), pl.BlockSpec((TM, TM), const)]
    args += [b, r, w_out, mods3, nw.reshape(1, D), mods3, mods3, rw_pad, rb_col, tri]
    return pl.pallas_call(
        functools.partial(_outproj_body, lay=lay, with_ctx=a_ctx is not None),
        grid=(n_tok // TM,), in_specs=in_specs,
        out_specs=[tile(D), tile(D + LANES), pl.BlockSpec((8, LANES), const)],
        out_shape=[jax.ShapeDtypeStruct((n_tok, D), F32), jax.ShapeDtypeStruct((n_tok, D + LANES), F32),
                   jax.ShapeDtypeStruct((8, LANES), F32)],
        scratch_shapes=[pltpu.VMEM((1, LANES), F32)],
        compiler_params=_cparams("arbitrary"), name="outproj_router",
    )(*args)


def _row_copies(idx_ref, hbm, vmem, sem, scatter):
    n = vmem.shape[0]
    def copy(u, row):
        src, dst = (vmem.at[pl.ds(u, 1)], hbm.at[pl.ds(row, 1)])
        return pltpu.make_async_copy(src, dst, sem) if scatter else pltpu.make_async_copy(dst, src, sem)

    def start(u, carry):
        copy(u, idx_ref[0, 0, u]).start()
        return carry

    def wait(u, carry):
        copy(0, 0).wait()
        return carry

    lax.fori_loop(0, n, start, 0, unroll=8)
    lax.fori_loop(0, n, wait, 0, unroll=8)


def _dispatch_body(pend_ref, padded_ref, nvalid_ref, pos_ref, hx_ref, xs_hbm, zbuf, zsem, sem, *, n_blk):
    TB = MOE_TILE
    i = pl.program_id(0)

    @pl.when(i == 0)
    def _():
        zbuf[...] = jnp.zeros_like(zbuf)
        fill = lambda row: pltpu.make_async_copy(zbuf, xs_hbm.at[pl.ds(row, TB)], zsem.at[0])
        for k in range(N_CLASSES):
            @pl.when(padded_ref[k] > 0)
            def _():
                fill(pl.multiple_of(pend_ref[k] - TB, TB)).start()
        for j in range(n_blk):
            @pl.when(j >= nvalid_ref[0])
            def _():
                fill(j * TB).start()
        for k in range(N_CLASSES):
            @pl.when(padded_ref[k] > 0)
            def _():
                fill(0).wait()
        for j in range(n_blk):
            @pl.when(j >= nvalid_ref[0])
            def _():
                fill(0).wait()

    _row_copies(pos_ref, xs_hbm, hx_ref, sem.at[0], True)


def _combine_body(pos_ref, ys_hbm, f_ref, sem):
    _row_copies(pos_ref, ys_hbm, f_ref, sem.at[0], False)


def _moe_body(elo_ref, ehi_ref, nvalid_ref, xs_ref, wg_lo, wu_lo, wd_lo, wg_hi, wu_hi, wd_hi, ys_ref, *, D):
    j = pl.program_id(0)

    @pl.when(j < nvalid_ref[0])
    def _():
        xb = xs_ref[:, :D].astype(MXU_DTYPE)
        info = xs_ref[:, D:]
        first_is_lo = info[:, REC_E0:REC_E0 + 1] <= info[:, REC_E1:REC_E1 + 1]
        g0, g1 = info[:, REC_G0:REC_G0 + 1], info[:, REC_G1:REC_G1 + 1]
        g_lo = jnp.where(first_is_lo, g0, g1)
        g_hi = jnp.where(first_is_lo, g1, g0)

        def ffn(wg, wu, wd):
            hg = jnp.dot(xb, wg[...], preferred_element_type=F32)
            hu = jnp.dot(xb, wu[...], preferred_element_type=F32)
            return _mm((hg * _sigmoid(hg)) * hu, wd[...])

        ys_ref[...] = g_lo * ffn(wg_lo, wu_lo, wd_lo) + g_hi * ffn(wg_hi, wu_hi, wd_hi)

    @pl.when(j >= nvalid_ref[0])
    def _():
        ys_ref[...] = jnp.zeros_like(ys_ref)


def _moe(n_tok, D, hx, counts, wg, wu, wd):
    TB = MOE_TILE
    DE = wg.shape[-1]
    chunk = math.gcd(n_tok, PERM_CHUNK)
    cls = hx[:, D + REC_CLS].astype(jnp.int32)
    rank = hx[:, D + REC_RANK].astype(jnp.int32)
    cnt = counts[0, :N_CLASSES].astype(jnp.int32)
    padded = (cnt + TB - 1) // TB * TB
    pend = jnp.cumsum(padded)
    pstart = pend - padded
    onehot = cls[:, None] == jnp.arange(N_CLASSES, dtype=jnp.int32)[None, :]
    pos = jnp.sum(jnp.where(onehot, pstart[None, :], 0), axis=1) + rank
    n_blk = -(-n_tok // TB) + N_CLASSES
    P = n_blk * TB
    nvalid = (pend[-1] // TB).astype(jnp.int32).reshape(1)
    blk = jnp.minimum(jnp.arange(n_blk, dtype=jnp.int32), nvalid - 1)
    blk_cls = jnp.minimum(jnp.sum(pend[None, :] <= (blk * TB)[:, None], axis=1), N_CLASSES - 1).astype(jnp.int32)
    pairs = [(u, v) for u in range(EXPERTS_PER_GROUP) for v in range(u + 1, EXPERTS_PER_GROUP)]
    pair_lo = jnp.array([u for u, _ in pairs], jnp.int32)
    pair_hi = jnp.array([v for _, v in pairs], jnp.int32)
    base = (blk_cls // PAIRS_PER_GROUP) * EXPERTS_PER_GROUP
    blk_lo = base + pair_lo[blk_cls % PAIRS_PER_GROUP]
    blk_hi = base + pair_hi[blk_cls % PAIRS_PER_GROUP]
    pos3 = pos.reshape(n_tok // chunk, 1, chunk)
    pos_spec = pl.BlockSpec((1, 1, chunk), lambda i, *_: (i, 0, 0), memory_space=pltpu.SMEM)
    any_spec = pl.BlockSpec(memory_space=pl.ANY)

    xs = pl.pallas_call(
        functools.partial(_dispatch_body, n_blk=n_blk),
        grid_spec=pltpu.PrefetchScalarGridSpec(
            num_scalar_prefetch=3, grid=(n_tok // chunk,),
            in_specs=[pos_spec, pl.BlockSpec((chunk, D + LANES), lambda i, *_: (i, 0))], out_specs=any_spec,
            scratch_shapes=[pltpu.VMEM((TB, D + LANES), F32), pltpu.SemaphoreType.DMA((1,)),
                            pltpu.SemaphoreType.DMA((1,))]),
        out_shape=jax.ShapeDtypeStruct((P, D + LANES), F32),
        compiler_params=_cparams("arbitrary"), name="moe_dispatch",
    )(pend.astype(jnp.int32), padded.astype(jnp.int32), nvalid, pos3, hx)

    w_spec = lambda shape, which: pl.BlockSpec(
        (None,) + shape, (lambda j, elo, ehi, nv: (elo[j], 0, 0)) if which == 0 else (lambda j, elo, ehi, nv: (ehi[j], 0, 0)))
    ys = pl.pallas_call(
        functools.partial(_moe_body, D=D),
        grid_spec=pltpu.PrefetchScalarGridSpec(
            num_scalar_prefetch=3, grid=(n_blk,),
            in_specs=[pl.BlockSpec((TB, D + LANES), lambda j, elo, ehi, nv: (jnp.minimum(j, nv[0] - 1), 0)),
                      w_spec((D, DE), 0), w_spec((D, DE), 0), w_spec((DE, D), 0),
                      w_spec((D, DE), 1), w_spec((D, DE), 1), w_spec((DE, D), 1)],
            out_specs=pl.BlockSpec((TB, D), lambda j, elo, ehi, nv: (j, 0))),
        out_shape=jax.ShapeDtypeStruct((P, D), F32),
        compiler_params=_cparams("arbitrary"), name="moe_experts",
    )(blk_lo, blk_hi, nvalid, xs, wg, wu, wd, wg, wu, wd)

    return pl.pallas_call(
        _combine_body,
        grid_spec=pltpu.PrefetchScalarGridSpec(
            num_scalar_prefetch=0, grid=(n_tok // chunk,), in_specs=[pos_spec, any_spec],
            out_specs=pl.BlockSpec((chunk, D), lambda i: (i, 0)),
            scratch_shapes=[pltpu.SemaphoreType.DMA((1,))]),
        out_shape=jax.ShapeDtypeStruct((n_tok, D), F32),
        compiler_params=_cparams("arbitrary"), name="moe_combine",
    )(pos3, ys)


def _final_body(x_ref, f_ref, g2_ref, nw_ref, o_ref):
    o_ref[...] = _rms(x_ref[...] + g2_ref[0] * f_ref[...]) * nw_ref[...]


def _final(lay, layer, x, f, mods3, nw):
    D, TM = lay.D, TOKEN_TILE
    row = lambda i: (i, 0)
    return pl.pallas_call(
        _final_body, grid=(lay.NL // TM,),
        in_specs=[pl.BlockSpec((TM, D), row), pl.BlockSpec((TM, D), row), lay.mod_spec(layer, 5),
                  pl.BlockSpec((1, D), lambda i: (0, 0))],
        out_specs=pl.BlockSpec((TM, D), row),
        out_shape=jax.ShapeDtypeStruct((lay.NL, D), F32),
        compiler_params=_cparams("arbitrary"), name="final_norm",
    )(x, f, mods3, nw.reshape(1, D))


def kernel(x, c, ctx, c_ctx, w_in, w_out, ada_w, ada_b, norm1_w, norm2_w, diff_lambda, diff_norm_w, pool_w,
           pool_scale, ret_a_f, ret_a_b, ret_norm_w, router_w, router_b, moe_w_gate, moe_w_up, moe_w_down,
           final_norm_w):
    B, L, D = x.shape
    Lc = ctx.shape[1]
    depth = w_in.shape[0]
    lay = _Layout(B, L, Lc, D)
    assert B + 1 <= MOD_ROWS

    xa = jnp.concatenate([x.reshape(B * L, D), ctx.reshape(B * Lc, D)], axis=0)
    c_all = jnp.zeros((MOD_ROWS, D), F32).at[:B].set(c).at[B].set(c_ctx)
    mods3 = _ada_table(c_all, ada_w, ada_b).reshape(depth * MOD_ROWS * N_MOD, 1, D)
    tables = _rope_tables(lay)
    rw_pad = jnp.zeros((D, LANES), F32).at[:, :N_EXPERTS].set(router_w.astype(F32))
    rb_col = router_b.astype(F32).reshape(N_EXPERTS, 1)
    tri = jnp.tril(jnp.ones((TOKEN_TILE, TOKEN_TILE), F32)).astype(MXU_DTYPE)
    w_in_b, w_out_b = w_in.astype(MXU_DTYPE), w_out.astype(MXU_DTYPE)
    wg_b, wu_b, wd_b = moe_w_gate.astype(MXU_DTYPE), moe_w_up.astype(MXU_DTYPE), moe_w_down.astype(MXU_DTYPE)

    f = None
    for l in range(depth):
        last = l == depth - 1
        lam_init = 0.8 - 0.6 * math.exp(-0.3 * l)
        dl = diff_lambda[l].astype(F32)
        lam = (jnp.exp(jnp.sum(dl[0] * dl[1])) - jnp.exp(jnp.sum(dl[2] * dl[3])) + lam_init).reshape(1)
        p, xa = _inproj(lay, l, xa, f, mods3, norm1_w[l], w_in_b[l], tables)
        a_lat, a_ctx = _diff_attention(lay, p, lam, diff_norm_w[l], 1.0 - lam_init, not last)
        eye = jnp.eye(len(POOL_WINDOWS), dtype=F32)
        w_bd = (eye[:, None, :, None] * pool_w[l][:, :, None, :]).reshape(POOL_WIDTH, POOL_WIDTH).astype(MXU_DTYPE)
        bp = _pool(lay, p, w_bd, pool_scale[l])
        lg_f = -jnp.exp(ret_a_f[l].astype(F32))
        lg_b = -jnp.exp(ret_a_b[l].astype(F32))
        lanes = lambda lg: jnp.repeat(lg, HEAD_DIM).reshape(1, RET_WIDTH)
        yf = _retention_pass(lay, p, lg_f, lanes(lg_f), False)
        r = _retention_pass(lay, p, lg_b, lanes(lg_b), True, (yf, ret_norm_w[l]))
        n_tok = lay.NL if last else lay.T
        xa, hx, counts = _outproj(lay, l, n_tok, xa, a_lat, a_ctx, bp, r, w_out_b[l], mods3, norm2_w[l],
                                  rw_pad, rb_col, tri)
        f = _moe(n_tok, D, hx, counts, wg_b[l], wu_b[l], wd_b[l])
    out = _final(lay, depth - 1, xa, f, mods3, final_norm_w)
    return out.reshape(B, L, D)
```

```python
import functools
import math

import jax
import jax.numpy as jnp
from jax import lax
from jax.experimental import pallas as pl
from jax.experimental.pallas import tpu as pltpu

F32 = jnp.float32
MXU_DTYPE = jnp.bfloat16
ACT_DTYPE = jnp.bfloat16

GRID_W = 64
HEAD_DIM = 64
DIFF_HEADS = 4
DIFF_V_DIM = 2 * HEAD_DIM
DIFF_WIDTH = DIFF_HEADS * DIFF_V_DIM
POOL_WINDOWS = (2, 4, 8, 16)
POOL_GROUP = 64
POOL_WIDTH = POOL_GROUP * len(POOL_WINDOWS)
RET_HEADS = 4
RET_WIDTH = RET_HEADS * HEAD_DIM
OFF_DQ = 0
OFF_DK = OFF_DQ + DIFF_WIDTH
OFF_DV = OFF_DK + DIFF_WIDTH
OFF_PU = OFF_DV + DIFF_WIDTH
OFF_RQ = OFF_PU + POOL_WIDTH
OFF_RK = OFF_RQ + RET_WIDTH
OFF_RV = OFF_RK + RET_WIDTH
OFF_RG = OFF_RV + RET_WIDTH
IN_WIDTH = OFF_RG + RET_WIDTH
ROPE_BASE = 10000.0
ROPE_FREQS = HEAD_DIM // 4
N_EXPERTS = 16
N_GROUPS = 4
EXPERTS_PER_GROUP = N_EXPERTS // N_GROUPS
PAIRS_PER_GROUP = EXPERTS_PER_GROUP * (EXPERTS_PER_GROUP - 1) // 2
N_CLASSES = N_GROUPS * PAIRS_PER_GROUP
N_MOD = 6
EPS = 1e-6

LANES = 128
MOD_ROWS = 16
TOKEN_TILE = 512
SEQ_TILE = 256
ATTN_SUBTILES = 2
ATTN_SUB_ROWS = 256
ATTN_KEY_CHUNK = 512
HALO = 16
MOE_TILE = 256
COL = 256
ROUTER_BLOCK = 2048
PERM_CHUNK = 2048
REC_E0, REC_E1, REC_G0, REC_G1, REC_CLS, REC_RANK = range(6)
VMEM_LIMIT = 56 * 1024 * 1024


def _mm(a, b):
    return jnp.dot(a.astype(MXU_DTYPE), b.astype(MXU_DTYPE), preferred_element_type=F32)


def _mm_nt(a, b):
    return lax.dot_general(a.astype(MXU_DTYPE), b.astype(MXU_DTYPE), (((1,), (1,)), ((), ())),
                           preferred_element_type=F32)


def _mm_tn(a, b):
    return lax.dot_general(a.astype(MXU_DTYPE), b.astype(MXU_DTYPE), (((0,), (0,)), ((), ())),
                           preferred_element_type=F32)


def _split(x):
    hi = x.astype(MXU_DTYPE)
    lo = (x - hi.astype(F32)).astype(MXU_DTYPE)
    return hi, lo


def _mm_hi(a, b):
    ah, al = _split(a)
    bh, bl = _split(b)
    d = lambda u, v: jnp.dot(u, v, preferred_element_type=F32)
    return d(ah, bh) + (d(ah, bl) + d(al, bh))


def _sigmoid(x):
    return 1.0 / (1.0 + jnp.exp(-x))


def _rms(x):
    return x * lax.rsqrt(jnp.mean(x * x, axis=-1, keepdims=True) + EPS)


def _cparams(*sem):
    return pltpu.CompilerParams(dimension_semantics=sem, vmem_limit_bytes=VMEM_LIMIT)


def _ada_body(c_ref, w_ref, b_ref, o_ref):
    c = c_ref[...]
    o_ref[...] = _mm_hi(c * _sigmoid(c), w_ref[...]) + b_ref[...]


def _ada_table(c_all, ada_w, ada_b):
    depth, d, n = ada_w.shape
    nb = n // N_MOD
    return pl.pallas_call(
        _ada_body,
        grid=(depth, n // nb),
        in_specs=[pl.BlockSpec((MOD_ROWS, d), lambda l, j: (0, 0)),
                  pl.BlockSpec((None, d, nb), lambda l, j: (l, 0, j)),
                  pl.BlockSpec((None, 1, nb), lambda l, j: (l, 0, j))],
        out_specs=pl.BlockSpec((None, MOD_ROWS, nb), lambda l, j: (l, 0, j)),
        out_shape=jax.ShapeDtypeStruct((depth, MOD_ROWS, n), F32),
        compiler_params=_cparams("arbitrary", "arbitrary"),
        name="ada_table",
    )(c_all, ada_w, ada_b.reshape(depth, 1, n))


class _Layout:
    def __init__(self, B, L, Lc, D):
        self.B, self.L, self.Lc, self.D = B, L, Lc, D
        self.NL, self.NC = B * L, B * Lc
        self.T = self.NL + self.NC
        assert L % TOKEN_TILE == 0 and self.NC % TOKEN_TILE == 0
        assert L % SEQ_TILE == 0 and Lc == SEQ_TILE and L % GRID_W == 0
        self.n_lat_tiles = self.NL // TOKEN_TILE
        self.tiles_per_seq = L // TOKEN_TILE

    def mod_row(self, i):
        return jnp.where(i < self.n_lat_tiles, i // self.tiles_per_seq, self.B)

    def rope_block(self, i):
        return jnp.where(i < self.n_lat_tiles, i % self.tiles_per_seq, self.tiles_per_seq)

    def mod_spec(self, layer, k):
        base = layer * MOD_ROWS * N_MOD + k
        return pl.BlockSpec((1, 1, self.D), lambda i: (base + self.mod_row(i) * N_MOD, 0, 0))


def _rope_tables(lay):
    L = lay.L
    rows = L // GRID_W
    row = jnp.repeat(jnp.arange(rows), GRID_W).astype(F32)
    col = jnp.tile(jnp.arange(GRID_W), rows).astype(F32)
    inv = ROPE_BASE ** (-jnp.arange(ROPE_FREQS, dtype=F32) / ROPE_FREQS)
    ang_r = row[:, None] * inv
    ang_c = col[:, None] * inv
    ang = jnp.concatenate([ang_r, ang_r, ang_c, ang_c], axis=-1)
    cos, sin = jnp.cos(ang), jnp.sin(ang)
    first_half = (jnp.arange(HEAD_DIM) % (2 * ROPE_FREQS)) < ROPE_FREQS
    sin_a = jnp.where(first_half, -sin, 0.0)
    sin_b = jnp.where(first_half, 0.0, sin)
    rep = LANES // HEAD_DIM
    ident = [jnp.ones((TOKEN_TILE, LANES), F32), jnp.zeros((TOKEN_TILE, LANES), F32),
             jnp.zeros((TOKEN_TILE, LANES), F32)]
    return [jnp.concatenate([jnp.tile(t, (1, rep)), e], axis=0) for t, e in zip((cos, sin_a, sin_b), ident)]


_ROPE_COLS = tuple(range(OFF_DQ // COL, OFF_DV // COL)) + (OFF_RQ // COL, OFF_RK // COL)
LOG2E = 1.4426950408889634
_COL_SCALE = {j: HEAD_DIM ** -0.5 * LOG2E for j in range(OFF_DQ // COL, OFF_DK // COL)}
_COL_SCALE[OFF_RK // COL] = HEAD_DIM ** -0.5


def _inproj_body(*refs, with_f):
    if with_f:
        (x_ref, f_ref, g2_ref, nw_ref, sh_ref, sc_ref, w_ref, cos_ref, sa_ref, sb_ref, p_ref, xo_ref) = refs
        x = x_ref[...] + g2_ref[0] * f_ref[...]
        xo_ref[...] = x
    else:
        (x_ref, nw_ref, sh_ref, sc_ref, w_ref, cos_ref, sa_ref, sb_ref, p_ref) = refs
        x = x_ref[...]
    h = (_rms(x) * nw_ref[...]) * (1.0 + sc_ref[0]) + sh_ref[0]
    hb = h.astype(MXU_DTYPE)
    cos, sa, sb = cos_ref[...], sa_ref[...], sb_ref[...]
    for j in range(IN_WIDTH // COL):
        acc = jnp.dot(hb, w_ref[:, j * COL:(j + 1) * COL], preferred_element_type=F32)
        if j in _ROPE_COLS:
            parts = []
            for t in range(COL // LANES):
                a = acc[:, t * LANES:(t + 1) * LANES]
                parts.append(a * cos + pltpu.roll(a, LANES - ROPE_FREQS, 1) * sa + pltpu.roll(a, ROPE_FREQS, 1) * sb)
            acc = jnp.concatenate(parts, axis=1)
        if j in _COL_SCALE:
            acc = acc * _COL_SCALE[j]
        p_ref[:, j * COL:(j + 1) * COL] = acc.astype(p_ref.dtype)


def _inproj(lay, layer, x, f, mods3, nw, w_in, tables):
    D, TM = lay.D, TOKEN_TILE
    row = lambda i: (i, 0)
    const = lambda i: (0, 0)
    tbl = pl.BlockSpec((TM, LANES), lambda i: (lay.rope_block(i), 0))
    in_specs = [pl.BlockSpec((TM, D), row)]
    args = [x]
    if f is not None:
        in_specs += [pl.BlockSpec((TM, D), row), lay.mod_spec(layer - 1, 5)]
        args += [f, mods3]
    in_specs += [pl.BlockSpec((1, D), const), lay.mod_spec(layer, 0), lay.mod_spec(layer, 1),
                 pl.BlockSpec((D, IN_WIDTH), const), tbl, tbl, tbl]
    args += [nw.reshape(1, D), mods3, mods3, w_in] + list(tables)
    out_specs = [pl.BlockSpec((TM, IN_WIDTH), row)]
    out_shape = [jax.ShapeDtypeStruct((lay.T, IN_WIDTH), ACT_DTYPE)]
    if f is not None:
        out_specs.append(pl.BlockSpec((TM, D), row))
        out_shape.append(jax.ShapeDtypeStruct((lay.T, D), F32))
    outs = pl.pallas_call(
        functools.partial(_inproj_body, with_f=f is not None),
        grid=(lay.T // TM,), in_specs=in_specs, out_specs=out_specs, out_shape=out_shape,
        compiler_params=_cparams("arbitrary"), name="inproj",
    )(*args)
    return (outs[0], outs[1]) if f is not None else (outs[0], x)


def _map_masks(q):
    lane = lax.broadcasted_iota(jnp.int32, (1, DIFF_V_DIM), 1)
    zero = jnp.zeros_like(q)
    return [jnp.where(lane < HEAD_DIM, q, zero), jnp.where(lane >= HEAD_DIM, q, zero)]


def _scores(qm, k_refs):
    chunks, run = [], None
    for k in k_refs:
        for c0 in range(0, k.shape[0], ATTN_KEY_CHUNK):
            s = _mm_nt(qm, k[c0:c0 + ATTN_KEY_CHUNK, :])
            chunks.append(s)
            for t in range(s.shape[1] // LANES):
                piece = s[:, t * LANES:(t + 1) * LANES]
                run = piece if run is None else jnp.maximum(run, piece)
    return chunks, jnp.max(run, axis=-1, keepdims=True)


def _softmax_diff(maps, lam):
    es, ls = [], []
    for chunks, mx in maps:
        e = [jnp.exp2(t - mx) for t in chunks]
        ls.append(functools.reduce(jnp.add, [jnp.sum(t, axis=-1, keepdims=True) for t in e]))
        es.append([t.astype(MXU_DTYPE) for t in e])
    c = (lam * ls[0] / ls[1]).astype(MXU_DTYPE)
    return [e1 - e2 * c for e1, e2 in zip(es[0], es[1])], 1.0 / ls[0]


def _head_norm(o, w, post_scale):
    return (_rms(o) * w) * post_scale


def _attn_body(lam_ref, q_ref, *refs, n_seg, n_sub, post_scale):
    k_refs, v_refs = refs[:n_seg], refs[n_seg:2 * n_seg]
    w_ref, o_ref = refs[2 * n_seg], refs[2 * n_seg + 1]
    rows = q_ref.shape[0] // n_sub

    def finish(u, maps):
        p, r1 = _softmax_diff(maps, lam_ref[0])
        o, at = None, 0
        for v in v_refs:
            n_c = -(-v.shape[0] // ATTN_KEY_CHUNK)
            part = jnp.dot(jnp.concatenate(p[at:at + n_c], axis=1), v[...], preferred_element_type=F32)
            o = part if o is None else o + part
            at += n_c
        o_ref[u * rows:(u + 1) * rows, :] = _head_norm(o * r1, w_ref[...], post_scale).astype(o_ref.dtype)

    scores = [[_scores(qm, k_refs) for qm in _map_masks(q_ref[u * rows:(u + 1) * rows, :])] for u in range(n_sub)]
    for u, maps in enumerate(scores):
        finish(u, maps)


def _diff_attention(lay, p, lam, norm_w, post_scale, need_ctx):
    B, L, Lc, H = lay.B, lay.L, lay.Lc, DIFF_HEADS
    W = DIFF_V_DIM
    tq = ATTN_SUBTILES * ATTN_SUB_ROWS
    assert L % tq == 0
    nq = L // tq
    k_blk, v_blk = OFF_DK // W, OFF_DV // W
    ctx0 = lay.NL // Lc
    lam_spec = pl.BlockSpec(memory_space=pltpu.SMEM)
    w_spec = pl.BlockSpec((1, W), lambda *_: (0, 0))
    a_lat = pl.pallas_call(
        functools.partial(_attn_body, n_seg=2, n_sub=ATTN_SUBTILES, post_scale=post_scale),
        grid=(B, H, nq),
        in_specs=[lam_spec,
                  pl.BlockSpec((tq, W), lambda b, h, i: (b * nq + i, h)),
                  pl.BlockSpec((Lc, W), lambda b, h, i: (ctx0 + b, k_blk + h)),
                  pl.BlockSpec((L, W), lambda b, h, i: (b, k_blk + h)),
                  pl.BlockSpec((Lc, W), lambda b, h, i: (ctx0 + b, v_blk + h)),
                  pl.BlockSpec((L, W), lambda b, h, i: (b, v_blk + h)),
                  w_spec],
        out_specs=pl.BlockSpec((tq, W), lambda b, h, i: (b * nq + i, h)),
        out_shape=jax.ShapeDtypeStruct((lay.NL, DIFF_WIDTH), ACT_DTYPE),
        compiler_params=_cparams("arbitrary", "arbitrary", "arbitrary"),
        name="diff_attn",
    )(lam, p, p, p, p, p, norm_w.reshape(1, W))
    if not need_ctx:
        return a_lat, None
    a_ctx = pl.pallas_call(
        functools.partial(_attn_body, n_seg=1, n_sub=1, post_scale=post_scale),
        grid=(B, H),
        in_specs=[lam_spec,
                  pl.BlockSpec((Lc, W), lambda b, h: (ctx0 + b, h)),
                  pl.BlockSpec((Lc, W), lambda b, h: (ctx0 + b, k_blk + h)),
                  pl.BlockSpec((Lc, W), lambda b, h: (ctx0 + b, v_blk + h)),
                  w_spec],
        out_specs=pl.BlockSpec((Lc, W), lambda b, h: (b, h)),
        out_shape=jax.ShapeDtypeStruct((lay.NC, DIFF_WIDTH), ACT_DTYPE),
        compiler_params=_cparams("arbitrary", "arbitrary"),
        name="diff_attn_ctx",
    )(lam, p, p, p, norm_w.reshape(1, W))
    return a_lat, a_ctx


def _pool_body(prev_ref, cur_ref, next_ref, w_ref, scale_ref, o_ref, *, lay):
    i = pl.program_id(0)
    n_lat = lay.NL // SEQ_TILE
    per_seq = jnp.where(i < n_lat, lay.L // SEQ_TILE, lay.Lc // SEQ_TILE)
    idx = jnp.where(i < n_lat, i, i - n_lat) % per_seq
    seq_len = per_seq * SEQ_TILE
    has_prev = (idx > 0).astype(F32)
    has_next = (idx < per_seq - 1).astype(F32)
    cur = cur_ref[...].astype(F32)
    u = jnp.concatenate([prev_ref[...].astype(F32) * has_prev, cur, next_ref[...].astype(F32) * has_next], axis=0)
    sums = {1: u}
    w = 1
    while w < POOL_WINDOWS[-1]:
        s = sums[w]
        m = s.shape[0] - w
        sums[2 * w] = s[:m] + s[w:w + m]
        w *= 2
    pos = idx * SEQ_TILE + lax.broadcasted_iota(jnp.int32, (SEQ_TILE, 1), 0)
    group = lax.broadcasted_iota(jnp.int32, (1, POOL_WIDTH), 1) // POOL_GROUP
    mean = jnp.zeros((SEQ_TILE, POOL_WIDTH), F32)
    for g, win in enumerate(POOL_WINDOWS):
        start = HALO - win // 2
        cnt = jnp.minimum(pos + (win - win // 2), seq_len) - jnp.maximum(pos - win // 2, 0)
        mean = jnp.where(group == g, sums[win][start:start + SEQ_TILE] / cnt.astype(F32), mean)
    o_ref[...] = (_mm(mean - cur, w_ref[...]) * scale_ref[...]).astype(o_ref.dtype)


def _pool(lay, p, w_bd, scale):
    n = lay.T // SEQ_TILE
    col = OFF_PU // POOL_WIDTH
    per = SEQ_TILE // HALO
    last = lay.T // HALO - 1
    return pl.pallas_call(
        functools.partial(_pool_body, lay=lay),
        grid=(n,),
        in_specs=[pl.BlockSpec((HALO, POOL_WIDTH), lambda i: (jnp.maximum(i * per - 1, 0), col)),
                  pl.BlockSpec((SEQ_TILE, POOL_WIDTH), lambda i: (i, col)),
                  pl.BlockSpec((HALO, POOL_WIDTH), lambda i: (jnp.minimum((i + 1) * per, last), col)),
                  pl.BlockSpec((POOL_WIDTH, POOL_WIDTH), lambda i: (0, 0)),
                  pl.BlockSpec((1, POOL_WIDTH), lambda i: (0, 0))],
        out_specs=pl.BlockSpec((SEQ_TILE, POOL_WIDTH), lambda i: (i, 0)),
        out_shape=jax.ShapeDtypeStruct((lay.T, POOL_WIDTH), ACT_DTYPE),
        compiler_params=_cparams("arbitrary"), name="pool",
    )(p, p, p, w_bd, scale.reshape(1, POOL_WIDTH))


def _ret_body(lgs_ref, lgv_ref, q_ref, k_ref, v_ref, *refs, reverse):
    C = SEQ_TILE
    if reverse:
        g_ref, yf_ref, nw_ref, o_ref, s_ref, d_ref, xi_ref, zeta_ref = refs
    else:
        o_ref, s_ref, d_ref, xi_ref, zeta_ref = refs
    b, j = pl.program_id(0), pl.program_id(1)
    lane_head = lax.broadcasted_iota(jnp.int32, (1, RET_WIDTH), 1) // HEAD_DIM
    lgv = lgv_ref[...]

    @pl.when((b == 0) & (j == 0))
    def _():
        ri = lax.broadcasted_iota(jnp.int32, (C, C), 0)
        ci = lax.broadcasted_iota(jnp.int32, (C, C), 1)
        dist = (ci - ri if reverse else ri - ci).astype(F32)
        keep = dist > 0 if reverse else dist >= 0
        for h in range(RET_HEADS):
            d_ref[h * C:(h + 1) * C, :] = jnp.where(keep, jnp.exp(lgs_ref[h] * jnp.maximum(dist, 0.0)), 0.0)
        t = lax.broadcasted_iota(jnp.int32, (C, 1), 0).astype(F32)
        xi_ref[...] = jnp.exp(lgv * ((C - t) if reverse else (t + 1.0)))
        zeta_ref[...] = jnp.exp(lgv * (t if reverse else (C - 1.0 - t)))

    @pl.when(j == 0)
    def _():
        s_ref[...] = jnp.zeros_like(s_ref)

    q, k, v = q_ref[...], k_ref[...], v_ref[...]
    zero = jnp.zeros_like(q)
    qs = jnp.concatenate([jnp.where(lane_head == h, q, zero) for h in range(RET_HEADS)], axis=0)
    sd = _mm_nt(qs, k) * d_ref[...]
    yv = _mm(sd, v)
    y = functools.reduce(jnp.add, [jnp.where(lane_head == h, yv[h * C:(h + 1) * C], 0.0) for h in range(RET_HEADS)])
    state = s_ref[...]
    y = y + _mm(q.astype(F32) * xi_ref[...], state)
    kv = _mm_tn(k.astype(F32) * zeta_ref[...], v)
    row_head = lax.broadcasted_iota(jnp.int32, (RET_WIDTH, 1), 0) // HEAD_DIM
    s_ref[...] = state * jnp.exp(lgv * C) + jnp.where(row_head == lane_head, kv, 0.0)

    if reverse:
        y = y + yf_ref[...]
        same = (row_head == lane_head).astype(F32) * (1.0 / HEAD_DIM)
        ms = _mm_hi(y * y, same)
        yn = y * lax.rsqrt(ms + EPS) * nw_ref[...]
        g = g_ref[...].astype(F32)
        o_ref[...] = ((g * _sigmoid(g)) * yn).astype(o_ref.dtype)
    else:
        o_ref[...] = y


def _retention_pass(lay, p, lgs, lgv, reverse, extra=()):
    B, L = lay.B, lay.L
    C, W = SEQ_TILE, RET_WIDTH
    nch = L // C
    ctx0 = lay.NL // C

    def rows(b, j):
        lat = b * nch + (nch - j if reverse else j - 1)
        return jnp.where(j == 0, ctx0 + b, lat)

    col = lambda c: pl.BlockSpec((C, W), lambda b, j: (rows(b, j), c))
    in_specs = [pl.BlockSpec(memory_space=pltpu.SMEM), pl.BlockSpec((1, W), lambda b, j: (0, 0)),
                col(OFF_RQ // W), col(OFF_RK // W), col(OFF_RV // W)]
    args = [lgs, lgv, p, p, p]
    if reverse:
        yf, nw = extra
        in_specs += [col(OFF_RG // W), col(0), pl.BlockSpec((1, W), lambda b, j: (0, 0))]
        args += [p, yf, nw.reshape(1, W)]
    return pl.pallas_call(
        functools.partial(_ret_body, reverse=reverse),
        grid=(B, nch + 1), in_specs=in_specs, out_specs=col(0),
        out_shape=jax.ShapeDtypeStruct((lay.T, W), ACT_DTYPE if reverse else F32),
        scratch_shapes=[pltpu.VMEM((W, W), F32), pltpu.VMEM((RET_HEADS * C, C), F32),
                        pltpu.VMEM((C, W), F32), pltpu.VMEM((C, W), F32)],
        compiler_params=_cparams("arbitrary", "arbitrary"),
        name="retention_bwd" if reverse else "retention_fwd",
    )(*args)


def _top2(vals):
    n = len(vals)
    v1 = functools.reduce(jnp.maximum, vals)
    i1 = jnp.full_like(v1, n - 1)
    for e in range(n - 2, -1, -1):
        i1 = jnp.where(vals[e] == v1, float(e), i1)
    rest = [jnp.where(i1 == float(e), -jnp.inf, vals[e]) for e in range(n)]
    v2 = functools.reduce(jnp.maximum, rest)
    i2 = jnp.full_like(v1, n - 1)
    for e in range(n - 2, -1, -1):
        i2 = jnp.where(rest[e] == v2, float(e), i2)
    return v1, i1, v2, i2


def _route(logits_t, bias):
    s = _sigmoid(logits_t)
    sel = s + bias
    groups = []
    for g in range(N_GROUPS):
        rows = [sel[g * EXPERTS_PER_GROUP + e:g * EXPERTS_PER_GROUP + e + 1] for e in range(EXPERTS_PER_GROUP)]
        groups.append(_top2(rows))
    score = [v1 + v2 for v1, _, v2, _ in groups]
    best = functools.reduce(jnp.maximum, score)
    gi = jnp.full_like(best, N_GROUPS - 1)
    for g in range(N_GROUPS - 2, -1, -1):
        gi = jnp.where(score[g] == best, float(g), gi)
    pick = lambda k: functools.reduce(
        lambda acc, g: jnp.where(gi == float(g), groups[g][k], acc), range(N_GROUPS - 1), groups[N_GROUPS - 1][k])
    e0 = gi * EXPERTS_PER_GROUP + pick(1)
    e1 = gi * EXPERTS_PER_GROUP + pick(3)
    s0 = jnp.zeros_like(best)
    s1 = jnp.zeros_like(best)
    for e in range(N_EXPERTS):
        s0 = jnp.where(e0 == float(e), s[e:e + 1], s0)
        s1 = jnp.where(e1 == float(e), s[e:e + 1], s1)
    tot = s0 + s1
    return e0, e1, s0 / tot, s1 / tot


def _outproj_body(*refs, lay, with_ctx):
    if with_ctx:
        (x_ref, al_ref, ac_ref, b_ref, r_ref, w_ref, g1_ref, nw_ref, sh_ref, sc_ref, rw_ref, xo_ref, hx_ref, lt_ref) = refs
        a = jnp.where(pl.program_id(0) < lay.n_lat_tiles, al_ref[...], ac_ref[...])
    else:
        (x_ref, al_ref, b_ref, r_ref, w_ref, g1_ref, nw_ref, sh_ref, sc_ref, rw_ref, xo_ref, hx_ref, lt_ref) = refs
        a = al_ref[...]
    D = lay.D
    y = (_mm(a, w_ref[:DIFF_WIDTH, :])
         + _mm(b_ref[...], w_ref[DIFF_WIDTH:DIFF_WIDTH + POOL_WIDTH, :])
         + _mm(r_ref[...], w_ref[DIFF_WIDTH + POOL_WIDTH:, :]))
    x = x_ref[...] + g1_ref[0] * y
    xo_ref[...] = x
    h = (_rms(x) * nw_ref[...]) * (1.0 + sc_ref[0]) + sh_ref[0]
    hx_ref[:, :D] = h
    hx_ref[:, D:] = jnp.zeros((TOKEN_TILE, LANES), F32)
    lt_ref[...] = jnp.transpose(_mm(h, rw_ref[...]))[:N_EXPERTS]


def _outproj(lay, layer, n_tok, x, a_lat, a_ctx, b, r, w_out, mods3, nw, rw_pad):
    D, TM = lay.D, TOKEN_TILE
    row = lambda i: (i, 0)
    const = lambda i: (0, 0)
    tile = lambda w: pl.BlockSpec((TM, w), row)
    n_lat = lay.n_lat_tiles
    in_specs = [tile(D), pl.BlockSpec((TM, DIFF_WIDTH), lambda i: (jnp.minimum(i, n_lat - 1), 0))]
    args = [x, a_lat]
    if a_ctx is not None:
        in_specs.append(pl.BlockSpec((TM, DIFF_WIDTH), lambda i: (jnp.maximum(i - n_lat, 0), 0)))
        args.append(a_ctx)
    in_specs += [tile(POOL_WIDTH), tile(RET_WIDTH),
                 pl.BlockSpec((DIFF_WIDTH + POOL_WIDTH + RET_WIDTH, D), const),
                 lay.mod_spec(layer, 2), pl.BlockSpec((1, D), const), lay.mod_spec(layer, 3), lay.mod_spec(layer, 4),
                 pl.BlockSpec((D, LANES), const)]
    args += [b, r, w_out, mods3, nw.reshape(1, D), mods3, mods3, rw_pad]
    return pl.pallas_call(
        functools.partial(_outproj_body, lay=lay, with_ctx=a_ctx is not None),
        grid=(n_tok // TM,), in_specs=in_specs,
        out_specs=[tile(D), tile(D + LANES), pl.BlockSpec((N_EXPERTS, TM), lambda i: (0, i))],
        out_shape=[jax.ShapeDtypeStruct((n_tok, D), F32), jax.ShapeDtypeStruct((n_tok, D + LANES), F32),
                   jax.ShapeDtypeStruct((N_EXPERTS, n_tok), F32)],
        compiler_params=_cparams("arbitrary"), name="outproj",
    )(*args)


def _router_body(lt_ref, rb_ref, tri_ref, hx_any, rec_ref, cnt_ref, run_ref):
    del hx_any
    n = lt_ref.shape[1]
    e0, e1, g0, g1 = _route(lt_ref[...], rb_ref[...])
    lo, hi = jnp.minimum(e0, e1), jnp.maximum(e0, e1)
    grp = jnp.floor(lo * (1.0 / EXPERTS_PER_GROUP))
    pa, pb = lo - grp * EXPERTS_PER_GROUP, hi - grp * EXPERTS_PER_GROUP
    cls = grp * PAIRS_PER_GROUP + pa * ((2 * EXPERTS_PER_GROUP - 1) - pa) * 0.5 + (pb - pa - 1.0)
    rec_t = jnp.transpose(jnp.concatenate([e0, e1, g0, g1, cls, jnp.zeros((LANES - 5, n), F32)], axis=0))
    lane = lax.broadcasted_iota(jnp.int32, (1, LANES), 1)

    @pl.when(pl.program_id(0) == 0)
    def _():
        run_ref[...] = jnp.zeros_like(run_ref)

    run = run_ref[...]
    tri = tri_ref[...]
    for c0 in range(0, n, TOKEN_TILE):
        rows = rec_t[c0:c0 + TOKEN_TILE]
        onehot = jnp.where(rows[:, REC_CLS:REC_CLS + 1] == lane.astype(F32), 1.0, 0.0)
        incl = jnp.dot(tri, onehot.astype(MXU_DTYPE), preferred_element_type=F32)
        rank = jnp.sum(onehot * (incl + run), axis=-1, keepdims=True) - 1.0
        run = run + incl[TOKEN_TILE - 1:TOKEN_TILE, :]
        rec_ref[c0:c0 + TOKEN_TILE, :] = jnp.where(lane == REC_RANK, rank, rows)
    run_ref[...] = run
    cnt_ref[...] = jnp.broadcast_to(run, cnt_ref.shape)


def _router(n_tok, D, hx, logits_t, rb_col, tri):
    blk = math.gcd(n_tok, ROUTER_BLOCK)
    return pl.pallas_call(
        _router_body,
        grid=(n_tok // blk,),
        in_specs=[pl.BlockSpec((N_EXPERTS, blk), lambda i: (0, i)), pl.BlockSpec((N_EXPERTS, 1), lambda i: (0, 0)),
                  pl.BlockSpec((TOKEN_TILE, TOKEN_TILE), lambda i: (0, 0)), pl.BlockSpec(memory_space=pl.ANY)],
        out_specs=[pl.BlockSpec((blk, LANES), lambda i: (i, D // LANES)), pl.BlockSpec((8, LANES), lambda i: (0, 0))],
        out_shape=[jax.ShapeDtypeStruct((n_tok, D + LANES), F32), jax.ShapeDtypeStruct((8, LANES), F32)],
        scratch_shapes=[pltpu.VMEM((1, LANES), F32)],
        input_output_aliases={3: 0},
        compiler_params=_cparams("arbitrary"), name="router",
    )(logits_t, rb_col, tri, hx)


def _row_copies(idx_ref, hbm, vmem, sem, scatter):
    n = vmem.shape[0]
    def copy(u, row):
        src, dst = (vmem.at[pl.ds(u, 1)], hbm.at[pl.ds(row, 1)])
        return pltpu.make_async_copy(src, dst, sem) if scatter else pltpu.make_async_copy(dst, src, sem)

    def start(u, carry):
        copy(u, idx_ref[0, 0, u]).start()
        return carry

    def wait(u, carry):
        copy(0, 0).wait()
        return carry

    lax.fori_loop(0, n, start, 0, unroll=8)
    lax.fori_loop(0, n, wait, 0, unroll=8)


def _dispatch_body(pend_ref, padded_ref, nvalid_ref, pos_ref, hx_ref, xs_hbm, zbuf, zsem, sem, *, n_blk):
    TB = MOE_TILE
    i = pl.program_id(0)

    @pl.when(i == 0)
    def _():
        zbuf[...] = jnp.zeros_like(zbuf)
        fill = lambda row: pltpu.make_async_copy(zbuf, xs_hbm.at[pl.ds(row, TB)], zsem.at[0])
        for k in range(N_CLASSES):
            @pl.when(padded_ref[k] > 0)
            def _():
                fill(pl.multiple_of(pend_ref[k] - TB, TB)).start()
        for j in range(n_blk):
            @pl.when(j >= nvalid_ref[0])
            def _():
                fill(j * TB).start()
        for k in range(N_CLASSES):
            @pl.when(padded_ref[k] > 0)
            def _():
                fill(0).wait()
        for j in range(n_blk):
            @pl.when(j >= nvalid_ref[0])
            def _():
                fill(0).wait()

    _row_copies(pos_ref, xs_hbm, hx_ref, sem.at[0], True)


def _combine_body(pos_ref, ys_hbm, f_ref, sem):
    _row_copies(pos_ref, ys_hbm, f_ref, sem.at[0], False)


def _moe_body(elo_ref, ehi_ref, nvalid_ref, xs_ref, wg_lo, wu_lo, wd_lo, wg_hi, wu_hi, wd_hi, ys_ref, *, D):
    j = pl.program_id(0)

    @pl.when(j < nvalid_ref[0])
    def _():
        xb = xs_ref[:, :D].astype(MXU_DTYPE)
        info = xs_ref[:, D:]
        first_is_lo = info[:, REC_E0:REC_E0 + 1] <= info[:, REC_E1:REC_E1 + 1]
        g0, g1 = info[:, REC_G0:REC_G0 + 1], info[:, REC_G1:REC_G1 + 1]
        g_lo = jnp.where(first_is_lo, g0, g1)
        g_hi = jnp.where(first_is_lo, g1, g0)

        def ffn(wg, wu, wd):
            hg = jnp.dot(xb, wg[...], preferred_element_type=F32)
            hu = jnp.dot(xb, wu[...], preferred_element_type=F32)
            return _mm((hg * _sigmoid(hg)) * hu, wd[...])

        ys_ref[...] = g_lo * ffn(wg_lo, wu_lo, wd_lo) + g_hi * ffn(wg_hi, wu_hi, wd_hi)

    @pl.when(j >= nvalid_ref[0])
    def _():
        ys_ref[...] = jnp.zeros_like(ys_ref)


def _moe(n_tok, D, hx, counts, wg, wu, wd):
    TB = MOE_TILE
    DE = wg.shape[-1]
    chunk = math.gcd(n_tok, PERM_CHUNK)
    cls = hx[:, D + REC_CLS].astype(jnp.int32)
    rank = hx[:, D + REC_RANK].astype(jnp.int32)
    cnt = counts[0, :N_CLASSES].astype(jnp.int32)
    padded = (cnt + TB - 1) // TB * TB
    pend = jnp.cumsum(padded)
    pstart = pend - padded
    onehot = cls[:, None] == jnp.arange(N_CLASSES, dtype=jnp.int32)[None, :]
    pos = jnp.sum(jnp.where(onehot, pstart[None, :], 0), axis=1) + rank
    n_blk = -(-n_tok // TB) + N_CLASSES
    P = n_blk * TB
    nvalid = (pend[-1] // TB).astype(jnp.int32).reshape(1)
    blk = jnp.minimum(jnp.arange(n_blk, dtype=jnp.int32), nvalid - 1)
    blk_cls = jnp.minimum(jnp.sum(pend[None, :] <= (blk * TB)[:, None], axis=1), N_CLASSES - 1).astype(jnp.int32)
    pairs = [(u, v) for u in range(EXPERTS_PER_GROUP) for v in range(u + 1, EXPERTS_PER_GROUP)]
    pair_lo = jnp.array([u for u, _ in pairs], jnp.int32)
    pair_hi = jnp.array([v for _, v in pairs], jnp.int32)
    base = (blk_cls // PAIRS_PER_GROUP) * EXPERTS_PER_GROUP
    blk_lo = base + pair_lo[blk_cls % PAIRS_PER_GROUP]
    blk_hi = base + pair_hi[blk_cls % PAIRS_PER_GROUP]
    pos3 = pos.reshape(n_tok // chunk, 1, chunk)
    pos_spec = pl.BlockSpec((1, 1, chunk), lambda i, *_: (i, 0, 0), memory_space=pltpu.SMEM)
    any_spec = pl.BlockSpec(memory_space=pl.ANY)

    xs = pl.pallas_call(
        functools.partial(_dispatch_body, n_blk=n_blk),
        grid_spec=pltpu.PrefetchScalarGridSpec(
            num_scalar_prefetch=3, grid=(n_tok // chunk,),
            in_specs=[pos_spec, pl.BlockSpec((chunk, D + LANES), lambda i, *_: (i, 0))], out_specs=any_spec,
            scratch_shapes=[pltpu.VMEM((TB, D + LANES), F32), pltpu.SemaphoreType.DMA((1,)),
                            pltpu.SemaphoreType.DMA((1,))]),
        out_shape=jax.ShapeDtypeStruct((P, D + LANES), F32),
        compiler_params=_cparams("arbitrary"), name="moe_dispatch",
    )(pend.astype(jnp.int32), padded.astype(jnp.int32), nvalid, pos3, hx)

    w_spec = lambda shape, which: pl.BlockSpec(
        (None,) + shape, (lambda j, elo, ehi, nv: (elo[j], 0, 0)) if which == 0 else (lambda j, elo, ehi, nv: (ehi[j], 0, 0)))
    ys = pl.pallas_call(
        functools.partial(_moe_body, D=D),
        grid_spec=pltpu.PrefetchScalarGridSpec(
            num_scalar_prefetch=3, grid=(n_blk,),
            in_specs=[pl.BlockSpec((TB, D + LANES), lambda j, elo, ehi, nv: (jnp.minimum(j, nv[0] - 1), 0)),
                      w_spec((D, DE), 0), w_spec((D, DE), 0), w_spec((DE, D), 0),
                      w_spec((D, DE), 1), w_spec((D, DE), 1), w_spec((DE, D), 1)],
            out_specs=pl.BlockSpec((TB, D), lambda j, elo, ehi, nv: (j, 0))),
        out_shape=jax.ShapeDtypeStruct((P, D), F32),
        compiler_params=_cparams("arbitrary"), name="moe_experts",
    )(blk_lo, blk_hi, nvalid, xs, wg, wu, wd, wg, wu, wd)

    return pl.pallas_call(
        _combine_body,
        grid_spec=pltpu.PrefetchScalarGridSpec(
            num_scalar_prefetch=0, grid=(n_tok // chunk,), in_specs=[pos_spec, any_spec],
            out_specs=pl.BlockSpec((chunk, D), lambda i: (i, 0)),
            scratch_shapes=[pltpu.SemaphoreType.DMA((1,))]),
        out_shape=jax.ShapeDtypeStruct((n_tok, D), F32),
        compiler_params=_cparams("arbitrary"), name="moe_combine",
    )(pos3, ys)


def _final_body(x_ref, f_ref, g2_ref, nw_ref, o_ref):
    o_ref[...] = _rms(x_ref[...] + g2_ref[0] * f_ref[...]) * nw_ref[...]


def _final(lay, layer, x, f, mods3, nw):
    D, TM = lay.D, TOKEN_TILE
    row = lambda i: (i, 0)
    return pl.pallas_call(
        _final_body, grid=(lay.NL // TM,),
        in_specs=[pl.BlockSpec((TM, D), row), pl.BlockSpec((TM, D), row), lay.mod_spec(layer, 5),
                  pl.BlockSpec((1, D), lambda i: (0, 0))],
        out_specs=pl.BlockSpec((TM, D), row),
        out_shape=jax.ShapeDtypeStruct((lay.NL, D), F32),
        compiler_params=_cparams("arbitrary"), name="final_norm",
    )(x, f, mods3, nw.reshape(1, D))


def kernel(x, c, ctx, c_ctx, w_in, w_out, ada_w, ada_b, norm1_w, norm2_w, diff_lambda, diff_norm_w, pool_w,
           pool_scale, ret_a_f, ret_a_b, ret_norm_w, router_w, router_b, moe_w_gate, moe_w_up, moe_w_down,
           final_norm_w):
    B, L, D = x.shape
    Lc = ctx.shape[1]
    depth = w_in.shape[0]
    lay = _Layout(B, L, Lc, D)
    assert B + 1 <= MOD_ROWS

    xa = jnp.concatenate([x.reshape(B * L, D), ctx.reshape(B * Lc, D)], axis=0)
    c_all = jnp.zeros((MOD_ROWS, D), F32).at[:B].set(c).at[B].set(c_ctx)
    mods3 = _ada_table(c_all, ada_w, ada_b).reshape(depth * MOD_ROWS * N_MOD, 1, D)
    tables = _rope_tables(lay)
    rw_pad = jnp.zeros((D, LANES), MXU_DTYPE).at[:, :N_EXPERTS].set(router_w.astype(MXU_DTYPE))
    rb_col = router_b.astype(F32).reshape(N_EXPERTS, 1)
    tri = jnp.tril(jnp.ones((TOKEN_TILE, TOKEN_TILE), F32)).astype(MXU_DTYPE)
    w_in_b, w_out_b = w_in.astype(MXU_DTYPE), w_out.astype(MXU_DTYPE)
    wg_b, wu_b, wd_b = moe_w_gate.astype(MXU_DTYPE), moe_w_up.astype(MXU_DTYPE), moe_w_down.astype(MXU_DTYPE)

    f = None
    for l in range(depth):
        last = l == depth - 1
        lam_init = 0.8 - 0.6 * math.exp(-0.3 * l)
        dl = diff_lambda[l].astype(F32)
        lam = (jnp.exp(jnp.sum(dl[0] * dl[1])) - jnp.exp(jnp.sum(dl[2] * dl[3])) + lam_init).reshape(1)
        p, xa = _inproj(lay, l, xa, f, mods3, norm1_w[l], w_in_b[l], tables)
        a_lat, a_ctx = _diff_attention(lay, p, lam, diff_norm_w[l], 1.0 - lam_init, not last)
        eye = jnp.eye(len(POOL_WINDOWS), dtype=F32)
        w_bd = (eye[:, None, :, None] * pool_w[l][:, :, None, :]).reshape(POOL_WIDTH, POOL_WIDTH).astype(MXU_DTYPE)
        bp = _pool(lay, p, w_bd, pool_scale[l])
        lg_f = -jnp.exp(ret_a_f[l].astype(F32))
        lg_b = -jnp.exp(ret_a_b[l].astype(F32))
        lanes = lambda lg: jnp.repeat(lg, HEAD_DIM).reshape(1, RET_WIDTH)
        yf = _retention_pass(lay, p, lg_f, lanes(lg_f), False)
        r = _retention_pass(lay, p, lg_b, lanes(lg_b), True, (yf, ret_norm_w[l]))
        n_tok = lay.NL if last else lay.T
        xa, hx, logits_t = _outproj(lay, l, n_tok, xa, a_lat, a_ctx, bp, r, w_out_b[l], mods3, norm2_w[l], rw_pad)
        hx, counts = _router(n_tok, D, hx, logits_t, rb_col, tri)
        f = _moe(n_tok, D, hx, counts, wg_b[l], wu_b[l], wd_b[l])
    out = _final(lay, depth - 1, xa, f, mods3, final_norm_w)
    return out.reshape(B, L, D)
```

```python
import functools
import math

import jax
import jax.numpy as jnp
from jax import lax
from jax.experimental import pallas as pl
from jax.experimental.pallas import tpu as pltpu

F32 = jnp.float32
MXU_DTYPE = jnp.bfloat16
ACT_DTYPE = jnp.bfloat16

GRID_W = 64
HEAD_DIM = 64
DIFF_HEADS = 4
DIFF_V_DIM = 2 * HEAD_DIM
DIFF_WIDTH = DIFF_HEADS * DIFF_V_DIM
POOL_WINDOWS = (2, 4, 8, 16)
POOL_GROUP = 64
POOL_WIDTH = POOL_GROUP * len(POOL_WINDOWS)
RET_HEADS = 4
RET_WIDTH = RET_HEADS * HEAD_DIM
OFF_DQ = 0
OFF_DK = OFF_DQ + DIFF_WIDTH
OFF_DV = OFF_DK + DIFF_WIDTH
OFF_PU = OFF_DV + DIFF_WIDTH
OFF_RQ = OFF_PU + POOL_WIDTH
OFF_RK = OFF_RQ + RET_WIDTH
OFF_RV = OFF_RK + RET_WIDTH
OFF_RG = OFF_RV + RET_WIDTH
IN_WIDTH = OFF_RG + RET_WIDTH
ROPE_BASE = 10000.0
ROPE_FREQS = HEAD_DIM // 4
N_EXPERTS = 16
N_GROUPS = 4
EXPERTS_PER_GROUP = N_EXPERTS // N_GROUPS
PAIRS_PER_GROUP = EXPERTS_PER_GROUP * (EXPERTS_PER_GROUP - 1) // 2
N_CLASSES = N_GROUPS * PAIRS_PER_GROUP
N_MOD = 6
EPS = 1e-6

LANES = 128
MOD_ROWS = 16
TOKEN_TILE = 512
SEQ_TILE = 256
ATTN_SUBTILES = 2
ATTN_SUB_ROWS = 256
ATTN_KEY_CHUNK = 512
HALO = 16
MOE_TILE = 256
COL = 256
ROUTER_BLOCK = 2048
CLASS_ROWS = 32
PERM_CHUNK = 2048
REC_E0, REC_E1, REC_G0, REC_G1, REC_CLS, REC_RANK = range(6)
VMEM_LIMIT = 56 * 1024 * 1024


def _mm(a, b):
    return jnp.dot(a.astype(MXU_DTYPE), b.astype(MXU_DTYPE), preferred_element_type=F32)


def _mm_nt(a, b):
    return lax.dot_general(a.astype(MXU_DTYPE), b.astype(MXU_DTYPE), (((1,), (1,)), ((), ())),
                           preferred_element_type=F32)


def _mm_tn(a, b):
    return lax.dot_general(a.astype(MXU_DTYPE), b.astype(MXU_DTYPE), (((0,), (0,)), ((), ())),
                           preferred_element_type=F32)


def _split(x):
    hi = x.astype(MXU_DTYPE)
    lo = (x - hi.astype(F32)).astype(MXU_DTYPE)
    return hi, lo


def _mm_hi(a, b):
    ah, al = _split(a)
    bh, bl = _split(b)
    d = lambda u, v: jnp.dot(u, v, preferred_element_type=F32)
    return d(ah, bh) + (d(ah, bl) + d(al, bh))


def _sigmoid(x):
    return 1.0 / (1.0 + jnp.exp(-x))


def _rms(x):
    return x * lax.rsqrt(jnp.mean(x * x, axis=-1, keepdims=True) + EPS)


def _cparams(*sem):
    return pltpu.CompilerParams(dimension_semantics=sem, vmem_limit_bytes=VMEM_LIMIT)


def _ada_body(c_ref, w_ref, b_ref, o_ref):
    c = c_ref[...]
    o_ref[...] = _mm_hi(c * _sigmoid(c), w_ref[...]) + b_ref[...]


def _ada_table(c_all, ada_w, ada_b):
    depth, d, n = ada_w.shape
    nb = n // N_MOD
    return pl.pallas_call(
        _ada_body,
        grid=(depth, n // nb),
        in_specs=[pl.BlockSpec((MOD_ROWS, d), lambda l, j: (0, 0)),
                  pl.BlockSpec((None, d, nb), lambda l, j: (l, 0, j)),
                  pl.BlockSpec((None, 1, nb), lambda l, j: (l, 0, j))],
        out_specs=pl.BlockSpec((None, MOD_ROWS, nb), lambda l, j: (l, 0, j)),
        out_shape=jax.ShapeDtypeStruct((depth, MOD_ROWS, n), F32),
        compiler_params=_cparams("arbitrary", "arbitrary"),
        name="ada_table",
    )(c_all, ada_w, ada_b.reshape(depth, 1, n))


class _Layout:
    def __init__(self, B, L, Lc, D):
        self.B, self.L, self.Lc, self.D = B, L, Lc, D
        self.NL, self.NC = B * L, B * Lc
        self.T = self.NL + self.NC
        assert L % TOKEN_TILE == 0 and self.NC % TOKEN_TILE == 0
        assert L % SEQ_TILE == 0 and Lc == SEQ_TILE and L % GRID_W == 0
        self.n_lat_tiles = self.NL // TOKEN_TILE
        self.tiles_per_seq = L // TOKEN_TILE

    def mod_row(self, i):
        return jnp.where(i < self.n_lat_tiles, i // self.tiles_per_seq, self.B)

    def rope_block(self, i):
        return jnp.where(i < self.n_lat_tiles, i % self.tiles_per_seq, self.tiles_per_seq)

    def mod_spec(self, layer, k):
        base = layer * MOD_ROWS * N_MOD + k
        return pl.BlockSpec((1, 1, self.D), lambda i: (base + self.mod_row(i) * N_MOD, 0, 0))


def _rope_tables(lay):
    L = lay.L
    rows = L // GRID_W
    row = jnp.repeat(jnp.arange(rows), GRID_W).astype(F32)
    col = jnp.tile(jnp.arange(GRID_W), rows).astype(F32)
    inv = ROPE_BASE ** (-jnp.arange(ROPE_FREQS, dtype=F32) / ROPE_FREQS)
    ang_r = row[:, None] * inv
    ang_c = col[:, None] * inv
    ang = jnp.concatenate([ang_r, ang_r, ang_c, ang_c], axis=-1)
    cos, sin = jnp.cos(ang), jnp.sin(ang)
    first_half = (jnp.arange(HEAD_DIM) % (2 * ROPE_FREQS)) < ROPE_FREQS
    sin_a = jnp.where(first_half, -sin, 0.0)
    sin_b = jnp.where(first_half, 0.0, sin)
    rep = LANES // HEAD_DIM
    ident = [jnp.ones((TOKEN_TILE, LANES), F32), jnp.zeros((TOKEN_TILE, LANES), F32),
             jnp.zeros((TOKEN_TILE, LANES), F32)]
    return [jnp.concatenate([jnp.tile(t, (1, rep)), e], axis=0) for t, e in zip((cos, sin_a, sin_b), ident)]


_ROPE_COLS = tuple(range(OFF_DQ // COL, OFF_DV // COL)) + (OFF_RQ // COL, OFF_RK // COL)
LOG2E = 1.4426950408889634
_COL_SCALE = {j: HEAD_DIM ** -0.5 * LOG2E for j in range(OFF_DQ // COL, OFF_DK // COL)}
_COL_SCALE[OFF_RK // COL] = HEAD_DIM ** -0.5


def _inproj_body(*refs, with_f, n_lat_tiles):
    if with_f:
        (x_ref, f_ref, g2_ref, nw_ref, sh_ref, sc_ref, w_ref, cos_ref, sa_ref, sb_ref, p_ref, xo_ref) = refs
        x = x_ref[...] + g2_ref[0] * f_ref[...]
    else:
        (xl_ref, xc_ref, nw_ref, sh_ref, sc_ref, w_ref, cos_ref, sa_ref, sb_ref, p_ref, xo_ref) = refs
        x = jnp.where(pl.program_id(0) < n_lat_tiles, xl_ref[...], xc_ref[...])
    xo_ref[...] = x
    h = (_rms(x) * nw_ref[...]) * (1.0 + sc_ref[0]) + sh_ref[0]
    hb = h.astype(MXU_DTYPE)
    cos, sa, sb = cos_ref[...], sa_ref[...], sb_ref[...]
    for j in range(IN_WIDTH // COL):
        acc = jnp.dot(hb, w_ref[:, j * COL:(j + 1) * COL], preferred_element_type=F32)
        if j in _ROPE_COLS:
            parts = []
            for t in range(COL // LANES):
                a = acc[:, t * LANES:(t + 1) * LANES]
                parts.append(a * cos + pltpu.roll(a, LANES - ROPE_FREQS, 1) * sa + pltpu.roll(a, ROPE_FREQS, 1) * sb)
            acc = jnp.concatenate(parts, axis=1)
        if j in _COL_SCALE:
            acc = acc * _COL_SCALE[j]
        p_ref[:, j * COL:(j + 1) * COL] = acc.astype(p_ref.dtype)


def _inproj(lay, layer, x, f, mods3, nw, w_in, tables):
    D, TM = lay.D, TOKEN_TILE
    row = lambda i: (i, 0)
    const = lambda i: (0, 0)
    n_lat = lay.n_lat_tiles
    tbl = pl.BlockSpec((TM, LANES), lambda i: (lay.rope_block(i), 0))
    if f is not None:
        in_specs = [pl.BlockSpec((TM, D), row), pl.BlockSpec((TM, D), row), lay.mod_spec(layer - 1, 5)]
        args = [x, f, mods3]
    else:
        in_specs = [pl.BlockSpec((TM, D), lambda i: (jnp.minimum(i, n_lat - 1), 0)),
                    pl.BlockSpec((TM, D), lambda i: (jnp.maximum(i - n_lat, 0), 0))]
        args = list(x)
    in_specs += [pl.BlockSpec((1, D), const), lay.mod_spec(layer, 0), lay.mod_spec(layer, 1),
                 pl.BlockSpec((D, IN_WIDTH), const), tbl, tbl, tbl]
    args += [nw.reshape(1, D), mods3, mods3, w_in] + list(tables)
    return pl.pallas_call(
        functools.partial(_inproj_body, with_f=f is not None, n_lat_tiles=n_lat),
        grid=(lay.T // TM,), in_specs=in_specs,
        out_specs=[pl.BlockSpec((TM, IN_WIDTH), row), pl.BlockSpec((TM, D), row)],
        out_shape=[jax.ShapeDtypeStruct((lay.T, IN_WIDTH), ACT_DTYPE), jax.ShapeDtypeStruct((lay.T, D), F32)],
        compiler_params=_cparams("arbitrary"), name="inproj",
    )(*args)


def _map_masks(q):
    lane = lax.broadcasted_iota(jnp.int32, (1, DIFF_V_DIM), 1)
    zero = jnp.zeros_like(q)
    return [jnp.where(lane < HEAD_DIM, q, zero), jnp.where(lane >= HEAD_DIM, q, zero)]


def _scores(qm, k_refs):
    chunks, run = [], None
    for k in k_refs:
        for c0 in range(0, k.shape[0], ATTN_KEY_CHUNK):
            s = _mm_nt(qm, k[c0:c0 + ATTN_KEY_CHUNK, :])
            chunks.append(s)
            for t in range(s.shape[1] // LANES):
                piece = s[:, t * LANES:(t + 1) * LANES]
                run = piece if run is None else jnp.maximum(run, piece)
    return chunks, jnp.max(run, axis=-1, keepdims=True)


def _softmax_diff(maps, lam):
    es, ls = [], []
    for chunks, mx in maps:
        e = [jnp.exp2(t - mx) for t in chunks]
        ls.append(functools.reduce(jnp.add, [jnp.sum(t, axis=-1, keepdims=True) for t in e]))
        es.append([t.astype(MXU_DTYPE) for t in e])
    c = (lam * ls[0] / ls[1]).astype(MXU_DTYPE)
    return [e1 - e2 * c for e1, e2 in zip(es[0], es[1])], 1.0 / ls[0]


def _head_norm(o, w, post_scale):
    return (_rms(o) * w) * post_scale


def _attn_body(lam_ref, q_ref, *refs, n_seg, n_sub, post_scale):
    k_refs, v_refs = refs[:n_seg], refs[n_seg:2 * n_seg]
    w_ref, o_ref = refs[2 * n_seg], refs[2 * n_seg + 1]
    rows = q_ref.shape[0] // n_sub

    def finish(u, maps):
        p, r1 = _softmax_diff(maps, lam_ref[0])
        o, at = None, 0
        for v in v_refs:
            n_c = -(-v.shape[0] // ATTN_KEY_CHUNK)
            part = jnp.dot(jnp.concatenate(p[at:at + n_c], axis=1), v[...], preferred_element_type=F32)
            o = part if o is None else o + part
            at += n_c
        o_ref[u * rows:(u + 1) * rows, :] = _head_norm(o * r1, w_ref[...], post_scale).astype(o_ref.dtype)

    scores = [[_scores(qm, k_refs) for qm in _map_masks(q_ref[u * rows:(u + 1) * rows, :])] for u in range(n_sub)]
    for u, maps in enumerate(scores):
        finish(u, maps)


def _diff_attention(lay, p, lam, norm_w, post_scale, need_ctx):
    B, L, Lc, H = lay.B, lay.L, lay.Lc, DIFF_HEADS
    W = DIFF_V_DIM
    tq = ATTN_SUBTILES * ATTN_SUB_ROWS
    assert L % tq == 0
    nq = L // tq
    k_blk, v_blk = OFF_DK // W, OFF_DV // W
    ctx0 = lay.NL // Lc
    lam_spec = pl.BlockSpec(memory_space=pltpu.SMEM)
    w_spec = pl.BlockSpec((1, W), lambda *_: (0, 0))
    a_lat = pl.pallas_call(
        functools.partial(_attn_body, n_seg=2, n_sub=ATTN_SUBTILES, post_scale=post_scale),
        grid=(B, H, nq),
        in_specs=[lam_spec,
                  pl.BlockSpec((tq, W), lambda b, h, i: (b * nq + i, h)),
                  pl.BlockSpec((Lc, W), lambda b, h, i: (ctx0 + b, k_blk + h)),
                  pl.BlockSpec((L, W), lambda b, h, i: (b, k_blk + h)),
                  pl.BlockSpec((Lc, W), lambda b, h, i: (ctx0 + b, v_blk + h)),
                  pl.BlockSpec((L, W), lambda b, h, i: (b, v_blk + h)),
                  w_spec],
        out_specs=pl.BlockSpec((tq, W), lambda b, h, i: (b * nq + i, h)),
        out_shape=jax.ShapeDtypeStruct((lay.NL, DIFF_WIDTH), ACT_DTYPE),
        compiler_params=_cparams("arbitrary", "arbitrary", "arbitrary"),
        name="diff_attn",
    )(lam, p, p, p, p, p, norm_w.reshape(1, W))
    if not need_ctx:
        return a_lat, None
    a_ctx = pl.pallas_call(
        functools.partial(_attn_body, n_seg=1, n_sub=1, post_scale=post_scale),
        grid=(B, H),
        in_specs=[lam_spec,
                  pl.BlockSpec((Lc, W), lambda b, h: (ctx0 + b, h)),
                  pl.BlockSpec((Lc, W), lambda b, h: (ctx0 + b, k_blk + h)),
                  pl.BlockSpec((Lc, W), lambda b, h: (ctx0 + b, v_blk + h)),
                  w_spec],
        out_specs=pl.BlockSpec((Lc, W), lambda b, h: (b, h)),
        out_shape=jax.ShapeDtypeStruct((lay.NC, DIFF_WIDTH), ACT_DTYPE),
        compiler_params=_cparams("arbitrary", "arbitrary"),
        name="diff_attn_ctx",
    )(lam, p, p, p, norm_w.reshape(1, W))
    return a_lat, a_ctx


def _pool_body(prev_ref, cur_ref, next_ref, w_ref, scale_ref, o_ref, *, lay):
    i = pl.program_id(0)
    n_lat = lay.NL // SEQ_TILE
    per_seq = jnp.where(i < n_lat, lay.L // SEQ_TILE, lay.Lc // SEQ_TILE)
    idx = jnp.where(i < n_lat, i, i - n_lat) % per_seq
    seq_len = per_seq * SEQ_TILE
    has_prev = (idx > 0).astype(F32)
    has_next = (idx < per_seq - 1).astype(F32)
    cur = cur_ref[...].astype(F32)
    u = jnp.concatenate([prev_ref[...].astype(F32) * has_prev, cur, next_ref[...].astype(F32) * has_next], axis=0)
    sums = {1: u}
    w = 1
    while w < POOL_WINDOWS[-1]:
        s = sums[w]
        m = s.shape[0] - w
        sums[2 * w] = s[:m] + s[w:w + m]
        w *= 2
    pos = idx * SEQ_TILE + lax.broadcasted_iota(jnp.int32, (SEQ_TILE, 1), 0)
    group = lax.broadcasted_iota(jnp.int32, (1, POOL_WIDTH), 1) // POOL_GROUP
    mean = jnp.zeros((SEQ_TILE, POOL_WIDTH), F32)
    for g, win in enumerate(POOL_WINDOWS):
        start = HALO - win // 2
        cnt = jnp.minimum(pos + (win - win // 2), seq_len) - jnp.maximum(pos - win // 2, 0)
        mean = jnp.where(group == g, sums[win][start:start + SEQ_TILE] / cnt.astype(F32), mean)
    o_ref[...] = (_mm(mean - cur, w_ref[...]) * scale_ref[...]).astype(o_ref.dtype)


def _pool(lay, p, w_bd, scale):
    n = lay.T // SEQ_TILE
    col = OFF_PU // POOL_WIDTH
    per = SEQ_TILE // HALO
    last = lay.T // HALO - 1
    return pl.pallas_call(
        functools.partial(_pool_body, lay=lay),
        grid=(n,),
        in_specs=[pl.BlockSpec((HALO, POOL_WIDTH), lambda i: (jnp.maximum(i * per - 1, 0), col)),
                  pl.BlockSpec((SEQ_TILE, POOL_WIDTH), lambda i: (i, col)),
                  pl.BlockSpec((HALO, POOL_WIDTH), lambda i: (jnp.minimum((i + 1) * per, last), col)),
                  pl.BlockSpec((POOL_WIDTH, POOL_WIDTH), lambda i: (0, 0)),
                  pl.BlockSpec((1, POOL_WIDTH), lambda i: (0, 0))],
        out_specs=pl.BlockSpec((SEQ_TILE, POOL_WIDTH), lambda i: (i, 0)),
        out_shape=jax.ShapeDtypeStruct((lay.T, POOL_WIDTH), ACT_DTYPE),
        compiler_params=_cparams("arbitrary"), name="pool",
    )(p, p, p, w_bd, scale.reshape(1, POOL_WIDTH))


def _ret_body(lgs_ref, lgv_ref, q_ref, k_ref, v_ref, *refs, reverse):
    C = SEQ_TILE
    if reverse:
        g_ref, yf_ref, nw_ref, o_ref, s_ref, d_ref, xi_ref, zeta_ref = refs
    else:
        o_ref, s_ref, d_ref, xi_ref, zeta_ref = refs
    b, j = pl.program_id(0), pl.program_id(1)
    lane_head = lax.broadcasted_iota(jnp.int32, (1, RET_WIDTH), 1) // HEAD_DIM
    lgv = lgv_ref[...]

    @pl.when((b == 0) & (j == 0))
    def _():
        ri = lax.broadcasted_iota(jnp.int32, (C, C), 0)
        ci = lax.broadcasted_iota(jnp.int32, (C, C), 1)
        dist = (ci - ri if reverse else ri - ci).astype(F32)
        keep = dist > 0 if reverse else dist >= 0
        for h in range(RET_HEADS):
            d_ref[h * C:(h + 1) * C, :] = jnp.where(keep, jnp.exp(lgs_ref[h] * jnp.maximum(dist, 0.0)), 0.0)
        t = lax.broadcasted_iota(jnp.int32, (C, 1), 0).astype(F32)
        xi_ref[...] = jnp.exp(lgv * ((C - t) if reverse else (t + 1.0)))
        zeta_ref[...] = jnp.exp(lgv * (t if reverse else (C - 1.0 - t)))

    @pl.when(j == 0)
    def _():
        s_ref[...] = jnp.zeros_like(s_ref)

    q, k, v = q_ref[...], k_ref[...], v_ref[...]
    zero = jnp.zeros_like(q)
    qs = jnp.concatenate([jnp.where(lane_head == h, q, zero) for h in range(RET_HEADS)], axis=0)
    sd = _mm_nt(qs, k) * d_ref[...]
    yv = _mm(sd, v)
    y = functools.reduce(jnp.add, [jnp.where(lane_head == h, yv[h * C:(h + 1) * C], 0.0) for h in range(RET_HEADS)])
    state = s_ref[...]
    y = y + _mm(q.astype(F32) * xi_ref[...], state)
    kv = _mm_tn(k.astype(F32) * zeta_ref[...], v)
    row_head = lax.broadcasted_iota(jnp.int32, (RET_WIDTH, 1), 0) // HEAD_DIM
    s_ref[...] = state * jnp.exp(lgv * C) + jnp.where(row_head == lane_head, kv, 0.0)

    if reverse:
        y = y + yf_ref[...]
        same = (row_head == lane_head).astype(F32) * (1.0 / HEAD_DIM)
        ms = _mm_hi(y * y, same)
        yn = y * lax.rsqrt(ms + EPS) * nw_ref[...]
        g = g_ref[...].astype(F32)
        o_ref[...] = ((g * _sigmoid(g)) * yn).astype(o_ref.dtype)
    else:
        o_ref[...] = y


def _retention_pass(lay, p, lgs, lgv, reverse, extra=()):
    B, L = lay.B, lay.L
    C, W = SEQ_TILE, RET_WIDTH
    nch = L // C
    ctx0 = lay.NL // C

    def rows(b, j):
        lat = b * nch + (nch - j if reverse else j - 1)
        return jnp.where(j == 0, ctx0 + b, lat)

    col = lambda c: pl.BlockSpec((C, W), lambda b, j: (rows(b, j), c))
    in_specs = [pl.BlockSpec(memory_space=pltpu.SMEM), pl.BlockSpec((1, W), lambda b, j: (0, 0)),
                col(OFF_RQ // W), col(OFF_RK // W), col(OFF_RV // W)]
    args = [lgs, lgv, p, p, p]
    if reverse:
        yf, nw = extra
        in_specs += [col(OFF_RG // W), col(0), pl.BlockSpec((1, W), lambda b, j: (0, 0))]
        args += [p, yf, nw.reshape(1, W)]
    return pl.pallas_call(
        functools.partial(_ret_body, reverse=reverse),
        grid=(B, nch + 1), in_specs=in_specs, out_specs=col(0),
        out_shape=jax.ShapeDtypeStruct((lay.T, W), ACT_DTYPE if reverse else F32),
        scratch_shapes=[pltpu.VMEM((W, W), F32), pltpu.VMEM((RET_HEADS * C, C), F32),
                        pltpu.VMEM((C, W), F32), pltpu.VMEM((C, W), F32)],
        compiler_params=_cparams("arbitrary", "arbitrary"),
        name="retention_bwd" if reverse else "retention_fwd",
    )(*args)


def _top2(vals):
    n = len(vals)
    v1 = functools.reduce(jnp.maximum, vals)
    i1 = jnp.full_like(v1, n - 1)
    for e in range(n - 2, -1, -1):
        i1 = jnp.where(vals[e] == v1, float(e), i1)
    rest = [jnp.where(i1 == float(e), -jnp.inf, vals[e]) for e in range(n)]
    v2 = functools.reduce(jnp.maximum, rest)
    i2 = jnp.full_like(v1, n - 1)
    for e in range(n - 2, -1, -1):
        i2 = jnp.where(rest[e] == v2, float(e), i2)
    return v1, i1, v2, i2


def _route(logits_t, bias):
    s = _sigmoid(logits_t)
    sel = s + bias
    groups = []
    for g in range(N_GROUPS):
        rows = [sel[g * EXPERTS_PER_GROUP + e:g * EXPERTS_PER_GROUP + e + 1] for e in range(EXPERTS_PER_GROUP)]
        groups.append(_top2(rows))
    score = [v1 + v2 for v1, _, v2, _ in groups]
    best = functools.reduce(jnp.maximum, score)
    gi = jnp.full_like(best, N_GROUPS - 1)
    for g in range(N_GROUPS - 2, -1, -1):
        gi = jnp.where(score[g] == best, float(g), gi)
    pick = lambda k: functools.reduce(
        lambda acc, g: jnp.where(gi == float(g), groups[g][k], acc), range(N_GROUPS - 1), groups[N_GROUPS - 1][k])
    e0 = gi * EXPERTS_PER_GROUP + pick(1)
    e1 = gi * EXPERTS_PER_GROUP + pick(3)
    s0 = jnp.zeros_like(best)
    s1 = jnp.zeros_like(best)
    for e in range(N_EXPERTS):
        s0 = jnp.where(e0 == float(e), s[e:e + 1], s0)
        s1 = jnp.where(e1 == float(e), s[e:e + 1], s1)
    tot = s0 + s1
    return e0, e1, s0 / tot, s1 / tot


def _outproj_body(*refs, lay, with_ctx):
    if with_ctx:
        (x_ref, al_ref, ac_ref, b_ref, r_ref, w_ref, g1_ref, nw_ref, sh_ref, sc_ref, rw_ref, xo_ref, hx_ref, lt_ref) = refs
        a = jnp.where(pl.program_id(0) < lay.n_lat_tiles, al_ref[...], ac_ref[...])
    else:
        (x_ref, al_ref, b_ref, r_ref, w_ref, g1_ref, nw_ref, sh_ref, sc_ref, rw_ref, xo_ref, hx_ref, lt_ref) = refs
        a = al_ref[...]
    D = lay.D
    y = (_mm(a, w_ref[:DIFF_WIDTH, :])
         + _mm(b_ref[...], w_ref[DIFF_WIDTH:DIFF_WIDTH + POOL_WIDTH, :])
         + _mm(r_ref[...], w_ref[DIFF_WIDTH + POOL_WIDTH:, :]))
    x = x_ref[...] + g1_ref[0] * y
    xo_ref[...] = x
    h = (_rms(x) * nw_ref[...]) * (1.0 + sc_ref[0]) + sh_ref[0]
    hx_ref[:, :D] = h
    hx_ref[:, D:] = jnp.zeros((TOKEN_TILE, LANES), F32)
    lt_ref[...] = jnp.transpose(_mm(h, rw_ref[...]))[:N_EXPERTS]


def _outproj(lay, layer, n_tok, x, a_lat, a_ctx, b, r, w_out, mods3, nw, rw_pad):
    D, TM = lay.D, TOKEN_TILE
    row = lambda i: (i, 0)
    const = lambda i: (0, 0)
    tile = lambda w: pl.BlockSpec((TM, w), row)
    n_lat = lay.n_lat_tiles
    in_specs = [tile(D), pl.BlockSpec((TM, DIFF_WIDTH), lambda i: (jnp.minimum(i, n_lat - 1), 0))]
    args = [x, a_lat]
    if a_ctx is not None:
        in_specs.append(pl.BlockSpec((TM, DIFF_WIDTH), lambda i: (jnp.maximum(i - n_lat, 0), 0)))
        args.append(a_ctx)
    in_specs += [tile(POOL_WIDTH), tile(RET_WIDTH),
                 pl.BlockSpec((DIFF_WIDTH + POOL_WIDTH + RET_WIDTH, D), const),
                 lay.mod_spec(layer, 2), pl.BlockSpec((1, D), const), lay.mod_spec(layer, 3), lay.mod_spec(layer, 4),
                 pl.BlockSpec((D, LANES), const)]
    args += [b, r, w_out, mods3, nw.reshape(1, D), mods3, mods3, rw_pad]
    return pl.pallas_call(
        functools.partial(_outproj_body, lay=lay, with_ctx=a_ctx is not None),
        grid=(n_tok // TM,), in_specs=in_specs,
        out_specs=[tile(D), tile(D + LANES), pl.BlockSpec((N_EXPERTS, TM), lambda i: (0, i))],
        out_shape=[jax.ShapeDtypeStruct((n_tok, D), F32), jax.ShapeDtypeStruct((n_tok, D + LANES), F32),
                   jax.ShapeDtypeStruct((N_EXPERTS, n_tok), F32)],
        compiler_params=_cparams("arbitrary"), name="outproj",
    )(*args)


def _router_body(lt_ref, rb_ref, tri_ref, hx_any, rec_ref, cr_ref, cnt_ref, run_ref):
    del hx_any
    n = lt_ref.shape[1]
    e0, e1, g0, g1 = _route(lt_ref[...], rb_ref[...])
    lo, hi = jnp.minimum(e0, e1), jnp.maximum(e0, e1)
    grp = jnp.floor(lo * (1.0 / EXPERTS_PER_GROUP))
    pa, pb = lo - grp * EXPERTS_PER_GROUP, hi - grp * EXPERTS_PER_GROUP
    cls = grp * PAIRS_PER_GROUP + pa * ((2 * EXPERTS_PER_GROUP - 1) - pa) * 0.5 + (pb - pa - 1.0)

    @pl.when(pl.program_id(0) == 0)
    def _():
        run_ref[...] = jnp.zeros_like(run_ref)

    run = run_ref[:, :1]
    onehot = jnp.where(cls == lax.broadcasted_iota(jnp.int32, (CLASS_ROWS, 1), 0).astype(F32), 1.0, 0.0)
    tri = tri_ref[...]
    ranks = []
    for c0 in range(0, n, TOKEN_TILE):
        oh = onehot[:, c0:c0 + TOKEN_TILE]
        incl = _mm_nt(oh, tri)
        ranks.append(jnp.sum(oh * (incl + run), axis=0, keepdims=True) - 1.0)
        run = run + incl[:, TOKEN_TILE - 1:TOKEN_TILE]
    rank = jnp.concatenate(ranks, axis=1)
    run_ref[...] = jnp.broadcast_to(run, run_ref.shape)
    cnt_ref[...] = jnp.broadcast_to(run, cnt_ref.shape)
    cr_ref[...] = jnp.concatenate([cls, rank, jnp.zeros((6, n), F32)], axis=0)
    rec_ref[...] = jnp.transpose(jnp.concatenate([e0, e1, g0, g1, cls, rank, jnp.zeros((LANES - 6, n), F32)], axis=0))


def _router(n_tok, D, hx, logits_t, rb_col, tri):
    blk = math.gcd(n_tok, ROUTER_BLOCK)
    return pl.pallas_call(
        _router_body,
        grid=(n_tok // blk,),
        in_specs=[pl.BlockSpec((N_EXPERTS, blk), lambda i: (0, i)), pl.BlockSpec((N_EXPERTS, 1), lambda i: (0, 0)),
                  pl.BlockSpec((TOKEN_TILE, TOKEN_TILE), lambda i: (0, 0)), pl.BlockSpec(memory_space=pl.ANY)],
        out_specs=[pl.BlockSpec((blk, LANES), lambda i: (i, D // LANES)), pl.BlockSpec((8, blk), lambda i: (0, i)),
                   pl.BlockSpec((CLASS_ROWS, LANES), lambda i: (0, 0))],
        out_shape=[jax.ShapeDtypeStruct((n_tok, D + LANES), F32), jax.ShapeDtypeStruct((8, n_tok), F32),
                   jax.ShapeDtypeStruct((CLASS_ROWS, LANES), F32)],
        scratch_shapes=[pltpu.VMEM((CLASS_ROWS, LANES), F32)],
        input_output_aliases={3: 0},
        compiler_params=_cparams("arbitrary"), name="router",
    )(logits_t, rb_col, tri, hx)


def _row_copies(idx_ref, hbm, vmem, sem, scatter):
    n = vmem.shape[0]
    def copy(u, row):
        src, dst = (vmem.at[pl.ds(u, 1)], hbm.at[pl.ds(row, 1)])
        return pltpu.make_async_copy(src, dst, sem) if scatter else pltpu.make_async_copy(dst, src, sem)

    def start(u, carry):
        copy(u, idx_ref[0, 0, u]).start()
        return carry

    def wait(u, carry):
        copy(0, 0).wait()
        return carry

    lax.fori_loop(0, n, start, 0, unroll=8)
    lax.fori_loop(0, n, wait, 0, unroll=8)


def _dispatch_body(pend_ref, padded_ref, nvalid_ref, pos_ref, hx_ref, xs_hbm, zbuf, zsem, sem, *, n_blk):
    TB = MOE_TILE
    i = pl.program_id(0)

    @pl.when(i == 0)
    def _():
        zbuf[...] = jnp.zeros_like(zbuf)
        fill = lambda row: pltpu.make_async_copy(zbuf, xs_hbm.at[pl.ds(row, TB)], zsem.at[0])
        for k in range(N_CLASSES):
            @pl.when(padded_ref[k] > 0)
            def _():
                fill(pl.multiple_of(pend_ref[k] - TB, TB)).start()
        for j in range(n_blk):
            @pl.when(j >= nvalid_ref[0])
            def _():
                fill(j * TB).start()
        for k in range(N_CLASSES):
            @pl.when(padded_ref[k] > 0)
            def _():
                fill(0).wait()
        for j in range(n_blk):
            @pl.when(j >= nvalid_ref[0])
            def _():
                fill(0).wait()

    _row_copies(pos_ref, xs_hbm, hx_ref, sem.at[0], True)


def _combine_body(pos_ref, ys_hbm, f_ref, sem):
    _row_copies(pos_ref, ys_hbm, f_ref, sem.at[0], False)


def _moe_body(elo_ref, ehi_ref, nvalid_ref, xs_ref, wg_lo, wu_lo, wd_lo, wg_hi, wu_hi, wd_hi, ys_ref, *, D):
    j = pl.program_id(0)

    @pl.when(j < nvalid_ref[0])
    def _():
        xb = xs_ref[:, :D].astype(MXU_DTYPE)
        info = xs_ref[:, D:]
        first_is_lo = info[:, REC_E0:REC_E0 + 1] <= info[:, REC_E1:REC_E1 + 1]
        g0, g1 = info[:, REC_G0:REC_G0 + 1], info[:, REC_G1:REC_G1 + 1]
        g_lo = jnp.where(first_is_lo, g0, g1)
        g_hi = jnp.where(first_is_lo, g1, g0)

        def ffn(wg, wu, wd):
            hg = jnp.dot(xb, wg[...], preferred_element_type=F32)
            hu = jnp.dot(xb, wu[...], preferred_element_type=F32)
            return _mm((hg * _sigmoid(hg)) * hu, wd[...])

        ys_ref[...] = g_lo * ffn(wg_lo, wu_lo, wd_lo) + g_hi * ffn(wg_hi, wu_hi, wd_hi)

    @pl.when(j >= nvalid_ref[0])
    def _():
        ys_ref[...] = jnp.zeros_like(ys_ref)


def _moe(n_tok, D, hx, cls_rank, counts, wg, wu, wd):
    TB = MOE_TILE
    DE = wg.shape[-1]
    chunk = math.gcd(n_tok, PERM_CHUNK)
    cls = cls_rank[0].astype(jnp.int32)
    rank = cls_rank[1].astype(jnp.int32)
    cnt = counts[:N_CLASSES, 0].astype(jnp.int32)
    padded = (cnt + TB - 1) // TB * TB
    pend = jnp.cumsum(padded)
    pstart = pend - padded
    onehot = cls[:, None] == jnp.arange(N_CLASSES, dtype=jnp.int32)[None, :]
    pos = jnp.sum(jnp.where(onehot, pstart[None, :], 0), axis=1) + rank
    n_blk = -(-n_tok // TB) + N_CLASSES
    P = n_blk * TB
    nvalid = (pend[-1] // TB).astype(jnp.int32).reshape(1)
    blk = jnp.minimum(jnp.arange(n_blk, dtype=jnp.int32), nvalid - 1)
    blk_cls = jnp.minimum(jnp.sum(pend[None, :] <= (blk * TB)[:, None], axis=1), N_CLASSES - 1).astype(jnp.int32)
    pairs = [(u, v) for u in range(EXPERTS_PER_GROUP) for v in range(u + 1, EXPERTS_PER_GROUP)]
    pair_lo = jnp.array([u for u, _ in pairs], jnp.int32)
    pair_hi = jnp.array([v for _, v in pairs], jnp.int32)
    base = (blk_cls // PAIRS_PER_GROUP) * EXPERTS_PER_GROUP
    blk_lo = base + pair_lo[blk_cls % PAIRS_PER_GROUP]
    blk_hi = base + pair_hi[blk_cls % PAIRS_PER_GROUP]
    pos3 = pos.reshape(n_tok // chunk, 1, chunk)
    pos_spec = pl.BlockSpec((1, 1, chunk), lambda i, *_: (i, 0, 0), memory_space=pltpu.SMEM)
    any_spec = pl.BlockSpec(memory_space=pl.ANY)

    xs = pl.pallas_call(
        functools.partial(_dispatch_body, n_blk=n_blk),
        grid_spec=pltpu.PrefetchScalarGridSpec(
            num_scalar_prefetch=3, grid=(n_tok // chunk,),
            in_specs=[pos_spec, pl.BlockSpec((chunk, D + LANES), lambda i, *_: (i, 0))], out_specs=any_spec,
            scratch_shapes=[pltpu.VMEM((TB, D + LANES), F32), pltpu.SemaphoreType.DMA((1,)),
                            pltpu.SemaphoreType.DMA((1,))]),
        out_shape=jax.ShapeDtypeStruct((P, D + LANES), F32),
        compiler_params=_cparams("arbitrary"), name="moe_dispatch",
    )(pend.astype(jnp.int32), padded.astype(jnp.int32), nvalid, pos3, hx)

    w_spec = lambda shape, which: pl.BlockSpec(
        (None,) + shape, (lambda j, elo, ehi, nv: (elo[j], 0, 0)) if which == 0 else (lambda j, elo, ehi, nv: (ehi[j], 0, 0)))
    ys = pl.pallas_call(
        functools.partial(_moe_body, D=D),
        grid_spec=pltpu.PrefetchScalarGridSpec(
            num_scalar_prefetch=3, grid=(n_blk,),
            in_specs=[pl.BlockSpec((TB, D + LANES), lambda j, elo, ehi, nv: (jnp.minimum(j, nv[0] - 1), 0)),
                      w_spec((D, DE), 0), w_spec((D, DE), 0), w_spec((DE, D), 0),
                      w_spec((D, DE), 1), w_spec((D, DE), 1), w_spec((DE, D), 1)],
            out_specs=pl.BlockSpec((TB, D), lambda j, elo, ehi, nv: (j, 0))),
        out_shape=jax.ShapeDtypeStruct((P, D), F32),
        compiler_params=_cparams("arbitrary"), name="moe_experts",
    )(blk_lo, blk_hi, nvalid, xs, wg, wu, wd, wg, wu, wd)

    return pl.pallas_call(
        _combine_body,
        grid_spec=pltpu.PrefetchScalarGridSpec(
            num_scalar_prefetch=0, grid=(n_tok // chunk,), in_specs=[pos_spec, any_spec],
            out_specs=pl.BlockSpec((chunk, D), lambda i: (i, 0)),
            scratch_shapes=[pltpu.SemaphoreType.DMA((1,))]),
        out_shape=jax.ShapeDtypeStruct((n_tok, D), F32),
        compiler_params=_cparams("arbitrary"), name="moe_combine",
    )(pos3, ys)


def _final_body(x_ref, f_ref, g2_ref, nw_ref, o_ref):
    o_ref[...] = _rms(x_ref[...] + g2_ref[0] * f_ref[...]) * nw_ref[...]


def _final(lay, layer, x, f, mods3, nw):
    D, TM = lay.D, TOKEN_TILE
    row = lambda i: (i, 0)
    return pl.pallas_call(
        _final_body, grid=(lay.NL // TM,),
        in_specs=[pl.BlockSpec((TM, D), row), pl.BlockSpec((TM, D), row), lay.mod_spec(layer, 5),
                  pl.BlockSpec((1, D), lambda i: (0, 0))],
        out_specs=pl.BlockSpec((TM, D), row),
        out_shape=jax.ShapeDtypeStruct((lay.NL, D), F32),
        compiler_params=_cparams("arbitrary"), name="final_norm",
    )(x, f, mods3, nw.reshape(1, D))


def kernel(x, c, ctx, c_ctx, w_in, w_out, ada_w, ada_b, norm1_w, norm2_w, diff_lambda, diff_norm_w, pool_w,
           pool_scale, ret_a_f, ret_a_b, ret_norm_w, router_w, router_b, moe_w_gate, moe_w_up, moe_w_down,
           final_norm_w):
    B, L, D = x.shape
    Lc = ctx.shape[1]
    depth = w_in.shape[0]
    lay = _Layout(B, L, Lc, D)
    assert B + 1 <= MOD_ROWS

    xa = (x.reshape(B * L, D), ctx.reshape(B * Lc, D))
    c_all = jnp.zeros((MOD_ROWS, D), F32).at[:B].set(c).at[B].set(c_ctx)
    mods3 = _ada_table(c_all, ada_w, ada_b).reshape(depth * MOD_ROWS * N_MOD, 1, D)
    tables = _rope_tables(lay)
    rw_pad = jnp.zeros((D, LANES), MXU_DTYPE).at[:, :N_EXPERTS].set(router_w.astype(MXU_DTYPE))
    rb_col = router_b.astype(F32).reshape(N_EXPERTS, 1)
    tri = jnp.tril(jnp.ones((TOKEN_TILE, TOKEN_TILE), F32)).astype(MXU_DTYPE)
    w_in_b, w_out_b = w_in.astype(MXU_DTYPE), w_out.astype(MXU_DTYPE)
    wg_b, wu_b, wd_b = moe_w_gate.astype(MXU_DTYPE), moe_w_up.astype(MXU_DTYPE), moe_w_down.astype(MXU_DTYPE)

    f = None
    for l in range(depth):
        last = l == depth - 1
        lam_init = 0.8 - 0.6 * math.exp(-0.3 * l)
        dl = diff_lambda[l].astype(F32)
        lam = (jnp.exp(jnp.sum(dl[0] * dl[1])) - jnp.exp(jnp.sum(dl[2] * dl[3])) + lam_init).reshape(1)
        p, xa = _inproj(lay, l, xa, f, mods3, norm1_w[l], w_in_b[l], tables)
        a_lat, a_ctx = _diff_attention(lay, p, lam, diff_norm_w[l], 1.0 - lam_init, not last)
        eye = jnp.eye(len(POOL_WINDOWS), dtype=F32)
        w_bd = (eye[:, None, :, None] * pool_w[l][:, :, None, :]).reshape(POOL_WIDTH, POOL_WIDTH).astype(MXU_DTYPE)
        bp = _pool(lay, p, w_bd, pool_scale[l])
        lg_f = -jnp.exp(ret_a_f[l].astype(F32))
        lg_b = -jnp.exp(ret_a_b[l].astype(F32))
        lanes = lambda lg: jnp.repeat(lg, HEAD_DIM).reshape(1, RET_WIDTH)
        yf = _retention_pass(lay, p, lg_f, lanes(lg_f), False)
        r = _retention_pass(lay, p, lg_b, lanes(lg_b), True, (yf, ret_norm_w[l]))
        n_tok = lay.NL if last else lay.T
        xa, hx, logits_t = _outproj(lay, l, n_tok, xa, a_lat, a_ctx, bp, r, w_out_b[l], mods3, norm2_w[l], rw_pad)
        hx, cls_rank, counts = _router(n_tok, D, hx, logits_t, rb_col, tri)
        f = _moe(n_tok, D, hx, cls_rank, counts, wg_b[l], wu_b[l], wd_b[l])
    out = _final(lay, depth - 1, xa, f, mods3, final_norm_w)
    return out.reshape(B, L, D)
```

```python
import functools
import math

import jax
import jax.numpy as jnp
from jax import lax
from jax.experimental import pallas as pl
from jax.experimental.pallas import tpu as pltpu

F32 = jnp.float32
MXU_DTYPE = jnp.bfloat16
ACT_DTYPE = jnp.bfloat16

GRID_W = 64
HEAD_DIM = 64
DIFF_HEADS = 4
DIFF_V_DIM = 2 * HEAD_DIM
DIFF_WIDTH = DIFF_HEADS * DIFF_V_DIM
POOL_WINDOWS = (2, 4, 8, 16)
POOL_GROUP = 64
POOL_WIDTH = POOL_GROUP * len(POOL_WINDOWS)
RET_HEADS = 4
RET_WIDTH = RET_HEADS * HEAD_DIM
OFF_DQ = 0
OFF_DK = OFF_DQ + DIFF_WIDTH
OFF_DV = OFF_DK + DIFF_WIDTH
OFF_PU = OFF_DV + DIFF_WIDTH
OFF_RQ = OFF_PU + POOL_WIDTH
OFF_RK = OFF_RQ + RET_WIDTH
OFF_RV = OFF_RK + RET_WIDTH
OFF_RG = OFF_RV + RET_WIDTH
IN_WIDTH = OFF_RG + RET_WIDTH
ROPE_BASE = 10000.0
ROPE_FREQS = HEAD_DIM // 4
N_EXPERTS = 16
N_GROUPS = 4
EXPERTS_PER_GROUP = N_EXPERTS // N_GROUPS
PAIRS_PER_GROUP = EXPERTS_PER_GROUP * (EXPERTS_PER_GROUP - 1) // 2
N_CLASSES = N_GROUPS * PAIRS_PER_GROUP
N_MOD = 6
EPS = 1e-6

LANES = 128
MOD_ROWS = 16
TOKEN_TILE = 512
SEQ_TILE = 256
ATTN_SUBTILES = 2
ATTN_SUB_ROWS = 256
ATTN_KEY_CHUNK = 512
HALO = 16
MOE_TILE = 256
COL = 256
ROUTER_BLOCK = 2048
CLASS_ROWS = 32
PERM_CHUNK = 2048
REC_E0, REC_E1, REC_G0, REC_G1, REC_CLS, REC_RANK = range(6)
VMEM_LIMIT = 56 * 1024 * 1024


def _mm(a, b):
    return jnp.dot(a.astype(MXU_DTYPE), b.astype(MXU_DTYPE), preferred_element_type=F32)


def _mm_nt(a, b):
    return lax.dot_general(a.astype(MXU_DTYPE), b.astype(MXU_DTYPE), (((1,), (1,)), ((), ())),
                           preferred_element_type=F32)


def _mm_tn(a, b):
    return lax.dot_general(a.astype(MXU_DTYPE), b.astype(MXU_DTYPE), (((0,), (0,)), ((), ())),
                           preferred_element_type=F32)


def _split(x):
    hi = x.astype(MXU_DTYPE)
    lo = (x - hi.astype(F32)).astype(MXU_DTYPE)
    return hi, lo


def _mm_hi(a, b):
    ah, al = _split(a)
    bh, bl = _split(b)
    d = lambda u, v: jnp.dot(u, v, preferred_element_type=F32)
    return d(ah, bh) + (d(ah, bl) + d(al, bh))


def _sigmoid(x):
    return 1.0 / (1.0 + jnp.exp(-x))


def _rms(x):
    return x * lax.rsqrt(jnp.mean(x * x, axis=-1, keepdims=True) + EPS)


def _cparams(*sem):
    return pltpu.CompilerParams(dimension_semantics=sem, vmem_limit_bytes=VMEM_LIMIT)


def _ada_body(c_ref, w_ref, b_ref, o_ref):
    c = c_ref[...]
    o_ref[...] = _mm_hi(c * _sigmoid(c), w_ref[...]) + b_ref[...]


def _ada_table(c_all, ada_w, ada_b):
    depth, d, n = ada_w.shape
    nb = n // N_MOD
    return pl.pallas_call(
        _ada_body,
        grid=(depth, n // nb),
        in_specs=[pl.BlockSpec((MOD_ROWS, d), lambda l, j: (0, 0)),
                  pl.BlockSpec((None, d, nb), lambda l, j: (l, 0, j)),
                  pl.BlockSpec((None, 1, nb), lambda l, j: (l, 0, j))],
        out_specs=pl.BlockSpec((None, MOD_ROWS, nb), lambda l, j: (l, 0, j)),
        out_shape=jax.ShapeDtypeStruct((depth, MOD_ROWS, n), F32),
        compiler_params=_cparams("arbitrary", "arbitrary"),
        name="ada_table",
    )(c_all, ada_w, ada_b.reshape(depth, 1, n))


class _Layout:
    def __init__(self, B, L, Lc, D):
        self.B, self.L, self.Lc, self.D = B, L, Lc, D
        self.NL, self.NC = B * L, B * Lc
        self.T = self.NL + self.NC
        assert L % TOKEN_TILE == 0 and self.NC % TOKEN_TILE == 0
        assert L % SEQ_TILE == 0 and Lc == SEQ_TILE and L % GRID_W == 0
        self.n_lat_tiles = self.NL // TOKEN_TILE
        self.tiles_per_seq = L // TOKEN_TILE

    def mod_row(self, i):
        return jnp.where(i < self.n_lat_tiles, i // self.tiles_per_seq, self.B)

    def rope_block(self, i):
        return jnp.where(i < self.n_lat_tiles, i % self.tiles_per_seq, self.tiles_per_seq)

    def mod_spec(self, layer, k):
        base = layer * MOD_ROWS * N_MOD + k
        return pl.BlockSpec((1, 1, self.D), lambda i: (base + self.mod_row(i) * N_MOD, 0, 0))


def _rope_tables(lay):
    L = lay.L
    rows = L // GRID_W
    row = jnp.repeat(jnp.arange(rows), GRID_W).astype(F32)
    col = jnp.tile(jnp.arange(GRID_W), rows).astype(F32)
    inv = ROPE_BASE ** (-jnp.arange(ROPE_FREQS, dtype=F32) / ROPE_FREQS)
    ang_r = row[:, None] * inv
    ang_c = col[:, None] * inv
    ang = jnp.concatenate([ang_r, ang_r, ang_c, ang_c], axis=-1)
    cos, sin = jnp.cos(ang), jnp.sin(ang)
    first_half = (jnp.arange(HEAD_DIM) % (2 * ROPE_FREQS)) < ROPE_FREQS
    sin_a = jnp.where(first_half, -sin, 0.0)
    sin_b = jnp.where(first_half, 0.0, sin)
    rep = LANES // HEAD_DIM
    ident = [jnp.ones((TOKEN_TILE, LANES), F32), jnp.zeros((TOKEN_TILE, LANES), F32),
             jnp.zeros((TOKEN_TILE, LANES), F32)]
    return [jnp.concatenate([jnp.tile(t, (1, rep)), e], axis=0) for t, e in zip((cos, sin_a, sin_b), ident)]


_ROPE_COLS = tuple(range(OFF_DQ // COL, OFF_DV // COL)) + (OFF_RQ // COL, OFF_RK // COL)
LOG2E = 1.4426950408889634
_COL_SCALE = {j: HEAD_DIM ** -0.5 * LOG2E for j in range(OFF_DQ // COL, OFF_DK // COL)}
_COL_SCALE[OFF_RK // COL] = HEAD_DIM ** -0.5


def _inproj_body(*refs, with_f, n_lat_tiles):
    if with_f:
        (x_ref, f_ref, g2_ref, nw_ref, sh_ref, sc_ref, w_ref, cos_ref, sa_ref, sb_ref, p_ref, xo_ref) = refs
        x = x_ref[...] + g2_ref[0] * f_ref[...]
    else:
        (xl_ref, xc_ref, nw_ref, sh_ref, sc_ref, w_ref, cos_ref, sa_ref, sb_ref, p_ref, xo_ref) = refs
        x = jnp.where(pl.program_id(0) < n_lat_tiles, xl_ref[...], xc_ref[...])
    xo_ref[...] = x
    h = (_rms(x) * nw_ref[...]) * (1.0 + sc_ref[0]) + sh_ref[0]
    hb = h.astype(MXU_DTYPE)
    cos, sa, sb = cos_ref[...], sa_ref[...], sb_ref[...]
    for j in range(IN_WIDTH // COL):
        acc = jnp.dot(hb, w_ref[:, j * COL:(j + 1) * COL], preferred_element_type=F32)
        if j in _ROPE_COLS:
            parts = []
            for t in range(COL // LANES):
                a = acc[:, t * LANES:(t + 1) * LANES]
                parts.append(a * cos + pltpu.roll(a, LANES - ROPE_FREQS, 1) * sa + pltpu.roll(a, ROPE_FREQS, 1) * sb)
            acc = jnp.concatenate(parts, axis=1)
        if j in _COL_SCALE:
            acc = acc * _COL_SCALE[j]
        p_ref[:, j * COL:(j + 1) * COL] = acc.astype(p_ref.dtype)


def _inproj(lay, layer, x, f, mods3, nw, w_in, tables):
    D, TM = lay.D, TOKEN_TILE
    row = lambda i: (i, 0)
    const = lambda i: (0, 0)
    n_lat = lay.n_lat_tiles
    tbl = pl.BlockSpec((TM, LANES), lambda i: (lay.rope_block(i), 0))
    if f is not None:
        in_specs = [pl.BlockSpec((TM, D), row), pl.BlockSpec((TM, D), row), lay.mod_spec(layer - 1, 5)]
        args = [x, f, mods3]
    else:
        in_specs = [pl.BlockSpec((TM, D), lambda i: (jnp.minimum(i, n_lat - 1), 0)),
                    pl.BlockSpec((TM, D), lambda i: (jnp.maximum(i - n_lat, 0), 0))]
        args = list(x)
    in_specs += [pl.BlockSpec((1, D), const), lay.mod_spec(layer, 0), lay.mod_spec(layer, 1),
                 pl.BlockSpec((None, D, IN_WIDTH), lambda i: (layer, 0, 0)), tbl, tbl, tbl]
    args += [nw.reshape(1, D), mods3, mods3, w_in] + list(tables)
    return pl.pallas_call(
        functools.partial(_inproj_body, with_f=f is not None, n_lat_tiles=n_lat),
        grid=(lay.T // TM,), in_specs=in_specs,
        out_specs=[pl.BlockSpec((TM, IN_WIDTH), row), pl.BlockSpec((TM, D), row)],
        out_shape=[jax.ShapeDtypeStruct((lay.T, IN_WIDTH), ACT_DTYPE), jax.ShapeDtypeStruct((lay.T, D), F32)],
        compiler_params=_cparams("arbitrary"), name="inproj",
    )(*args)


def _map_masks(q):
    lane = lax.broadcasted_iota(jnp.int32, (1, DIFF_V_DIM), 1)
    zero = jnp.zeros_like(q)
    return [jnp.where(lane < HEAD_DIM, q, zero), jnp.where(lane >= HEAD_DIM, q, zero)]


def _scores(qm, k_refs):
    chunks, run = [], None
    for k in k_refs:
        for c0 in range(0, k.shape[0], ATTN_KEY_CHUNK):
            s = _mm_nt(qm, k[c0:c0 + ATTN_KEY_CHUNK, :])
            chunks.append(s)
            for t in range(s.shape[1] // LANES):
                piece = s[:, t * LANES:(t + 1) * LANES]
                run = piece if run is None else jnp.maximum(run, piece)
    return chunks, jnp.max(run, axis=-1, keepdims=True)


def _softmax_diff(maps, lam):
    es, ls = [], []
    for chunks, mx in maps:
        e = [jnp.exp2(t - mx) for t in chunks]
        ls.append(functools.reduce(jnp.add, [jnp.sum(t, axis=-1, keepdims=True) for t in e]))
        es.append([t.astype(MXU_DTYPE) for t in e])
    c = (lam * ls[0] / ls[1]).astype(MXU_DTYPE)
    return [e1 - e2 * c for e1, e2 in zip(es[0], es[1])], 1.0 / ls[0]


def _head_norm(o, w, post_scale):
    return (_rms(o) * w) * post_scale


def _attn_body(lam_ref, q_ref, *refs, n_seg, n_sub, post_scale):
    k_refs, v_refs = refs[:n_seg], refs[n_seg:2 * n_seg]
    w_ref, o_ref = refs[2 * n_seg], refs[2 * n_seg + 1]
    rows = q_ref.shape[0] // n_sub

    def finish(u, maps):
        p, r1 = _softmax_diff(maps, lam_ref[0])
        o, at = None, 0
        for v in v_refs:
            n_c = -(-v.shape[0] // ATTN_KEY_CHUNK)
            part = jnp.dot(jnp.concatenate(p[at:at + n_c], axis=1), v[...], preferred_element_type=F32)
            o = part if o is None else o + part
            at += n_c
        o_ref[u * rows:(u + 1) * rows, :] = _head_norm(o * r1, w_ref[...], post_scale).astype(o_ref.dtype)

    scores = [[_scores(qm, k_refs) for qm in _map_masks(q_ref[u * rows:(u + 1) * rows, :])] for u in range(n_sub)]
    for u, maps in enumerate(scores):
        finish(u, maps)


def _diff_attention(lay, p, lam, norm_w, post_scale, need_ctx):
    B, L, Lc, H = lay.B, lay.L, lay.Lc, DIFF_HEADS
    W = DIFF_V_DIM
    tq = ATTN_SUBTILES * ATTN_SUB_ROWS
    assert L % tq == 0
    nq = L // tq
    k_blk, v_blk = OFF_DK // W, OFF_DV // W
    ctx0 = lay.NL // Lc
    lam_spec = pl.BlockSpec(memory_space=pltpu.SMEM)
    w_spec = pl.BlockSpec((1, W), lambda *_: (0, 0))
    a_lat = pl.pallas_call(
        functools.partial(_attn_body, n_seg=2, n_sub=ATTN_SUBTILES, post_scale=post_scale),
        grid=(B, H, nq),
        in_specs=[lam_spec,
                  pl.BlockSpec((tq, W), lambda b, h, i: (b * nq + i, h)),
                  pl.BlockSpec((Lc, W), lambda b, h, i: (ctx0 + b, k_blk + h)),
                  pl.BlockSpec((L, W), lambda b, h, i: (b, k_blk + h)),
                  pl.BlockSpec((Lc, W), lambda b, h, i: (ctx0 + b, v_blk + h)),
                  pl.BlockSpec((L, W), lambda b, h, i: (b, v_blk + h)),
                  w_spec],
        out_specs=pl.BlockSpec((tq, W), lambda b, h, i: (b * nq + i, h)),
        out_shape=jax.ShapeDtypeStruct((lay.NL, DIFF_WIDTH), ACT_DTYPE),
        compiler_params=_cparams("arbitrary", "arbitrary", "arbitrary"),
        name="diff_attn",
    )(lam, p, p, p, p, p, norm_w.reshape(1, W))
    if not need_ctx:
        return a_lat, None
    a_ctx = pl.pallas_call(
        functools.partial(_attn_body, n_seg=1, n_sub=1, post_scale=post_scale),
        grid=(B, H),
        in_specs=[lam_spec,
                  pl.BlockSpec((Lc, W), lambda b, h: (ctx0 + b, h)),
                  pl.BlockSpec((Lc, W), lambda b, h: (ctx0 + b, k_blk + h)),
                  pl.BlockSpec((Lc, W), lambda b, h: (ctx0 + b, v_blk + h)),
                  w_spec],
        out_specs=pl.BlockSpec((Lc, W), lambda b, h: (b, h)),
        out_shape=jax.ShapeDtypeStruct((lay.NC, DIFF_WIDTH), ACT_DTYPE),
        compiler_params=_cparams("arbitrary", "arbitrary"),
        name="diff_attn_ctx",
    )(lam, p, p, p, norm_w.reshape(1, W))
    return a_lat, a_ctx


def _pool_body(prev_ref, cur_ref, next_ref, w_ref, scale_ref, o_ref, *, lay):
    i = pl.program_id(0)
    n_lat = lay.NL // SEQ_TILE
    per_seq = jnp.where(i < n_lat, lay.L // SEQ_TILE, lay.Lc // SEQ_TILE)
    idx = jnp.where(i < n_lat, i, i - n_lat) % per_seq
    seq_len = per_seq * SEQ_TILE
    has_prev = (idx > 0).astype(F32)
    has_next = (idx < per_seq - 1).astype(F32)
    cur = cur_ref[...].astype(F32)
    u = jnp.concatenate([prev_ref[...].astype(F32) * has_prev, cur, next_ref[...].astype(F32) * has_next], axis=0)
    sums = {1: u}
    w = 1
    while w < POOL_WINDOWS[-1]:
        s = sums[w]
        m = s.shape[0] - w
        sums[2 * w] = s[:m] + s[w:w + m]
        w *= 2
    pos = idx * SEQ_TILE + lax.broadcasted_iota(jnp.int32, (SEQ_TILE, 1), 0)
    group = lax.broadcasted_iota(jnp.int32, (1, POOL_WIDTH), 1) // POOL_GROUP
    mean = jnp.zeros((SEQ_TILE, POOL_WIDTH), F32)
    for g, win in enumerate(POOL_WINDOWS):
        start = HALO - win // 2
        cnt = jnp.minimum(pos + (win - win // 2), seq_len) - jnp.maximum(pos - win // 2, 0)
        mean = jnp.where(group == g, sums[win][start:start + SEQ_TILE] / cnt.astype(F32), mean)
    o_ref[...] = (_mm(mean - cur, w_ref[...]) * scale_ref[...]).astype(o_ref.dtype)


def _pool(lay, p, w_bd, scale):
    n = lay.T // SEQ_TILE
    col = OFF_PU // POOL_WIDTH
    per = SEQ_TILE // HALO
    last = lay.T // HALO - 1
    return pl.pallas_call(
        functools.partial(_pool_body, lay=lay),
        grid=(n,),
        in_specs=[pl.BlockSpec((HALO, POOL_WIDTH), lambda i: (jnp.maximum(i * per - 1, 0), col)),
                  pl.BlockSpec((SEQ_TILE, POOL_WIDTH), lambda i: (i, col)),
                  pl.BlockSpec((HALO, POOL_WIDTH), lambda i: (jnp.minimum((i + 1) * per, last), col)),
                  pl.BlockSpec((POOL_WIDTH, POOL_WIDTH), lambda i: (0, 0)),
                  pl.BlockSpec((1, POOL_WIDTH), lambda i: (0, 0))],
        out_specs=pl.BlockSpec((SEQ_TILE, POOL_WIDTH), lambda i: (i, 0)),
        out_shape=jax.ShapeDtypeStruct((lay.T, POOL_WIDTH), ACT_DTYPE),
        compiler_params=_cparams("arbitrary"), name="pool",
    )(p, p, p, w_bd, scale.reshape(1, POOL_WIDTH))


def _ret_body(lgs_ref, lgv_ref, q_ref, k_ref, v_ref, *refs, reverse):
    C = SEQ_TILE
    if reverse:
        g_ref, yf_ref, nw_ref, o_ref, s_ref, d_ref, xi_ref, zeta_ref = refs
    else:
        o_ref, s_ref, d_ref, xi_ref, zeta_ref = refs
    b, j = pl.program_id(0), pl.program_id(1)
    lane_head = lax.broadcasted_iota(jnp.int32, (1, RET_WIDTH), 1) // HEAD_DIM
    lgv = lgv_ref[...]

    @pl.when((b == 0) & (j == 0))
    def _():
        ri = lax.broadcasted_iota(jnp.int32, (C, C), 0)
        ci = lax.broadcasted_iota(jnp.int32, (C, C), 1)
        dist = (ci - ri if reverse else ri - ci).astype(F32)
        keep = dist > 0 if reverse else dist >= 0
        for h in range(RET_HEADS):
            d_ref[h * C:(h + 1) * C, :] = jnp.where(keep, jnp.exp(lgs_ref[h] * jnp.maximum(dist, 0.0)), 0.0)
        t = lax.broadcasted_iota(jnp.int32, (C, 1), 0).astype(F32)
        xi_ref[...] = jnp.exp(lgv * ((C - t) if reverse else (t + 1.0)))
        zeta_ref[...] = jnp.exp(lgv * (t if reverse else (C - 1.0 - t)))

    @pl.when(j == 0)
    def _():
        s_ref[...] = jnp.zeros_like(s_ref)

    q, k, v = q_ref[...], k_ref[...], v_ref[...]
    zero = jnp.zeros_like(q)
    qs = jnp.concatenate([jnp.where(lane_head == h, q, zero) for h in range(RET_HEADS)], axis=0)
    sd = _mm_nt(qs, k) * d_ref[...]
    yv = _mm(sd, v)
    y = functools.reduce(jnp.add, [jnp.where(lane_head == h, yv[h * C:(h + 1) * C], 0.0) for h in range(RET_HEADS)])
    state = s_ref[...]
    y = y + _mm(q.astype(F32) * xi_ref[...], state)
    kv = _mm_tn(k.astype(F32) * zeta_ref[...], v)
    row_head = lax.broadcasted_iota(jnp.int32, (RET_WIDTH, 1), 0) // HEAD_DIM
    s_ref[...] = state * jnp.exp(lgv * C) + jnp.where(row_head == lane_head, kv, 0.0)

    if reverse:
        y = y + yf_ref[...]
        same = (row_head == lane_head).astype(F32) * (1.0 / HEAD_DIM)
        ms = _mm_hi(y * y, same)
        yn = y * lax.rsqrt(ms + EPS) * nw_ref[...]
        g = g_ref[...].astype(F32)
        o_ref[...] = ((g * _sigmoid(g)) * yn).astype(o_ref.dtype)
    else:
        o_ref[...] = y


def _retention_pass(lay, p, lgs, lgv, reverse, extra=()):
    B, L = lay.B, lay.L
    C, W = SEQ_TILE, RET_WIDTH
    nch = L // C
    ctx0 = lay.NL // C

    def rows(b, j):
        lat = b * nch + (nch - j if reverse else j - 1)
        return jnp.where(j == 0, ctx0 + b, lat)

    col = lambda c: pl.BlockSpec((C, W), lambda b, j: (rows(b, j), c))
    in_specs = [pl.BlockSpec(memory_space=pltpu.SMEM), pl.BlockSpec((1, W), lambda b, j: (0, 0)),
                col(OFF_RQ // W), col(OFF_RK // W), col(OFF_RV // W)]
    args = [lgs, lgv, p, p, p]
    if reverse:
        yf, nw = extra
        in_specs += [col(OFF_RG // W), col(0), pl.BlockSpec((1, W), lambda b, j: (0, 0))]
        args += [p, yf, nw.reshape(1, W)]
    return pl.pallas_call(
        functools.partial(_ret_body, reverse=reverse),
        grid=(B, nch + 1), in_specs=in_specs, out_specs=col(0),
        out_shape=jax.ShapeDtypeStruct((lay.T, W), ACT_DTYPE if reverse else F32),
        scratch_shapes=[pltpu.VMEM((W, W), F32), pltpu.VMEM((RET_HEADS * C, C), F32),
                        pltpu.VMEM((C, W), F32), pltpu.VMEM((C, W), F32)],
        compiler_params=_cparams("arbitrary", "arbitrary"),
        name="retention_bwd" if reverse else "retention_fwd",
    )(*args)


def _top2(vals):
    n = len(vals)
    v1 = functools.reduce(jnp.maximum, vals)
    i1 = jnp.full_like(v1, n - 1)
    for e in range(n - 2, -1, -1):
        i1 = jnp.where(vals[e] == v1, float(e), i1)
    rest = [jnp.where(i1 == float(e), -jnp.inf, vals[e]) for e in range(n)]
    v2 = functools.reduce(jnp.maximum, rest)
    i2 = jnp.full_like(v1, n - 1)
    for e in range(n - 2, -1, -1):
        i2 = jnp.where(rest[e] == v2, float(e), i2)
    return v1, i1, v2, i2


def _route(logits_t, bias):
    s = _sigmoid(logits_t)
    sel = s + bias
    groups = []
    for g in range(N_GROUPS):
        rows = [sel[g * EXPERTS_PER_GROUP + e:g * EXPERTS_PER_GROUP + e + 1] for e in range(EXPERTS_PER_GROUP)]
        groups.append(_top2(rows))
    score = [v1 + v2 for v1, _, v2, _ in groups]
    best = functools.reduce(jnp.maximum, score)
    gi = jnp.full_like(best, N_GROUPS - 1)
    for g in range(N_GROUPS - 2, -1, -1):
        gi = jnp.where(score[g] == best, float(g), gi)
    pick = lambda k: functools.reduce(
        lambda acc, g: jnp.where(gi == float(g), groups[g][k], acc), range(N_GROUPS - 1), groups[N_GROUPS - 1][k])
    e0 = gi * EXPERTS_PER_GROUP + pick(1)
    e1 = gi * EXPERTS_PER_GROUP + pick(3)
    s0 = jnp.zeros_like(best)
    s1 = jnp.zeros_like(best)
    for e in range(N_EXPERTS):
        s0 = jnp.where(e0 == float(e), s[e:e + 1], s0)
        s1 = jnp.where(e1 == float(e), s[e:e + 1], s1)
    tot = s0 + s1
    return e0, e1, s0 / tot, s1 / tot


def _outproj_body(*refs, lay, with_ctx):
    if with_ctx:
        (x_ref, al_ref, ac_ref, b_ref, r_ref, w_ref, g1_ref, nw_ref, sh_ref, sc_ref, rw_ref, xo_ref, hx_ref, lt_ref) = refs
        a = jnp.where(pl.program_id(0) < lay.n_lat_tiles, al_ref[...], ac_ref[...])
    else:
        (x_ref, al_ref, b_ref, r_ref, w_ref, g1_ref, nw_ref, sh_ref, sc_ref, rw_ref, xo_ref, hx_ref, lt_ref) = refs
        a = al_ref[...]
    D = lay.D
    y = (_mm(a, w_ref[:DIFF_WIDTH, :])
         + _mm(b_ref[...], w_ref[DIFF_WIDTH:DIFF_WIDTH + POOL_WIDTH, :])
         + _mm(r_ref[...], w_ref[DIFF_WIDTH + POOL_WIDTH:, :]))
    x = x_ref[...] + g1_ref[0] * y
    xo_ref[...] = x
    h = (_rms(x) * nw_ref[...]) * (1.0 + sc_ref[0]) + sh_ref[0]
    hx_ref[:, :D] = h
    hx_ref[:, D:] = jnp.zeros((TOKEN_TILE, LANES), F32)
    lt_ref[...] = jnp.transpose(_mm(h, rw_ref[...]))[:N_EXPERTS]


def _outproj(lay, layer, n_tok, x, a_lat, a_ctx, b, r, w_out, mods3, nw, rw_pad):
    D, TM = lay.D, TOKEN_TILE
    row = lambda i: (i, 0)
    const = lambda i: (0, 0)
    tile = lambda w: pl.BlockSpec((TM, w), row)
    n_lat = lay.n_lat_tiles
    in_specs = [tile(D), pl.BlockSpec((TM, DIFF_WIDTH), lambda i: (jnp.minimum(i, n_lat - 1), 0))]
    args = [x, a_lat]
    if a_ctx is not None:
        in_specs.append(pl.BlockSpec((TM, DIFF_WIDTH), lambda i: (jnp.maximum(i - n_lat, 0), 0)))
        args.append(a_ctx)
    in_specs += [tile(POOL_WIDTH), tile(RET_WIDTH),
                 pl.BlockSpec((None, DIFF_WIDTH + POOL_WIDTH + RET_WIDTH, D), lambda i: (layer, 0, 0)),
                 lay.mod_spec(layer, 2), pl.BlockSpec((1, D), const), lay.mod_spec(layer, 3), lay.mod_spec(layer, 4),
                 pl.BlockSpec((D, LANES), const)]
    args += [b, r, w_out, mods3, nw.reshape(1, D), mods3, mods3, rw_pad]
    return pl.pallas_call(
        functools.partial(_outproj_body, lay=lay, with_ctx=a_ctx is not None),
        grid=(n_tok // TM,), in_specs=in_specs,
        out_specs=[tile(D), tile(D + LANES), pl.BlockSpec((N_EXPERTS, TM), lambda i: (0, i))],
        out_shape=[jax.ShapeDtypeStruct((n_tok, D), F32), jax.ShapeDtypeStruct((n_tok, D + LANES), F32),
                   jax.ShapeDtypeStruct((N_EXPERTS, n_tok), F32)],
        compiler_params=_cparams("arbitrary"), name="outproj",
    )(*args)


def _router_body(lt_ref, rb_ref, tri_ref, hx_any, rec_ref, cr_ref, cnt_ref, run_ref):
    del hx_any
    n = lt_ref.shape[1]
    e0, e1, g0, g1 = _route(lt_ref[...], rb_ref[...])
    lo, hi = jnp.minimum(e0, e1), jnp.maximum(e0, e1)
    grp = jnp.floor(lo * (1.0 / EXPERTS_PER_GROUP))
    pa, pb = lo - grp * EXPERTS_PER_GROUP, hi - grp * EXPERTS_PER_GROUP
    cls = grp * PAIRS_PER_GROUP + pa * ((2 * EXPERTS_PER_GROUP - 1) - pa) * 0.5 + (pb - pa - 1.0)

    @pl.when(pl.program_id(0) == 0)
    def _():
        run_ref[...] = jnp.zeros_like(run_ref)

    run = run_ref[:, :1]
    onehot = jnp.where(cls == lax.broadcasted_iota(jnp.int32, (CLASS_ROWS, 1), 0).astype(F32), 1.0, 0.0)
    tri = tri_ref[...]
    ranks = []
    for c0 in range(0, n, TOKEN_TILE):
        oh = onehot[:, c0:c0 + TOKEN_TILE]
        incl = _mm_nt(oh, tri)
        ranks.append(jnp.sum(oh * (incl + run), axis=0, keepdims=True) - 1.0)
        run = run + incl[:, TOKEN_TILE - 1:TOKEN_TILE]
    rank = jnp.concatenate(ranks, axis=1)
    run_ref[...] = jnp.broadcast_to(run, run_ref.shape)
    cnt_ref[...] = jnp.broadcast_to(run, cnt_ref.shape)
    cr_ref[...] = jnp.concatenate([cls, rank, jnp.zeros((6, n), F32)], axis=0)
    rec_ref[...] = jnp.transpose(jnp.concatenate([e0, e1, g0, g1, cls, rank, jnp.zeros((LANES - 6, n), F32)], axis=0))


def _router(n_tok, D, hx, logits_t, rb_col, tri):
    blk = math.gcd(n_tok, ROUTER_BLOCK)
    return pl.pallas_call(
        _router_body,
        grid=(n_tok // blk,),
        in_specs=[pl.BlockSpec((N_EXPERTS, blk), lambda i: (0, i)), pl.BlockSpec((N_EXPERTS, 1), lambda i: (0, 0)),
                  pl.BlockSpec((TOKEN_TILE, TOKEN_TILE), lambda i: (0, 0)), pl.BlockSpec(memory_space=pl.ANY)],
        out_specs=[pl.BlockSpec((blk, LANES), lambda i: (i, D // LANES)), pl.BlockSpec((8, blk), lambda i: (0, i)),
                   pl.BlockSpec((CLASS_ROWS, LANES), lambda i: (0, 0))],
        out_shape=[jax.ShapeDtypeStruct((n_tok, D + LANES), F32), jax.ShapeDtypeStruct((8, n_tok), F32),
                   jax.ShapeDtypeStruct((CLASS_ROWS, LANES), F32)],
        scratch_shapes=[pltpu.VMEM((CLASS_ROWS, LANES), F32)],
        input_output_aliases={3: 0},
        compiler_params=_cparams("arbitrary"), name="router",
    )(logits_t, rb_col, tri, hx)


def _row_copies(idx_ref, hbm, vmem, sem, scatter):
    n = vmem.shape[0]
    def copy(u, row):
        src, dst = (vmem.at[pl.ds(u, 1)], hbm.at[pl.ds(row, 1)])
        return pltpu.make_async_copy(src, dst, sem) if scatter else pltpu.make_async_copy(dst, src, sem)

    def start(u, carry):
        copy(u, idx_ref[0, 0, u]).start()
        return carry

    def wait(u, carry):
        copy(0, 0).wait()
        return carry

    lax.fori_loop(0, n, start, 0, unroll=8)
    lax.fori_loop(0, n, wait, 0, unroll=8)


def _dispatch_body(pend_ref, padded_ref, nvalid_ref, pos_ref, hx_ref, xs_hbm, zbuf, zsem, sem, *, n_blk):
    TB = MOE_TILE
    i = pl.program_id(0)

    @pl.when(i == 0)
    def _():
        zbuf[...] = jnp.zeros_like(zbuf)
        fill = lambda row: pltpu.make_async_copy(zbuf, xs_hbm.at[pl.ds(row, TB)], zsem.at[0])
        for k in range(N_CLASSES):
            @pl.when(padded_ref[k] > 0)
            def _():
                fill(pl.multiple_of(pend_ref[k] - TB, TB)).start()
        for j in range(n_blk):
            @pl.when(j >= nvalid_ref[0])
            def _():
                fill(j * TB).start()
        for k in range(N_CLASSES):
            @pl.when(padded_ref[k] > 0)
            def _():
                fill(0).wait()
        for j in range(n_blk):
            @pl.when(j >= nvalid_ref[0])
            def _():
                fill(0).wait()

    _row_copies(pos_ref, xs_hbm, hx_ref, sem.at[0], True)


def _combine_body(pos_ref, ys_hbm, f_ref, sem):
    _row_copies(pos_ref, ys_hbm, f_ref, sem.at[0], False)


def _moe_body(elo_ref, ehi_ref, nvalid_ref, xs_ref, wg_lo, wu_lo, wd_lo, wg_hi, wu_hi, wd_hi, ys_ref, *, D):
    j = pl.program_id(0)

    @pl.when(j < nvalid_ref[0])
    def _():
        xb = xs_ref[:, :D].astype(MXU_DTYPE)
        info = xs_ref[:, D:]
        first_is_lo = info[:, REC_E0:REC_E0 + 1] <= info[:, REC_E1:REC_E1 + 1]
        g0, g1 = info[:, REC_G0:REC_G0 + 1], info[:, REC_G1:REC_G1 + 1]
        g_lo = jnp.where(first_is_lo, g0, g1)
        g_hi = jnp.where(first_is_lo, g1, g0)

        def ffn(wg, wu, wd):
            hg = jnp.dot(xb, wg[...], preferred_element_type=F32)
            hu = jnp.dot(xb, wu[...], preferred_element_type=F32)
            return _mm((hg * _sigmoid(hg)) * hu, wd[...])

        ys_ref[...] = g_lo * ffn(wg_lo, wu_lo, wd_lo) + g_hi * ffn(wg_hi, wu_hi, wd_hi)

    @pl.when(j >= nvalid_ref[0])
    def _():
        ys_ref[...] = jnp.zeros_like(ys_ref)


def _moe(layer, n_tok, D, hx, cls_rank, counts, wg, wu, wd):
    TB = MOE_TILE
    DE = wg.shape[-1]
    chunk = math.gcd(n_tok, PERM_CHUNK)
    cls = cls_rank[0].astype(jnp.int32)
    rank = cls_rank[1].astype(jnp.int32)
    cnt = counts[:N_CLASSES, 0].astype(jnp.int32)
    padded = (cnt + TB - 1) // TB * TB
    pend = jnp.cumsum(padded)
    pstart = pend - padded
    onehot = cls[:, None] == jnp.arange(N_CLASSES, dtype=jnp.int32)[None, :]
    pos = jnp.sum(jnp.where(onehot, pstart[None, :], 0), axis=1) + rank
    n_blk = -(-n_tok // TB) + N_CLASSES
    P = n_blk * TB
    nvalid = (pend[-1] // TB).astype(jnp.int32).reshape(1)
    blk = jnp.minimum(jnp.arange(n_blk, dtype=jnp.int32), nvalid - 1)
    blk_cls = jnp.minimum(jnp.sum(pend[None, :] <= (blk * TB)[:, None], axis=1), N_CLASSES - 1).astype(jnp.int32)
    pairs = [(u, v) for u in range(EXPERTS_PER_GROUP) for v in range(u + 1, EXPERTS_PER_GROUP)]
    pair_lo = jnp.array([u for u, _ in pairs], jnp.int32)
    pair_hi = jnp.array([v for _, v in pairs], jnp.int32)
    base = (blk_cls // PAIRS_PER_GROUP) * EXPERTS_PER_GROUP
    blk_lo = base + pair_lo[blk_cls % PAIRS_PER_GROUP]
    blk_hi = base + pair_hi[blk_cls % PAIRS_PER_GROUP]
    pos3 = pos.reshape(n_tok // chunk, 1, chunk)
    pos_spec = pl.BlockSpec((1, 1, chunk), lambda i, *_: (i, 0, 0), memory_space=pltpu.SMEM)
    any_spec = pl.BlockSpec(memory_space=pl.ANY)

    xs = pl.pallas_call(
        functools.partial(_dispatch_body, n_blk=n_blk),
        grid_spec=pltpu.PrefetchScalarGridSpec(
            num_scalar_prefetch=3, grid=(n_tok // chunk,),
            in_specs=[pos_spec, pl.BlockSpec((chunk, D + LANES), lambda i, *_: (i, 0))], out_specs=any_spec,
            scratch_shapes=[pltpu.VMEM((TB, D + LANES), F32), pltpu.SemaphoreType.DMA((1,)),
                            pltpu.SemaphoreType.DMA((1,))]),
        out_shape=jax.ShapeDtypeStruct((P, D + LANES), F32),
        compiler_params=_cparams("arbitrary"), name="moe_dispatch",
    )(pend.astype(jnp.int32), padded.astype(jnp.int32), nvalid, pos3, hx)

    w_spec = lambda shape, which: pl.BlockSpec(
        (None, None) + shape,
        (lambda j, elo, ehi, nv: (layer, elo[j], 0, 0)) if which == 0 else (lambda j, elo, ehi, nv: (layer, ehi[j], 0, 0)))
    ys = pl.pallas_call(
        functools.partial(_moe_body, D=D),
        grid_spec=pltpu.PrefetchScalarGridSpec(
            num_scalar_prefetch=3, grid=(n_blk,),
            in_specs=[pl.BlockSpec((TB, D + LANES), lambda j, elo, ehi, nv: (jnp.minimum(j, nv[0] - 1), 0)),
                      w_spec((D, DE), 0), w_spec((D, DE), 0), w_spec((DE, D), 0),
                      w_spec((D, DE), 1), w_spec((D, DE), 1), w_spec((DE, D), 1)],
            out_specs=pl.BlockSpec((TB, D), lambda j, elo, ehi, nv: (j, 0))),
        out_shape=jax.ShapeDtypeStruct((P, D), F32),
        compiler_params=_cparams("arbitrary"), name="moe_experts",
    )(blk_lo, blk_hi, nvalid, xs, wg, wu, wd, wg, wu, wd)

    return pl.pallas_call(
        _combine_body,
        grid_spec=pltpu.PrefetchScalarGridSpec(
            num_scalar_prefetch=0, grid=(n_tok // chunk,), in_specs=[pos_spec, any_spec],
            out_specs=pl.BlockSpec((chunk, D), lambda i: (i, 0)),
            scratch_shapes=[pltpu.SemaphoreType.DMA((1,))]),
        out_shape=jax.ShapeDtypeStruct((n_tok, D), F32),
        compiler_params=_cparams("arbitrary"), name="moe_combine",
    )(pos3, ys)


def _final_body(x_ref, f_ref, g2_ref, nw_ref, o_ref):
    o_ref[...] = _rms(x_ref[...] + g2_ref[0] * f_ref[...]) * nw_ref[...]


def _final(lay, layer, x, f, mods3, nw):
    D, TM = lay.D, TOKEN_TILE
    row = lambda i: (i, 0)
    return pl.pallas_call(
        _final_body, grid=(lay.NL // TM,),
        in_specs=[pl.BlockSpec((TM, D), row), pl.BlockSpec((TM, D), row), lay.mod_spec(layer, 5),
                  pl.BlockSpec((1, D), lambda i: (0, 0))],
        out_specs=pl.BlockSpec((TM, D), row),
        out_shape=jax.ShapeDtypeStruct((lay.NL, D), F32),
        compiler_params=_cparams("arbitrary"), name="final_norm",
    )(x, f, mods3, nw.reshape(1, D))


def kernel(x, c, ctx, c_ctx, w_in, w_out, ada_w, ada_b, norm1_w, norm2_w, diff_lambda, diff_norm_w, pool_w,
           pool_scale, ret_a_f, ret_a_b, ret_norm_w, router_w, router_b, moe_w_gate, moe_w_up, moe_w_down,
           final_norm_w):
    B, L, D = x.shape
    Lc = ctx.shape[1]
    depth = w_in.shape[0]
    lay = _Layout(B, L, Lc, D)
    assert B + 1 <= MOD_ROWS

    xa = (x.reshape(B * L, D), ctx.reshape(B * Lc, D))
    c_all = jnp.zeros((MOD_ROWS, D), F32).at[:B].set(c).at[B].set(c_ctx)
    mods3 = _ada_table(c_all, ada_w, ada_b).reshape(depth * MOD_ROWS * N_MOD, 1, D)
    tables = _rope_tables(lay)
    rw_pad = jnp.zeros((D, LANES), MXU_DTYPE).at[:, :N_EXPERTS].set(router_w.astype(MXU_DTYPE))
    rb_col = router_b.astype(F32).reshape(N_EXPERTS, 1)
    tri = jnp.tril(jnp.ones((TOKEN_TILE, TOKEN_TILE), F32)).astype(MXU_DTYPE)
    w_in_b, w_out_b = w_in.astype(MXU_DTYPE), w_out.astype(MXU_DTYPE)
    wg_b, wu_b, wd_b = moe_w_gate.astype(MXU_DTYPE), moe_w_up.astype(MXU_DTYPE), moe_w_down.astype(MXU_DTYPE)

    f = None
    for l in range(depth):
        last = l == depth - 1
        lam_init = 0.8 - 0.6 * math.exp(-0.3 * l)
        dl = diff_lambda[l].astype(F32)
        lam = (jnp.exp(jnp.sum(dl[0] * dl[1])) - jnp.exp(jnp.sum(dl[2] * dl[3])) + lam_init).reshape(1)
        p, xa = _inproj(lay, l, xa, f, mods3, norm1_w[l], w_in_b, tables)
        a_lat, a_ctx = _diff_attention(lay, p, lam, diff_norm_w[l], 1.0 - lam_init, not last)
        eye = jnp.eye(len(POOL_WINDOWS), dtype=F32)
        w_bd = (eye[:, None, :, None] * pool_w[l][:, :, None, :]).reshape(POOL_WIDTH, POOL_WIDTH).astype(MXU_DTYPE)
        bp = _pool(lay, p, w_bd, pool_scale[l])
        lg_f = -jnp.exp(ret_a_f[l].astype(F32))
        lg_b = -jnp.exp(ret_a_b[l].astype(F32))
        lanes = lambda lg: jnp.repeat(lg, HEAD_DIM).reshape(1, RET_WIDTH)
        yf = _retention_pass(lay, p, lg_f, lanes(lg_f), False)
        r = _retention_pass(lay, p, lg_b, lanes(lg_b), True, (yf, ret_norm_w[l]))
        n_tok = lay.NL if last else lay.T
        xa, hx, logits_t = _outproj(lay, l, n_tok, xa, a_lat, a_ctx, bp, r, w_out_b, mods3, norm2_w[l], rw_pad)
        hx, cls_rank, counts = _router(n_tok, D, hx, logits_t, rb_col, tri)
        f = _moe(l, n_tok, D, hx, cls_rank, counts, wg_b, wu_b, wd_b)
    out = _final(lay, depth - 1, xa, f, mods3, final_norm_w)
    return out.reshape(B, L, D)
```

```python
import functools
import math

import jax
import jax.numpy as jnp
from jax import lax
from jax.experimental import pallas as pl
from jax.experimental.pallas import tpu as pltpu

F32 = jnp.float32
MXU_DTYPE = jnp.bfloat16
ACT_DTYPE = jnp.bfloat16

GRID_W = 64
HEAD_DIM = 64
DIFF_HEADS = 4
DIFF_V_DIM = 2 * HEAD_DIM
DIFF_WIDTH = DIFF_HEADS * DIFF_V_DIM
POOL_WINDOWS = (2, 4, 8, 16)
POOL_GROUP = 64
POOL_WIDTH = POOL_GROUP * len(POOL_WINDOWS)
RET_HEADS = 4
RET_WIDTH = RET_HEADS * HEAD_DIM
OFF_DQ = 0
OFF_DK = OFF_DQ + DIFF_WIDTH
OFF_DV = OFF_DK + DIFF_WIDTH
OFF_PU = OFF_DV + DIFF_WIDTH
OFF_RQ = OFF_PU + POOL_WIDTH
OFF_RK = OFF_RQ + RET_WIDTH
OFF_RV = OFF_RK + RET_WIDTH
OFF_RG = OFF_RV + RET_WIDTH
IN_WIDTH = OFF_RG + RET_WIDTH
ROPE_BASE = 10000.0
ROPE_FREQS = HEAD_DIM // 4
N_EXPERTS = 16
N_GROUPS = 4
EXPERTS_PER_GROUP = N_EXPERTS // N_GROUPS
PAIRS_PER_GROUP = EXPERTS_PER_GROUP * (EXPERTS_PER_GROUP - 1) // 2
N_CLASSES = N_GROUPS * PAIRS_PER_GROUP
N_MOD = 6
EPS = 1e-6

LANES = 128
MOD_ROWS = 16
TOKEN_TILE = 512
SEQ_TILE = 256
ATTN_SUBTILES = 2
ATTN_SUB_ROWS = 256
ATTN_KEY_CHUNK = 512
HALO = 16
MOE_TILE = 256
COL = 256
ROUTER_BLOCK = 2048
CLASS_ROWS = 32
PERM_CHUNK = 2048
REC_E0, REC_E1, REC_G0, REC_G1, REC_CLS, REC_RANK = range(6)
VMEM_LIMIT = 56 * 1024 * 1024


def _mm(a, b):
    return jnp.dot(a.astype(MXU_DTYPE), b.astype(MXU_DTYPE), preferred_element_type=F32)


def _mm_nt(a, b):
    return lax.dot_general(a.astype(MXU_DTYPE), b.astype(MXU_DTYPE), (((1,), (1,)), ((), ())),
                           preferred_element_type=F32)


def _mm_tn(a, b):
    return lax.dot_general(a.astype(MXU_DTYPE), b.astype(MXU_DTYPE), (((0,), (0,)), ((), ())),
                           preferred_element_type=F32)


def _split(x):
    hi = x.astype(MXU_DTYPE)
    lo = (x - hi.astype(F32)).astype(MXU_DTYPE)
    return hi, lo


def _mm_hi(a, b):
    ah, al = _split(a)
    bh, bl = _split(b)
    d = lambda u, v: jnp.dot(u, v, preferred_element_type=F32)
    return d(ah, bh) + (d(ah, bl) + d(al, bh))


def _sigmoid(x):
    return 1.0 / (1.0 + jnp.exp(-x))


def _rms(x):
    return x * lax.rsqrt(jnp.mean(x * x, axis=-1, keepdims=True) + EPS)


def _cparams(*sem):
    return pltpu.CompilerParams(dimension_semantics=sem, vmem_limit_bytes=VMEM_LIMIT)


def _ada_body(c_ref, w_ref, b_ref, o_ref):
    c = c_ref[...]
    o_ref[...] = _mm_hi(c * _sigmoid(c), w_ref[...]) + b_ref[...]


def _ada_table(c_all, ada_w, ada_b):
    depth, d, n = ada_w.shape
    nb = n // N_MOD
    return pl.pallas_call(
        _ada_body,
        grid=(depth, n // nb),
        in_specs=[pl.BlockSpec((MOD_ROWS, d), lambda l, j: (0, 0)),
                  pl.BlockSpec((None, d, nb), lambda l, j: (l, 0, j)),
                  pl.BlockSpec((None, 1, nb), lambda l, j: (l, 0, j))],
        out_specs=pl.BlockSpec((None, MOD_ROWS, nb), lambda l, j: (l, 0, j)),
        out_shape=jax.ShapeDtypeStruct((depth, MOD_ROWS, n), F32),
        compiler_params=_cparams("arbitrary", "arbitrary"),
        name="ada_table",
    )(c_all, ada_w, ada_b.reshape(depth, 1, n))


class _Layout:
    def __init__(self, B, L, Lc, D):
        self.B, self.L, self.Lc, self.D = B, L, Lc, D
        self.NL, self.NC = B * L, B * Lc
        self.T = self.NL + self.NC
        assert L % TOKEN_TILE == 0 and self.NC % TOKEN_TILE == 0
        assert L % SEQ_TILE == 0 and Lc == SEQ_TILE and L % GRID_W == 0
        self.n_lat_tiles = self.NL // TOKEN_TILE
        self.tiles_per_seq = L // TOKEN_TILE

    def mod_row(self, i):
        return jnp.where(i < self.n_lat_tiles, i // self.tiles_per_seq, self.B)

    def rope_block(self, i):
        return jnp.where(i < self.n_lat_tiles, i % self.tiles_per_seq, self.tiles_per_seq)

    def mod_spec(self, layer, k):
        base = layer * MOD_ROWS * N_MOD + k
        return pl.BlockSpec((1, 1, self.D), lambda i: (base + self.mod_row(i) * N_MOD, 0, 0))


def _rope_tables(lay):
    L = lay.L
    rows = L // GRID_W
    row = jnp.repeat(jnp.arange(rows), GRID_W).astype(F32)
    col = jnp.tile(jnp.arange(GRID_W), rows).astype(F32)
    inv = ROPE_BASE ** (-jnp.arange(ROPE_FREQS, dtype=F32) / ROPE_FREQS)
    ang_r = row[:, None] * inv
    ang_c = col[:, None] * inv
    ang = jnp.concatenate([ang_r, ang_r, ang_c, ang_c], axis=-1)
    cos, sin = jnp.cos(ang), jnp.sin(ang)
    first_half = (jnp.arange(HEAD_DIM) % (2 * ROPE_FREQS)) < ROPE_FREQS
    sin_a = jnp.where(first_half, -sin, 0.0)
    sin_b = jnp.where(first_half, 0.0, sin)
    rep = LANES // HEAD_DIM
    ident = [jnp.ones((TOKEN_TILE, LANES), F32), jnp.zeros((TOKEN_TILE, LANES), F32),
             jnp.zeros((TOKEN_TILE, LANES), F32)]
    return [jnp.concatenate([jnp.tile(t, (1, rep)), e], axis=0) for t, e in zip((cos, sin_a, sin_b), ident)]


_ROPE_COLS = tuple(range(OFF_DQ // COL, OFF_DV // COL)) + (OFF_RQ // COL, OFF_RK // COL)
LOG2E = 1.4426950408889634
_COL_SCALE = {j: HEAD_DIM ** -0.5 * LOG2E for j in range(OFF_DQ // COL, OFF_DK // COL)}
_COL_SCALE[OFF_RK // COL] = HEAD_DIM ** -0.5


def _inproj_body(*refs, with_f, n_lat_tiles):
    if with_f:
        (x_ref, f_ref, g2_ref, nw_ref, sh_ref, sc_ref, w_ref, cos_ref, sa_ref, sb_ref, p_ref, xo_ref) = refs
        x = x_ref[...] + g2_ref[0] * f_ref[...]
    else:
        (xl_ref, xc_ref, nw_ref, sh_ref, sc_ref, w_ref, cos_ref, sa_ref, sb_ref, p_ref, xo_ref) = refs
        x = jnp.where(pl.program_id(0) < n_lat_tiles, xl_ref[...], xc_ref[...])
    xo_ref[...] = x
    h = (_rms(x) * nw_ref[...]) * (1.0 + sc_ref[0]) + sh_ref[0]
    hb = h.astype(MXU_DTYPE)
    cos, sa, sb = cos_ref[...], sa_ref[...], sb_ref[...]
    for j in range(IN_WIDTH // COL):
        acc = jnp.dot(hb, w_ref[:, j * COL:(j + 1) * COL], preferred_element_type=F32)
        if j in _ROPE_COLS:
            parts = []
            for t in range(COL // LANES):
                a = acc[:, t * LANES:(t + 1) * LANES]
                parts.append(a * cos + pltpu.roll(a, LANES - ROPE_FREQS, 1) * sa + pltpu.roll(a, ROPE_FREQS, 1) * sb)
            acc = jnp.concatenate(parts, axis=1)
        if j in _COL_SCALE:
            acc = acc * _COL_SCALE[j]
        p_ref[:, j * COL:(j + 1) * COL] = acc.astype(p_ref.dtype)


def _inproj(lay, layer, x, f, mods3, nw, w_in, tables):
    D, TM = lay.D, TOKEN_TILE
    row = lambda i: (i, 0)
    const = lambda i: (0, 0)
    n_lat = lay.n_lat_tiles
    tbl = pl.BlockSpec((TM, LANES), lambda i: (lay.rope_block(i), 0))
    if f is not None:
        in_specs = [pl.BlockSpec((TM, D), row), pl.BlockSpec((TM, D), row), lay.mod_spec(layer - 1, 5)]
        args = [x, f, mods3]
    else:
        in_specs = [pl.BlockSpec((TM, D), lambda i: (jnp.minimum(i, n_lat - 1), 0)),
                    pl.BlockSpec((TM, D), lambda i: (jnp.maximum(i - n_lat, 0), 0))]
        args = list(x)
    in_specs += [pl.BlockSpec((1, D), const), lay.mod_spec(layer, 0), lay.mod_spec(layer, 1),
                 pl.BlockSpec((None, D, IN_WIDTH), lambda i: (layer, 0, 0)), tbl, tbl, tbl]
    args += [nw.reshape(1, D), mods3, mods3, w_in] + list(tables)
    return pl.pallas_call(
        functools.partial(_inproj_body, with_f=f is not None, n_lat_tiles=n_lat),
        grid=(lay.T // TM,), in_specs=in_specs,
        out_specs=[pl.BlockSpec((TM, IN_WIDTH), row), pl.BlockSpec((TM, D), row)],
        out_shape=[jax.ShapeDtypeStruct((lay.T, IN_WIDTH), ACT_DTYPE), jax.ShapeDtypeStruct((lay.T, D), F32)],
        compiler_params=_cparams("arbitrary"), name="inproj",
    )(*args)


def _map_masks(q):
    lane = lax.broadcasted_iota(jnp.int32, (1, DIFF_V_DIM), 1)
    zero = jnp.zeros_like(q)
    return [jnp.where(lane < HEAD_DIM, q, zero), jnp.where(lane >= HEAD_DIM, q, zero)]


def _scores(qm, k_refs):
    chunks, run = [], None
    for k in k_refs:
        for c0 in range(0, k.shape[0], ATTN_KEY_CHUNK):
            s = _mm_nt(qm, k[c0:c0 + ATTN_KEY_CHUNK, :])
            chunks.append(s)
            for t in range(s.shape[1] // LANES):
                piece = s[:, t * LANES:(t + 1) * LANES]
                run = piece if run is None else jnp.maximum(run, piece)
    return chunks, jnp.max(run, axis=-1, keepdims=True)


def _softmax_diff(maps, lam):
    es, ls = [], []
    for chunks, mx in maps:
        e = [jnp.exp2(t - mx) for t in chunks]
        ls.append(functools.reduce(jnp.add, [jnp.sum(t, axis=-1, keepdims=True) for t in e]))
        es.append([t.astype(MXU_DTYPE) for t in e])
    c = (lam * ls[0] / ls[1]).astype(MXU_DTYPE)
    return [e1 - e2 * c for e1, e2 in zip(es[0], es[1])], 1.0 / ls[0]


def _head_norm(o, w, post_scale):
    return (_rms(o) * w) * post_scale


def _attn_body(lam_ref, q_ref, *refs, n_seg, n_sub, post_scale):
    k_refs, v_refs = refs[:n_seg], refs[n_seg:2 * n_seg]
    w_ref, o_ref = refs[2 * n_seg], refs[2 * n_seg + 1]
    rows = q_ref.shape[0] // n_sub

    def finish(u, maps):
        p, r1 = _softmax_diff(maps, lam_ref[0])
        o, at = None, 0
        for v in v_refs:
            n_c = -(-v.shape[0] // ATTN_KEY_CHUNK)
            part = jnp.dot(jnp.concatenate(p[at:at + n_c], axis=1), v[...], preferred_element_type=F32)
            o = part if o is None else o + part
            at += n_c
        o_ref[u * rows:(u + 1) * rows, :] = _head_norm(o * r1, w_ref[...], post_scale).astype(o_ref.dtype)

    scores = [[_scores(qm, k_refs) for qm in _map_masks(q_ref[u * rows:(u + 1) * rows, :])] for u in range(n_sub)]
    for u, maps in enumerate(scores):
        finish(u, maps)


def _diff_attention(lay, p, lam, norm_w, post_scale, need_ctx):
    B, L, Lc, H = lay.B, lay.L, lay.Lc, DIFF_HEADS
    W = DIFF_V_DIM
    tq = ATTN_SUBTILES * ATTN_SUB_ROWS
    assert L % tq == 0
    nq = L // tq
    k_blk, v_blk = OFF_DK // W, OFF_DV // W
    ctx0 = lay.NL // Lc
    lam_spec = pl.BlockSpec(memory_space=pltpu.SMEM)
    w_spec = pl.BlockSpec((1, W), lambda *_: (0, 0))
    a_lat = pl.pallas_call(
        functools.partial(_attn_body, n_seg=2, n_sub=ATTN_SUBTILES, post_scale=post_scale),
        grid=(B, H, nq),
        in_specs=[lam_spec,
                  pl.BlockSpec((tq, W), lambda b, h, i: (b * nq + i, h)),
                  pl.BlockSpec((Lc, W), lambda b, h, i: (ctx0 + b, k_blk + h)),
                  pl.BlockSpec((L, W), lambda b, h, i: (b, k_blk + h)),
                  pl.BlockSpec((Lc, W), lambda b, h, i: (ctx0 + b, v_blk + h)),
                  pl.BlockSpec((L, W), lambda b, h, i: (b, v_blk + h)),
                  w_spec],
        out_specs=pl.BlockSpec((tq, W), lambda b, h, i: (b * nq + i, h)),
        out_shape=jax.ShapeDtypeStruct((lay.NL, DIFF_WIDTH), ACT_DTYPE),
        compiler_params=_cparams("arbitrary", "arbitrary", "arbitrary"),
        name="diff_attn",
    )(lam, p, p, p, p, p, norm_w.reshape(1, W))
    if not need_ctx:
        return a_lat, None
    a_ctx = pl.pallas_call(
        functools.partial(_attn_body, n_seg=1, n_sub=1, post_scale=post_scale),
        grid=(B, H),
        in_specs=[lam_spec,
                  pl.BlockSpec((Lc, W), lambda b, h: (ctx0 + b, h)),
                  pl.BlockSpec((Lc, W), lambda b, h: (ctx0 + b, k_blk + h)),
                  pl.BlockSpec((Lc, W), lambda b, h: (ctx0 + b, v_blk + h)),
                  w_spec],
        out_specs=pl.BlockSpec((Lc, W), lambda b, h: (b, h)),
        out_shape=jax.ShapeDtypeStruct((lay.NC, DIFF_WIDTH), ACT_DTYPE),
        compiler_params=_cparams("arbitrary", "arbitrary"),
        name="diff_attn_ctx",
    )(lam, p, p, p, norm_w.reshape(1, W))
    return a_lat, a_ctx


def _pool_body(prev_ref, cur_ref, next_ref, w_ref, scale_ref, o_ref, *, lay):
    i = pl.program_id(0)
    n_lat = lay.NL // SEQ_TILE
    per_seq = jnp.where(i < n_lat, lay.L // SEQ_TILE, lay.Lc // SEQ_TILE)
    idx = jnp.where(i < n_lat, i, i - n_lat) % per_seq
    seq_len = per_seq * SEQ_TILE
    has_prev = (idx > 0).astype(F32)
    has_next = (idx < per_seq - 1).astype(F32)
    cur = cur_ref[...].astype(F32)
    u = jnp.concatenate([prev_ref[...].astype(F32) * has_prev, cur, next_ref[...].astype(F32) * has_next], axis=0)
    sums = {1: u}
    w = 1
    while w < POOL_WINDOWS[-1]:
        s = sums[w]
        m = s.shape[0] - w
        sums[2 * w] = s[:m] + s[w:w + m]
        w *= 2
    pos = idx * SEQ_TILE + lax.broadcasted_iota(jnp.int32, (SEQ_TILE, 1), 0)
    group = lax.broadcasted_iota(jnp.int32, (1, POOL_WIDTH), 1) // POOL_GROUP
    mean = jnp.zeros((SEQ_TILE, POOL_WIDTH), F32)
    for g, win in enumerate(POOL_WINDOWS):
        start = HALO - win // 2
        cnt = jnp.minimum(pos + (win - win // 2), seq_len) - jnp.maximum(pos - win // 2, 0)
        mean = jnp.where(group == g, sums[win][start:start + SEQ_TILE] / cnt.astype(F32), mean)
    o_ref[...] = (_mm(mean - cur, w_ref[...]) * scale_ref[...]).astype(o_ref.dtype)


def _pool(lay, p, w_bd, scale):
    n = lay.T // SEQ_TILE
    col = OFF_PU // POOL_WIDTH
    per = SEQ_TILE // HALO
    last = lay.T // HALO - 1
    return pl.pallas_call(
        functools.partial(_pool_body, lay=lay),
        grid=(n,),
        in_specs=[pl.BlockSpec((HALO, POOL_WIDTH), lambda i: (jnp.maximum(i * per - 1, 0), col)),
                  pl.BlockSpec((SEQ_TILE, POOL_WIDTH), lambda i: (i, col)),
                  pl.BlockSpec((HALO, POOL_WIDTH), lambda i: (jnp.minimum((i + 1) * per, last), col)),
                  pl.BlockSpec((POOL_WIDTH, POOL_WIDTH), lambda i: (0, 0)),
                  pl.BlockSpec((1, POOL_WIDTH), lambda i: (0, 0))],
        out_specs=pl.BlockSpec((SEQ_TILE, POOL_WIDTH), lambda i: (i, 0)),
        out_shape=jax.ShapeDtypeStruct((lay.T, POOL_WIDTH), ACT_DTYPE),
        compiler_params=_cparams("arbitrary"), name="pool",
    )(p, p, p, w_bd, scale.reshape(1, POOL_WIDTH))


def _ret_tables(lgs_ref, lgv, d_ref, xi_ref, zeta_ref, reverse):
    C = SEQ_TILE
    ri = lax.broadcasted_iota(jnp.int32, (C, C), 0)
    ci = lax.broadcasted_iota(jnp.int32, (C, C), 1)
    dist = (ci - ri if reverse else ri - ci).astype(F32)
    keep = dist > 0 if reverse else dist >= 0
    for h in range(RET_HEADS):
        d_ref[h * C:(h + 1) * C, :] = jnp.where(keep, jnp.exp(lgs_ref[h] * jnp.maximum(dist, 0.0)), 0.0)
    t = lax.broadcasted_iota(jnp.int32, (C, 1), 0).astype(F32)
    xi_ref[...] = jnp.exp(lgv * ((C - t) if reverse else (t + 1.0)))
    zeta_ref[...] = jnp.exp(lgv * (t if reverse else (C - 1.0 - t)))


def _ret_chunk(q, k, v, lgv, s_ref, d_ref, xi_ref, zeta_ref):
    C = SEQ_TILE
    lane_head = lax.broadcasted_iota(jnp.int32, (1, RET_WIDTH), 1) // HEAD_DIM
    row_head = lax.broadcasted_iota(jnp.int32, (RET_WIDTH, 1), 0) // HEAD_DIM
    zero = jnp.zeros_like(q)
    qs = jnp.concatenate([jnp.where(lane_head == h, q, zero) for h in range(RET_HEADS)], axis=0)
    sd = _mm_nt(qs, k) * d_ref[...]
    yv = _mm(sd, v)
    y = functools.reduce(jnp.add, [jnp.where(lane_head == h, yv[h * C:(h + 1) * C], 0.0) for h in range(RET_HEADS)])
    state = s_ref[...]
    y = y + _mm(q.astype(F32) * xi_ref[...], state)
    kv = _mm_tn(k.astype(F32) * zeta_ref[...], v)
    s_ref[...] = state * jnp.exp(lgv * C) + jnp.where(row_head == lane_head, kv, 0.0)
    return y


def _ret_body(lgs_f, lgv_f, lgs_b, lgv_b, qf, kf, vf, qb, kb, vb, yf_ref, yb_ref,
              s_f, d_f, xi_f, zeta_f, s_b, d_b, xi_b, zeta_b):
    b, j = pl.program_id(0), pl.program_id(1)

    @pl.when((b == 0) & (j == 0))
    def _():
        _ret_tables(lgs_f, lgv_f[...], d_f, xi_f, zeta_f, False)
        _ret_tables(lgs_b, lgv_b[...], d_b, xi_b, zeta_b, True)

    @pl.when(j == 0)
    def _():
        s_f[...] = jnp.zeros_like(s_f)
        s_b[...] = jnp.zeros_like(s_b)

    yf_ref[...] = _ret_chunk(qf[...], kf[...], vf[...], lgv_f[...], s_f, d_f, xi_f, zeta_f)
    yb_ref[...] = _ret_chunk(qb[...], kb[...], vb[...], lgv_b[...], s_b, d_b, xi_b, zeta_b)


def _retention(lay, p, lgs_f, lgv_f, lgs_b, lgv_b):
    B, L = lay.B, lay.L
    C, W = SEQ_TILE, RET_WIDTH
    nch = L // C
    ctx0 = lay.NL // C
    rows_f = lambda b, j: jnp.where(j == 0, ctx0 + b, b * nch + j - 1)
    rows_b = lambda b, j: jnp.where(j == 0, ctx0 + b, b * nch + nch - j)
    spec = lambda rows, c: pl.BlockSpec((C, W), lambda b, j: (rows(b, j), c))
    smem = pl.BlockSpec(memory_space=pltpu.SMEM)
    vec = pl.BlockSpec((1, W), lambda b, j: (0, 0))
    cols = (OFF_RQ // W, OFF_RK // W, OFF_RV // W)
    tables = [pltpu.VMEM((W, W), F32), pltpu.VMEM((RET_HEADS * C, C), F32), pltpu.VMEM((C, W), F32), pltpu.VMEM((C, W), F32)]
    return pl.pallas_call(
        _ret_body,
        grid=(B, nch + 1),
        in_specs=[smem, vec, smem, vec] + [spec(rows_f, c) for c in cols] + [spec(rows_b, c) for c in cols],
        out_specs=[spec(rows_f, 0), spec(rows_b, 0)],
        out_shape=[jax.ShapeDtypeStruct((lay.T, W), F32), jax.ShapeDtypeStruct((lay.T, W), F32)],
        scratch_shapes=tables + tables,
        compiler_params=_cparams("arbitrary", "arbitrary"),
        name="retention",
    )(lgs_f, lgv_f, lgs_b, lgv_b, p, p, p, p, p, p)


def _retention_output(yf, yb, gate, w):
    y = yf + yb
    lane_head = lax.broadcasted_iota(jnp.int32, (1, RET_WIDTH), 1) // HEAD_DIM
    row_head = lax.broadcasted_iota(jnp.int32, (RET_WIDTH, 1), 0) // HEAD_DIM
    same = (row_head == lane_head).astype(F32) * (1.0 / HEAD_DIM)
    ms = _mm_hi(y * y, same)
    g = gate.astype(F32)
    return (g * _sigmoid(g)) * (y * lax.rsqrt(ms + EPS) * w)


def _top2(vals):
    n = len(vals)
    v1 = functools.reduce(jnp.maximum, vals)
    i1 = jnp.full_like(v1, n - 1)
    for e in range(n - 2, -1, -1):
        i1 = jnp.where(vals[e] == v1, float(e), i1)
    rest = [jnp.where(i1 == float(e), -jnp.inf, vals[e]) for e in range(n)]
    v2 = functools.reduce(jnp.maximum, rest)
    i2 = jnp.full_like(v1, n - 1)
    for e in range(n - 2, -1, -1):
        i2 = jnp.where(rest[e] == v2, float(e), i2)
    return v1, i1, v2, i2


def _route(logits_t, bias):
    s = _sigmoid(logits_t)
    sel = s + bias
    groups = []
    for g in range(N_GROUPS):
        rows = [sel[g * EXPERTS_PER_GROUP + e:g * EXPERTS_PER_GROUP + e + 1] for e in range(EXPERTS_PER_GROUP)]
        groups.append(_top2(rows))
    score = [v1 + v2 for v1, _, v2, _ in groups]
    best = functools.reduce(jnp.maximum, score)
    gi = jnp.full_like(best, N_GROUPS - 1)
    for g in range(N_GROUPS - 2, -1, -1):
        gi = jnp.where(score[g] == best, float(g), gi)
    pick = lambda k: functools.reduce(
        lambda acc, g: jnp.where(gi == float(g), groups[g][k], acc), range(N_GROUPS - 1), groups[N_GROUPS - 1][k])
    e0 = gi * EXPERTS_PER_GROUP + pick(1)
    e1 = gi * EXPERTS_PER_GROUP + pick(3)
    s0 = jnp.zeros_like(best)
    s1 = jnp.zeros_like(best)
    for e in range(N_EXPERTS):
        s0 = jnp.where(e0 == float(e), s[e:e + 1], s0)
        s1 = jnp.where(e1 == float(e), s[e:e + 1], s1)
    tot = s0 + s1
    return e0, e1, s0 / tot, s1 / tot


def _outproj_body(*refs, lay, with_ctx):
    if with_ctx:
        (x_ref, al_ref, ac_ref, b_ref, yf_ref, yb_ref, rg_ref, rnw_ref, w_ref, g1_ref, nw_ref, sh_ref, sc_ref, rw_ref,
         xo_ref, hx_ref, lt_ref) = refs
        a = jnp.where(pl.program_id(0) < lay.n_lat_tiles, al_ref[...], ac_ref[...])
    else:
        (x_ref, al_ref, b_ref, yf_ref, yb_ref, rg_ref, rnw_ref, w_ref, g1_ref, nw_ref, sh_ref, sc_ref, rw_ref,
         xo_ref, hx_ref, lt_ref) = refs
        a = al_ref[...]
    D = lay.D
    r = _retention_output(yf_ref[...], yb_ref[...], rg_ref[...], rnw_ref[...])
    y = (_mm(a, w_ref[:DIFF_WIDTH, :])
         + _mm(b_ref[...], w_ref[DIFF_WIDTH:DIFF_WIDTH + POOL_WIDTH, :])
         + _mm(r, w_ref[DIFF_WIDTH + POOL_WIDTH:, :]))
    x = x_ref[...] + g1_ref[0] * y
    xo_ref[...] = x
    h = (_rms(x) * nw_ref[...]) * (1.0 + sc_ref[0]) + sh_ref[0]
    hx_ref[:, :D] = h
    hx_ref[:, D:] = jnp.zeros((TOKEN_TILE, LANES), F32)
    lt_ref[...] = jnp.transpose(_mm(h, rw_ref[...]))[:N_EXPERTS]


def _outproj(lay, layer, n_tok, x, a_lat, a_ctx, b, yf, yb, p, ret_nw, w_out, mods3, nw, rw_pad):
    D, TM = lay.D, TOKEN_TILE
    row = lambda i: (i, 0)
    const = lambda i: (0, 0)
    tile = lambda w: pl.BlockSpec((TM, w), row)
    n_lat = lay.n_lat_tiles
    in_specs = [tile(D), pl.BlockSpec((TM, DIFF_WIDTH), lambda i: (jnp.minimum(i, n_lat - 1), 0))]
    args = [x, a_lat]
    if a_ctx is not None:
        in_specs.append(pl.BlockSpec((TM, DIFF_WIDTH), lambda i: (jnp.maximum(i - n_lat, 0), 0)))
        args.append(a_ctx)
    in_specs += [tile(POOL_WIDTH), tile(RET_WIDTH), tile(RET_WIDTH),
                 pl.BlockSpec((TM, RET_WIDTH), lambda i: (i, OFF_RG // RET_WIDTH)), pl.BlockSpec((1, RET_WIDTH), const),
                 pl.BlockSpec((None, DIFF_WIDTH + POOL_WIDTH + RET_WIDTH, D), lambda i: (layer, 0, 0)),
                 lay.mod_spec(layer, 2), pl.BlockSpec((1, D), const), lay.mod_spec(layer, 3), lay.mod_spec(layer, 4),
                 pl.BlockSpec((D, LANES), const)]
    args += [b, yf, yb, p, ret_nw.reshape(1, RET_WIDTH), w_out, mods3, nw.reshape(1, D), mods3, mods3, rw_pad]
    return pl.pallas_call(
        functools.partial(_outproj_body, lay=lay, with_ctx=a_ctx is not None),
        grid=(n_tok // TM,), in_specs=in_specs,
        out_specs=[tile(D), tile(D + LANES), pl.BlockSpec((N_EXPERTS, TM), lambda i: (0, i))],
        out_shape=[jax.ShapeDtypeStruct((n_tok, D), F32), jax.ShapeDtypeStruct((n_tok, D + LANES), F32),
                   jax.ShapeDtypeStruct((N_EXPERTS, n_tok), F32)],
        compiler_params=_cparams("arbitrary"), name="outproj",
    )(*args)


def _router_body(lt_ref, rb_ref, tri_ref, hx_any, rec_ref, cr_ref, cnt_ref, run_ref):
    del hx_any
    n = lt_ref.shape[1]
    e0, e1, g0, g1 = _route(lt_ref[...], rb_ref[...])
    lo, hi = jnp.minimum(e0, e1), jnp.maximum(e0, e1)
    grp = jnp.floor(lo * (1.0 / EXPERTS_PER_GROUP))
    pa, pb = lo - grp * EXPERTS_PER_GROUP, hi - grp * EXPERTS_PER_GROUP
    cls = grp * PAIRS_PER_GROUP + pa * ((2 * EXPERTS_PER_GROUP - 1) - pa) * 0.5 + (pb - pa - 1.0)

    @pl.when(pl.program_id(0) == 0)
    def _():
        run_ref[...] = jnp.zeros_like(run_ref)

    run = run_ref[:, :1]
    onehot = jnp.where(cls == lax.broadcasted_iota(jnp.int32, (CLASS_ROWS, 1), 0).astype(F32), 1.0, 0.0)
    tri = tri_ref[...]
    ranks = []
    for c0 in range(0, n, TOKEN_TILE):
        oh = onehot[:, c0:c0 + TOKEN_TILE]
        incl = _mm_nt(oh, tri)
        ranks.append(jnp.sum(oh * (incl + run), axis=0, keepdims=True) - 1.0)
        run = run + incl[:, TOKEN_TILE - 1:TOKEN_TILE]
    rank = jnp.concatenate(ranks, axis=1)
    run_ref[...] = jnp.broadcast_to(run, run_ref.shape)
    cnt_ref[...] = jnp.broadcast_to(run, cnt_ref.shape)
    cr_ref[...] = jnp.concatenate([cls, rank, jnp.zeros((6, n), F32)], axis=0)
    rec_ref[...] = jnp.transpose(jnp.concatenate([e0, e1, g0, g1, cls, rank, jnp.zeros((LANES - 6, n), F32)], axis=0))


def _router(n_tok, D, hx, logits_t, rb_col, tri):
    blk = math.gcd(n_tok, ROUTER_BLOCK)
    return pl.pallas_call(
        _router_body,
        grid=(n_tok // blk,),
        in_specs=[pl.BlockSpec((N_EXPERTS, blk), lambda i: (0, i)), pl.BlockSpec((N_EXPERTS, 1), lambda i: (0, 0)),
                  pl.BlockSpec((TOKEN_TILE, TOKEN_TILE), lambda i: (0, 0)), pl.BlockSpec(memory_space=pl.ANY)],
        out_specs=[pl.BlockSpec((blk, LANES), lambda i: (i, D // LANES)), pl.BlockSpec((8, blk), lambda i: (0, i)),
                   pl.BlockSpec((CLASS_ROWS, LANES), lambda i: (0, 0))],
        out_shape=[jax.ShapeDtypeStruct((n_tok, D + LANES), F32), jax.ShapeDtypeStruct((8, n_tok), F32),
                   jax.ShapeDtypeStruct((CLASS_ROWS, LANES), F32)],
        scratch_shapes=[pltpu.VMEM((CLASS_ROWS, LANES), F32)],
        input_output_aliases={3: 0},
        compiler_params=_cparams("arbitrary"), name="router",
    )(logits_t, rb_col, tri, hx)


def _row_copies(idx_ref, hbm, vmem, sem, scatter):
    n = vmem.shape[0]
    def copy(u, row):
        src, dst = (vmem.at[pl.ds(u, 1)], hbm.at[pl.ds(row, 1)])
        return pltpu.make_async_copy(src, dst, sem) if scatter else pltpu.make_async_copy(dst, src, sem)

    def start(u, carry):
        copy(u, idx_ref[0, 0, u]).start()
        return carry

    def wait(u, carry):
        copy(0, 0).wait()
        return carry

    lax.fori_loop(0, n, start, 0, unroll=8)
    lax.fori_loop(0, n, wait, 0, unroll=8)


def _dispatch_body(pend_ref, padded_ref, nvalid_ref, pos_ref, hx_ref, xs_hbm, zbuf, zsem, sem, *, n_blk):
    TB = MOE_TILE
    i = pl.program_id(0)

    @pl.when(i == 0)
    def _():
        zbuf[...] = jnp.zeros_like(zbuf)
        fill = lambda row: pltpu.make_async_copy(zbuf, xs_hbm.at[pl.ds(row, TB)], zsem.at[0])
        for k in range(N_CLASSES):
            @pl.when(padded_ref[k] > 0)
            def _():
                fill(pl.multiple_of(pend_ref[k] - TB, TB)).start()
        for j in range(n_blk):
            @pl.when(j >= nvalid_ref[0])
            def _():
                fill(j * TB).start()
        for k in range(N_CLASSES):
            @pl.when(padded_ref[k] > 0)
            def _():
                fill(0).wait()
        for j in range(n_blk):
            @pl.when(j >= nvalid_ref[0])
            def _():
                fill(0).wait()

    _row_copies(pos_ref, xs_hbm, hx_ref, sem.at[0], True)


def _combine_body(pos_ref, ys_hbm, f_ref, sem):
    _row_copies(pos_ref, ys_hbm, f_ref, sem.at[0], False)


def _moe_body(elo_ref, ehi_ref, nvalid_ref, xs_ref, wg_lo, wu_lo, wd_lo, wg_hi, wu_hi, wd_hi, ys_ref, *, D):
    j = pl.program_id(0)

    @pl.when(j < nvalid_ref[0])
    def _():
        xb = xs_ref[:, :D].astype(MXU_DTYPE)
        info = xs_ref[:, D:]
        first_is_lo = info[:, REC_E0:REC_E0 + 1] <= info[:, REC_E1:REC_E1 + 1]
        g0, g1 = info[:, REC_G0:REC_G0 + 1], info[:, REC_G1:REC_G1 + 1]
        g_lo = jnp.where(first_is_lo, g0, g1)
        g_hi = jnp.where(first_is_lo, g1, g0)

        def ffn(wg, wu, wd):
            hg = jnp.dot(xb, wg[...], preferred_element_type=F32)
            hu = jnp.dot(xb, wu[...], preferred_element_type=F32)
            return _mm((hg * _sigmoid(hg)) * hu, wd[...])

        ys_ref[...] = g_lo * ffn(wg_lo, wu_lo, wd_lo) + g_hi * ffn(wg_hi, wu_hi, wd_hi)

    @pl.when(j >= nvalid_ref[0])
    def _():
        ys_ref[...] = jnp.zeros_like(ys_ref)


def _moe(layer, n_tok, D, hx, cls_rank, counts, wg, wu, wd):
    TB = MOE_TILE
    DE = wg.shape[-1]
    chunk = math.gcd(n_tok, PERM_CHUNK)
    cls = cls_rank[0].astype(jnp.int32)
    rank = cls_rank[1].astype(jnp.int32)
    cnt = counts[:N_CLASSES, 0].astype(jnp.int32)
    padded = (cnt + TB - 1) // TB * TB
    pend = jnp.cumsum(padded)
    pstart = pend - padded
    onehot = cls[:, None] == jnp.arange(N_CLASSES, dtype=jnp.int32)[None, :]
    pos = jnp.sum(jnp.where(onehot, pstart[None, :], 0), axis=1) + rank
    n_blk = -(-n_tok // TB) + N_CLASSES
    P = n_blk * TB
    nvalid = (pend[-1] // TB).astype(jnp.int32).reshape(1)
    blk = jnp.minimum(jnp.arange(n_blk, dtype=jnp.int32), nvalid - 1)
    blk_cls = jnp.minimum(jnp.sum(pend[None, :] <= (blk * TB)[:, None], axis=1), N_CLASSES - 1).astype(jnp.int32)
    pairs = [(u, v) for u in range(EXPERTS_PER_GROUP) for v in range(u + 1, EXPERTS_PER_GROUP)]
    pair_lo = jnp.array([u for u, _ in pairs], jnp.int32)
    pair_hi = jnp.array([v for _, v in pairs], jnp.int32)
    base = (blk_cls // PAIRS_PER_GROUP) * EXPERTS_PER_GROUP
    blk_lo = base + pair_lo[blk_cls % PAIRS_PER_GROUP]
    blk_hi = base + pair_hi[blk_cls % PAIRS_PER_GROUP]
    pos3 = pos.reshape(n_tok // chunk, 1, chunk)
    pos_spec = pl.BlockSpec((1, 1, chunk), lambda i, *_: (i, 0, 0), memory_space=pltpu.SMEM)
    any_spec = pl.BlockSpec(memory_space=pl.ANY)

    xs = pl.pallas_call(
        functools.partial(_dispatch_body, n_blk=n_blk),
        grid_spec=pltpu.PrefetchScalarGridSpec(
            num_scalar_prefetch=3, grid=(n_tok // chunk,),
            in_specs=[pos_spec, pl.BlockSpec((chunk, D + LANES), lambda i, *_: (i, 0))], out_specs=any_spec,
            scratch_shapes=[pltpu.VMEM((TB, D + LANES), F32), pltpu.SemaphoreType.DMA((1,)),
                            pltpu.SemaphoreType.DMA((1,))]),
        out_shape=jax.ShapeDtypeStruct((P, D + LANES), F32),
        compiler_params=_cparams("arbitrary"), name="moe_dispatch",
    )(pend.astype(jnp.int32), padded.astype(jnp.int32), nvalid, pos3, hx)

    w_spec = lambda shape, which: pl.BlockSpec(
        (None, None) + shape,
        (lambda j, elo, ehi, nv: (layer, elo[j], 0, 0)) if which == 0 else (lambda j, elo, ehi, nv: (layer, ehi[j], 0, 0)))
    ys = pl.pallas_call(
        functools.partial(_moe_body, D=D),
        grid_spec=pltpu.PrefetchScalarGridSpec(
            num_scalar_prefetch=3, grid=(n_blk,),
            in_specs=[pl.BlockSpec((TB, D + LANES), lambda j, elo, ehi, nv: (jnp.minimum(j, nv[0] - 1), 0)),
                      w_spec((D, DE), 0), w_spec((D, DE), 0), w_spec((DE, D), 0),
                      w_spec((D, DE), 1), w_spec((D, DE), 1), w_spec((DE, D), 1)],
            out_specs=pl.BlockSpec((TB, D), lambda j, elo, ehi, nv: (j, 0))),
        out_shape=jax.ShapeDtypeStruct((P, D), F32),
        compiler_params=_cparams("arbitrary"), name="moe_experts",
    )(blk_lo, blk_hi, nvalid, xs, wg, wu, wd, wg, wu, wd)

    return pl.pallas_call(
        _combine_body,
        grid_spec=pltpu.PrefetchScalarGridSpec(
            num_scalar_prefetch=0, grid=(n_tok // chunk,), in_specs=[pos_spec, any_spec],
            out_specs=pl.BlockSpec((chunk, D), lambda i: (i, 0)),
            scratch_shapes=[pltpu.SemaphoreType.DMA((1,))]),
        out_shape=jax.ShapeDtypeStruct((n_tok, D), F32),
        compiler_params=_cparams("arbitrary"), name="moe_combine",
    )(pos3, ys)


def _final_body(x_ref, f_ref, g2_ref, nw_ref, o_ref):
    o_ref[...] = _rms(x_ref[...] + g2_ref[0] * f_ref[...]) * nw_ref[...]


def _final(lay, layer, x, f, mods3, nw):
    D, TM = lay.D, TOKEN_TILE
    row = lambda i: (i, 0)
    return pl.pallas_call(
        _final_body, grid=(lay.NL // TM,),
        in_specs=[pl.BlockSpec((TM, D), row), pl.BlockSpec((TM, D), row), lay.mod_spec(layer, 5),
                  pl.BlockSpec((1, D), lambda i: (0, 0))],
        out_specs=pl.BlockSpec((TM, D), row),
        out_shape=jax.ShapeDtypeStruct((lay.NL, D), F32),
        compiler_params=_cparams("arbitrary"), name="final_norm",
    )(x, f, mods3, nw.reshape(1, D))


def kernel(x, c, ctx, c_ctx, w_in, w_out, ada_w, ada_b, norm1_w, norm2_w, diff_lambda, diff_norm_w, pool_w,
           pool_scale, ret_a_f, ret_a_b, ret_norm_w, router_w, router_b, moe_w_gate, moe_w_up, moe_w_down,
           final_norm_w):
    B, L, D = x.shape
    Lc = ctx.shape[1]
    depth = w_in.shape[0]
    lay = _Layout(B, L, Lc, D)
    assert B + 1 <= MOD_ROWS

    xa = (x.reshape(B * L, D), ctx.reshape(B * Lc, D))
    c_all = jnp.zeros((MOD_ROWS, D), F32).at[:B].set(c).at[B].set(c_ctx)
    mods3 = _ada_table(c_all, ada_w, ada_b).reshape(depth * MOD_ROWS * N_MOD, 1, D)
    tables = _rope_tables(lay)
    rw_pad = jnp.zeros((D, LANES), MXU_DTYPE).at[:, :N_EXPERTS].set(router_w.astype(MXU_DTYPE))
    rb_col = router_b.astype(F32).reshape(N_EXPERTS, 1)
    tri = jnp.tril(jnp.ones((TOKEN_TILE, TOKEN_TILE), F32)).astype(MXU_DTYPE)
    w_in_b, w_out_b = w_in.astype(MXU_DTYPE), w_out.astype(MXU_DTYPE)
    wg_b, wu_b, wd_b = moe_w_gate.astype(MXU_DTYPE), moe_w_up.astype(MXU_DTYPE), moe_w_down.astype(MXU_DTYPE)

    f = None
    for l in range(depth):
        last = l == depth - 1
        lam_init = 0.8 - 0.6 * math.exp(-0.3 * l)
        dl = diff_lambda[l].astype(F32)
        lam = (jnp.exp(jnp.sum(dl[0] * dl[1])) - jnp.exp(jnp.sum(dl[2] * dl[3])) + lam_init).reshape(1)
        p, xa = _inproj(lay, l, xa, f, mods3, norm1_w[l], w_in_b, tables)
        a_lat, a_ctx = _diff_attention(lay, p, lam, diff_norm_w[l], 1.0 - lam_init, not last)
        eye = jnp.eye(len(POOL_WINDOWS), dtype=F32)
        w_bd = (eye[:, None, :, None] * pool_w[l][:, :, None, :]).reshape(POOL_WIDTH, POOL_WIDTH).astype(MXU_DTYPE)
        bp = _pool(lay, p, w_bd, pool_scale[l])
        lg_f = -jnp.exp(ret_a_f[l].astype(F32))
        lg_b = -jnp.exp(ret_a_b[l].astype(F32))
        lanes = lambda lg: jnp.repeat(lg, HEAD_DIM).reshape(1, RET_WIDTH)
        yf, yb = _retention(lay, p, lg_f, lanes(lg_f), lg_b, lanes(lg_b))
        n_tok = lay.NL if last else lay.T
        xa, hx, logits_t = _outproj(lay, l, n_tok, xa, a_lat, a_ctx, bp, yf, yb, p, ret_norm_w[l], w_out_b, mods3,
                                    norm2_w[l], rw_pad)
        hx, cls_rank, counts = _router(n_tok, D, hx, logits_t, rb_col, tri)
        f = _moe(l, n_tok, D, hx, cls_rank, counts, wg_b, wu_b, wd_b)
    out = _final(lay, depth - 1, xa, f, mods3, final_norm_w)
    return out.reshape(B, L, D)
```

```python
import functools
import math

import jax
import jax.numpy as jnp
from jax import lax
from jax.experimental import pallas as pl
from jax.experimental.pallas import tpu as pltpu

F32 = jnp.float32
MXU_DTYPE = jnp.bfloat16
ACT_DTYPE = jnp.bfloat16

GRID_W = 64
HEAD_DIM = 64
DIFF_HEADS = 4
DIFF_V_DIM = 2 * HEAD_DIM
DIFF_WIDTH = DIFF_HEADS * DIFF_V_DIM
POOL_WINDOWS = (2, 4, 8, 16)
POOL_GROUP = 64
POOL_WIDTH = POOL_GROUP * len(POOL_WINDOWS)
RET_HEADS = 4
RET_WIDTH = RET_HEADS * HEAD_DIM
OFF_DQ = 0
OFF_DK = OFF_DQ + DIFF_WIDTH
OFF_DV = OFF_DK + DIFF_WIDTH
OFF_PU = OFF_DV + DIFF_WIDTH
OFF_RQ = OFF_PU + POOL_WIDTH
OFF_RK = OFF_RQ + RET_WIDTH
OFF_RV = OFF_RK + RET_WIDTH
OFF_RG = OFF_RV + RET_WIDTH
IN_WIDTH = OFF_RG + RET_WIDTH
ROPE_BASE = 10000.0
ROPE_FREQS = HEAD_DIM // 4
N_EXPERTS = 16
N_GROUPS = 4
EXPERTS_PER_GROUP = N_EXPERTS // N_GROUPS
PAIRS_PER_GROUP = EXPERTS_PER_GROUP * (EXPERTS_PER_GROUP - 1) // 2
N_CLASSES = N_GROUPS * PAIRS_PER_GROUP
N_MOD = 6
EPS = 1e-6

LANES = 128
MOD_ROWS = 16
TOKEN_TILE = 512
SEQ_TILE = 256
ATTN_SUBTILES = 4
ATTN_SUB_ROWS = 256
ATTN_KEY_CHUNK = 512
HALO = 16
MOE_TILE = 256
COL = 256
ROUTER_BLOCK = 2048
CLASS_ROWS = 32
PERM_CHUNK = 2048
REC_E0, REC_E1, REC_G0, REC_G1, REC_CLS, REC_RANK = range(6)
VMEM_LIMIT = 56 * 1024 * 1024


def _mm(a, b):
    return jnp.dot(a.astype(MXU_DTYPE), b.astype(MXU_DTYPE), preferred_element_type=F32)


def _mm_nt(a, b):
    return lax.dot_general(a.astype(MXU_DTYPE), b.astype(MXU_DTYPE), (((1,), (1,)), ((), ())),
                           preferred_element_type=F32)


def _mm_tn(a, b):
    return lax.dot_general(a.astype(MXU_DTYPE), b.astype(MXU_DTYPE), (((0,), (0,)), ((), ())),
                           preferred_element_type=F32)


def _split(x):
    hi = x.astype(MXU_DTYPE)
    lo = (x - hi.astype(F32)).astype(MXU_DTYPE)
    return hi, lo


def _mm_hi(a, b):
    ah, al = _split(a)
    bh, bl = _split(b)
    d = lambda u, v: jnp.dot(u, v, preferred_element_type=F32)
    return d(ah, bh) + (d(ah, bl) + d(al, bh))


def _sigmoid(x):
    return 1.0 / (1.0 + jnp.exp(-x))


def _rms(x):
    return x * lax.rsqrt(jnp.mean(x * x, axis=-1, keepdims=True) + EPS)


def _cparams(*sem):
    return pltpu.CompilerParams(dimension_semantics=sem, vmem_limit_bytes=VMEM_LIMIT)


def _ada_body(c_ref, w_ref, b_ref, o_ref):
    c = c_ref[...]
    o_ref[...] = _mm_hi(c * _sigmoid(c), w_ref[...]) + b_ref[...]


def _ada_table(c_all, ada_w, ada_b):
    depth, d, n = ada_w.shape
    nb = n // N_MOD
    return pl.pallas_call(
        _ada_body,
        grid=(depth, n // nb),
        in_specs=[pl.BlockSpec((MOD_ROWS, d), lambda l, j: (0, 0)),
                  pl.BlockSpec((None, d, nb), lambda l, j: (l, 0, j)),
                  pl.BlockSpec((None, 1, nb), lambda l, j: (l, 0, j))],
        out_specs=pl.BlockSpec((None, MOD_ROWS, nb), lambda l, j: (l, 0, j)),
        out_shape=jax.ShapeDtypeStruct((depth, MOD_ROWS, n), F32),
        compiler_params=_cparams("arbitrary", "arbitrary"),
        name="ada_table",
    )(c_all, ada_w, ada_b.reshape(depth, 1, n))


class _Layout:
    def __init__(self, B, L, Lc, D):
        self.B, self.L, self.Lc, self.D = B, L, Lc, D
        self.NL, self.NC = B * L, B * Lc
        self.T = self.NL + self.NC
        assert L % TOKEN_TILE == 0 and self.NC % TOKEN_TILE == 0
        assert L % SEQ_TILE == 0 and Lc == SEQ_TILE and L % GRID_W == 0
        self.n_lat_tiles = self.NL // TOKEN_TILE
        self.tiles_per_seq = L // TOKEN_TILE

    def mod_row(self, i):
        return jnp.where(i < self.n_lat_tiles, i // self.tiles_per_seq, self.B)

    def rope_block(self, i):
        return jnp.where(i < self.n_lat_tiles, i % self.tiles_per_seq, self.tiles_per_seq)

    def mod_spec(self, layer, k):
        base = layer * MOD_ROWS * N_MOD + k
        return pl.BlockSpec((1, 1, self.D), lambda i: (base + self.mod_row(i) * N_MOD, 0, 0))


def _rope_tables(lay):
    L = lay.L
    rows = L // GRID_W
    row = jnp.repeat(jnp.arange(rows), GRID_W).astype(F32)
    col = jnp.tile(jnp.arange(GRID_W), rows).astype(F32)
    inv = ROPE_BASE ** (-jnp.arange(ROPE_FREQS, dtype=F32) / ROPE_FREQS)
    ang_r = row[:, None] * inv
    ang_c = col[:, None] * inv
    ang = jnp.concatenate([ang_r, ang_r, ang_c, ang_c], axis=-1)
    cos, sin = jnp.cos(ang), jnp.sin(ang)
    first_half = (jnp.arange(HEAD_DIM) % (2 * ROPE_FREQS)) < ROPE_FREQS
    sin_a = jnp.where(first_half, -sin, 0.0)
    sin_b = jnp.where(first_half, 0.0, sin)
    rep = LANES // HEAD_DIM
    ident = [jnp.ones((TOKEN_TILE, LANES), F32), jnp.zeros((TOKEN_TILE, LANES), F32),
             jnp.zeros((TOKEN_TILE, LANES), F32)]
    return [jnp.concatenate([jnp.tile(t, (1, rep)), e], axis=0) for t, e in zip((cos, sin_a, sin_b), ident)]


_ROPE_COLS = tuple(range(OFF_DQ // COL, OFF_DV // COL)) + (OFF_RQ // COL, OFF_RK // COL)
LOG2E = 1.4426950408889634
_COL_SCALE = {j: HEAD_DIM ** -0.5 * LOG2E for j in range(OFF_DQ // COL, OFF_DK // COL)}
_COL_SCALE[OFF_RK // COL] = HEAD_DIM ** -0.5


def _inproj_body(*refs, with_f, n_lat_tiles):
    if with_f:
        (x_ref, f_ref, g2_ref, nw_ref, sh_ref, sc_ref, w_ref, cos_ref, sa_ref, sb_ref, p_ref, xo_ref) = refs
        x = x_ref[...] + g2_ref[0] * f_ref[...]
    else:
        (xl_ref, xc_ref, nw_ref, sh_ref, sc_ref, w_ref, cos_ref, sa_ref, sb_ref, p_ref, xo_ref) = refs
        x = jnp.where(pl.program_id(0) < n_lat_tiles, xl_ref[...], xc_ref[...])
    xo_ref[...] = x
    h = (_rms(x) * nw_ref[...]) * (1.0 + sc_ref[0]) + sh_ref[0]
    hb = h.astype(MXU_DTYPE)
    cos, sa, sb = cos_ref[...], sa_ref[...], sb_ref[...]
    for j in range(IN_WIDTH // COL):
        acc = jnp.dot(hb, w_ref[:, j * COL:(j + 1) * COL], preferred_element_type=F32)
        if j in _ROPE_COLS:
            parts = []
            for t in range(COL // LANES):
                a = acc[:, t * LANES:(t + 1) * LANES]
                parts.append(a * cos + pltpu.roll(a, LANES - ROPE_FREQS, 1) * sa + pltpu.roll(a, ROPE_FREQS, 1) * sb)
            acc = jnp.concatenate(parts, axis=1)
        if j in _COL_SCALE:
            acc = acc * _COL_SCALE[j]
        p_ref[:, j * COL:(j + 1) * COL] = acc.astype(p_ref.dtype)


def _inproj(lay, layer, x, f, mods3, nw, w_in, tables):
    D, TM = lay.D, TOKEN_TILE
    row = lambda i: (i, 0)
    const = lambda i: (0, 0)
    n_lat = lay.n_lat_tiles
    tbl = pl.BlockSpec((TM, LANES), lambda i: (lay.rope_block(i), 0))
    if f is not None:
        in_specs = [pl.BlockSpec((TM, D), row), pl.BlockSpec((TM, D), row), lay.mod_spec(layer - 1, 5)]
        args = [x, f, mods3]
    else:
        in_specs = [pl.BlockSpec((TM, D), lambda i: (jnp.minimum(i, n_lat - 1), 0)),
                    pl.BlockSpec((TM, D), lambda i: (jnp.maximum(i - n_lat, 0), 0))]
        args = list(x)
    in_specs += [pl.BlockSpec((1, D), const), lay.mod_spec(layer, 0), lay.mod_spec(layer, 1),
                 pl.BlockSpec((None, D, IN_WIDTH), lambda i: (layer, 0, 0)), tbl, tbl, tbl]
    args += [nw.reshape(1, D), mods3, mods3, w_in] + list(tables)
    return pl.pallas_call(
        functools.partial(_inproj_body, with_f=f is not None, n_lat_tiles=n_lat),
        grid=(lay.T // TM,), in_specs=in_specs,
        out_specs=[pl.BlockSpec((TM, IN_WIDTH), row), pl.BlockSpec((TM, D), row)],
        out_shape=[jax.ShapeDtypeStruct((lay.T, IN_WIDTH), ACT_DTYPE), jax.ShapeDtypeStruct((lay.T, D), F32)],
        compiler_params=_cparams("arbitrary"), name="inproj",
    )(*args)


def _map_masks(q):
    lane = lax.broadcasted_iota(jnp.int32, (1, DIFF_V_DIM), 1)
    zero = jnp.zeros_like(q)
    return [jnp.where(lane < HEAD_DIM, q, zero), jnp.where(lane >= HEAD_DIM, q, zero)]


def _scores(qm, k_refs):
    chunks, run = [], None
    for k in k_refs:
        for c0 in range(0, k.shape[0], ATTN_KEY_CHUNK):
            s = _mm_nt(qm, k[c0:c0 + ATTN_KEY_CHUNK, :])
            chunks.append(s)
            for t in range(s.shape[1] // LANES):
                piece = s[:, t * LANES:(t + 1) * LANES]
                run = piece if run is None else jnp.maximum(run, piece)
    return chunks, jnp.max(run, axis=-1, keepdims=True)


def _softmax_diff(maps, lam):
    es, ls = [], []
    for chunks, mx in maps:
        e = [jnp.exp2(t - mx) for t in chunks]
        ls.append(functools.reduce(jnp.add, [jnp.sum(t, axis=-1, keepdims=True) for t in e]))
        es.append([t.astype(MXU_DTYPE) for t in e])
    c = (lam * ls[0] / ls[1]).astype(MXU_DTYPE)
    return [e1 - e2 * c for e1, e2 in zip(es[0], es[1])], 1.0 / ls[0]


def _head_norm(o, w, post_scale):
    return (_rms(o) * w) * post_scale


def _attn_body(lam_ref, q_ref, *refs, n_seg, n_sub, post_scale):
    k_refs, v_refs = refs[:n_seg], refs[n_seg:2 * n_seg]
    w_ref, o_ref = refs[2 * n_seg], refs[2 * n_seg + 1]
    rows = q_ref.shape[0] // n_sub

    def finish(u, maps):
        p, r1 = _softmax_diff(maps, lam_ref[0])
        o, at = None, 0
        for v in v_refs:
            n_c = -(-v.shape[0] // ATTN_KEY_CHUNK)
            part = jnp.dot(jnp.concatenate(p[at:at + n_c], axis=1), v[...], preferred_element_type=F32)
            o = part if o is None else o + part
            at += n_c
        o_ref[u * rows:(u + 1) * rows, :] = _head_norm(o * r1, w_ref[...], post_scale).astype(o_ref.dtype)

    scores = [[_scores(qm, k_refs) for qm in _map_masks(q_ref[u * rows:(u + 1) * rows, :])] for u in range(n_sub)]
    for u, maps in enumerate(scores):
        finish(u, maps)


def _diff_attention(lay, p, lam, norm_w, post_scale, need_ctx):
    B, L, Lc, H = lay.B, lay.L, lay.Lc, DIFF_HEADS
    W = DIFF_V_DIM
    tq = ATTN_SUBTILES * ATTN_SUB_ROWS
    assert L % tq == 0
    nq = L // tq
    k_blk, v_blk = OFF_DK // W, OFF_DV // W
    ctx0 = lay.NL // Lc
    lam_spec = pl.BlockSpec(memory_space=pltpu.SMEM)
    w_spec = pl.BlockSpec((1, W), lambda *_: (0, 0))
    a_lat = pl.pallas_call(
        functools.partial(_attn_body, n_seg=2, n_sub=ATTN_SUBTILES, post_scale=post_scale),
        grid=(B, H, nq),
        in_specs=[lam_spec,
                  pl.BlockSpec((tq, W), lambda b, h, i: (b * nq + i, h)),
                  pl.BlockSpec((Lc, W), lambda b, h, i: (ctx0 + b, k_blk + h)),
                  pl.BlockSpec((L, W), lambda b, h, i: (b, k_blk + h)),
                  pl.BlockSpec((Lc, W), lambda b, h, i: (ctx0 + b, v_blk + h)),
                  pl.BlockSpec((L, W), lambda b, h, i: (b, v_blk + h)),
                  w_spec],
        out_specs=pl.BlockSpec((tq, W), lambda b, h, i: (b * nq + i, h)),
        out_shape=jax.ShapeDtypeStruct((lay.NL, DIFF_WIDTH), ACT_DTYPE),
        compiler_params=_cparams("arbitrary", "arbitrary", "arbitrary"),
        name="diff_attn",
    )(lam, p, p, p, p, p, norm_w.reshape(1, W))
    if not need_ctx:
        return a_lat, None
    a_ctx = pl.pallas_call(
        functools.partial(_attn_body, n_seg=1, n_sub=1, post_scale=post_scale),
        grid=(B, H),
        in_specs=[lam_spec,
                  pl.BlockSpec((Lc, W), lambda b, h: (ctx0 + b, h)),
                  pl.BlockSpec((Lc, W), lambda b, h: (ctx0 + b, k_blk + h)),
                  pl.BlockSpec((Lc, W), lambda b, h: (ctx0 + b, v_blk + h)),
                  w_spec],
        out_specs=pl.BlockSpec((Lc, W), lambda b, h: (b, h)),
        out_shape=jax.ShapeDtypeStruct((lay.NC, DIFF_WIDTH), ACT_DTYPE),
        compiler_params=_cparams("arbitrary", "arbitrary"),
        name="diff_attn_ctx",
    )(lam, p, p, p, norm_w.reshape(1, W))
    return a_lat, a_ctx


def _pool_body(prev_ref, cur_ref, next_ref, w_ref, scale_ref, o_ref, *, lay):
    i = pl.program_id(0)
    n_lat = lay.NL // SEQ_TILE
    per_seq = jnp.where(i < n_lat, lay.L // SEQ_TILE, lay.Lc // SEQ_TILE)
    idx = jnp.where(i < n_lat, i, i - n_lat) % per_seq
    seq_len = per_seq * SEQ_TILE
    has_prev = (idx > 0).astype(F32)
    has_next = (idx < per_seq - 1).astype(F32)
    cur = cur_ref[...].astype(F32)
    u = jnp.concatenate([prev_ref[...].astype(F32) * has_prev, cur, next_ref[...].astype(F32) * has_next], axis=0)
    sums = {1: u}
    w = 1
    while w < POOL_WINDOWS[-1]:
        s = sums[w]
        m = s.shape[0] - w
        sums[2 * w] = s[:m] + s[w:w + m]
        w *= 2
    pos = idx * SEQ_TILE + lax.broadcasted_iota(jnp.int32, (SEQ_TILE, 1), 0)
    group = lax.broadcasted_iota(jnp.int32, (1, POOL_WIDTH), 1) // POOL_GROUP
    mean = jnp.zeros((SEQ_TILE, POOL_WIDTH), F32)
    for g, win in enumerate(POOL_WINDOWS):
        start = HALO - win // 2
        cnt = jnp.minimum(pos + (win - win // 2), seq_len) - jnp.maximum(pos - win // 2, 0)
        mean = jnp.where(group == g, sums[win][start:start + SEQ_TILE] / cnt.astype(F32), mean)
    o_ref[...] = (_mm(mean - cur, w_ref[...]) * scale_ref[...]).astype(o_ref.dtype)


def _pool(lay, p, w_bd, scale):
    n = lay.T // SEQ_TILE
    col = OFF_PU // POOL_WIDTH
    per = SEQ_TILE // HALO
    last = lay.T // HALO - 1
    return pl.pallas_call(
        functools.partial(_pool_body, lay=lay),
        grid=(n,),
        in_specs=[pl.BlockSpec((HALO, POOL_WIDTH), lambda i: (jnp.maximum(i * per - 1, 0), col)),
                  pl.BlockSpec((SEQ_TILE, POOL_WIDTH), lambda i: (i, col)),
                  pl.BlockSpec((HALO, POOL_WIDTH), lambda i: (jnp.minimum((i + 1) * per, last), col)),
                  pl.BlockSpec((POOL_WIDTH, POOL_WIDTH), lambda i: (0, 0)),
                  pl.BlockSpec((1, POOL_WIDTH), lambda i: (0, 0))],
        out_specs=pl.BlockSpec((SEQ_TILE, POOL_WIDTH), lambda i: (i, 0)),
        out_shape=jax.ShapeDtypeStruct((lay.T, POOL_WIDTH), ACT_DTYPE),
        compiler_params=_cparams("arbitrary"), name="pool",
    )(p, p, p, w_bd, scale.reshape(1, POOL_WIDTH))


def _ret_tables(lgs_ref, lgv, d_ref, xi_ref, zeta_ref, reverse):
    C = SEQ_TILE
    ri = lax.broadcasted_iota(jnp.int32, (C, C), 0)
    ci = lax.broadcasted_iota(jnp.int32, (C, C), 1)
    dist = (ci - ri if reverse else ri - ci).astype(F32)
    keep = dist > 0 if reverse else dist >= 0
    for h in range(RET_HEADS):
        d_ref[h * C:(h + 1) * C, :] = jnp.where(keep, jnp.exp(lgs_ref[h] * jnp.maximum(dist, 0.0)), 0.0)
    t = lax.broadcasted_iota(jnp.int32, (C, 1), 0).astype(F32)
    xi_ref[...] = jnp.exp(lgv * ((C - t) if reverse else (t + 1.0)))
    zeta_ref[...] = jnp.exp(lgv * (t if reverse else (C - 1.0 - t)))


def _ret_chunk(q, k, v, lgv, s_ref, d_ref, xi_ref, zeta_ref):
    C = SEQ_TILE
    lane_head = lax.broadcasted_iota(jnp.int32, (1, RET_WIDTH), 1) // HEAD_DIM
    row_head = lax.broadcasted_iota(jnp.int32, (RET_WIDTH, 1), 0) // HEAD_DIM
    zero = jnp.zeros_like(q)
    qs = jnp.concatenate([jnp.where(lane_head == h, q, zero) for h in range(RET_HEADS)], axis=0)
    sd = _mm_nt(qs, k) * d_ref[...]
    yv = _mm(sd, v)
    y = functools.reduce(jnp.add, [jnp.where(lane_head == h, yv[h * C:(h + 1) * C], 0.0) for h in range(RET_HEADS)])
    state = s_ref[...]
    y = y + _mm(q.astype(F32) * xi_ref[...], state)
    kv = _mm_tn(k.astype(F32) * zeta_ref[...], v)
    s_ref[...] = state * jnp.exp(lgv * C) + jnp.where(row_head == lane_head, kv, 0.0)
    return y


def _ret_body(lgs_f, lgv_f, lgs_b, lgv_b, qf, kf, vf, qb, kb, vb, yf_ref, yb_ref,
              s_f, d_f, xi_f, zeta_f, s_b, d_b, xi_b, zeta_b):
    b, j = pl.program_id(0), pl.program_id(1)

    @pl.when((b == 0) & (j == 0))
    def _():
        _ret_tables(lgs_f, lgv_f[...], d_f, xi_f, zeta_f, False)
        _ret_tables(lgs_b, lgv_b[...], d_b, xi_b, zeta_b, True)

    @pl.when(j == 0)
    def _():
        s_f[...] = jnp.zeros_like(s_f)
        s_b[...] = jnp.zeros_like(s_b)

    yf_ref[...] = _ret_chunk(qf[...], kf[...], vf[...], lgv_f[...], s_f, d_f, xi_f, zeta_f)
    yb_ref[...] = _ret_chunk(qb[...], kb[...], vb[...], lgv_b[...], s_b, d_b, xi_b, zeta_b)


def _retention(lay, p, lgs_f, lgv_f, lgs_b, lgv_b):
    B, L = lay.B, lay.L
    C, W = SEQ_TILE, RET_WIDTH
    nch = L // C
    ctx0 = lay.NL // C
    rows_f = lambda b, j: jnp.where(j == 0, ctx0 + b, b * nch + j - 1)
    rows_b = lambda b, j: jnp.where(j == 0, ctx0 + b, b * nch + nch - j)
    spec = lambda rows, c: pl.BlockSpec((C, W), lambda b, j: (rows(b, j), c))
    smem = pl.BlockSpec(memory_space=pltpu.SMEM)
    vec = pl.BlockSpec((1, W), lambda b, j: (0, 0))
    cols = (OFF_RQ // W, OFF_RK // W, OFF_RV // W)
    tables = [pltpu.VMEM((W, W), F32), pltpu.VMEM((RET_HEADS * C, C), F32), pltpu.VMEM((C, W), F32), pltpu.VMEM((C, W), F32)]
    return pl.pallas_call(
        _ret_body,
        grid=(B, nch + 1),
        in_specs=[smem, vec, smem, vec] + [spec(rows_f, c) for c in cols] + [spec(rows_b, c) for c in cols],
        out_specs=[spec(rows_f, 0), spec(rows_b, 0)],
        out_shape=[jax.ShapeDtypeStruct((lay.T, W), F32), jax.ShapeDtypeStruct((lay.T, W), F32)],
        scratch_shapes=tables + tables,
        compiler_params=_cparams("arbitrary", "arbitrary"),
        name="retention",
    )(lgs_f, lgv_f, lgs_b, lgv_b, p, p, p, p, p, p)


def _retention_output(yf, yb, gate, w):
    y = yf + yb
    lane_head = lax.broadcasted_iota(jnp.int32, (1, RET_WIDTH), 1) // HEAD_DIM
    row_head = lax.broadcasted_iota(jnp.int32, (RET_WIDTH, 1), 0) // HEAD_DIM
    same = (row_head == lane_head).astype(F32) * (1.0 / HEAD_DIM)
    ms = _mm_hi(y * y, same)
    g = gate.astype(F32)
    return (g * _sigmoid(g)) * (y * lax.rsqrt(ms + EPS) * w)


def _top2(vals):
    n = len(vals)
    v1 = functools.reduce(jnp.maximum, vals)
    i1 = jnp.full_like(v1, n - 1)
    for e in range(n - 2, -1, -1):
        i1 = jnp.where(vals[e] == v1, float(e), i1)
    rest = [jnp.where(i1 == float(e), -jnp.inf, vals[e]) for e in range(n)]
    v2 = functools.reduce(jnp.maximum, rest)
    i2 = jnp.full_like(v1, n - 1)
    for e in range(n - 2, -1, -1):
        i2 = jnp.where(rest[e] == v2, float(e), i2)
    return v1, i1, v2, i2


def _route(logits_t, bias):
    s = _sigmoid(logits_t)
    sel = s + bias
    groups = []
    for g in range(N_GROUPS):
        rows = [sel[g * EXPERTS_PER_GROUP + e:g * EXPERTS_PER_GROUP + e + 1] for e in range(EXPERTS_PER_GROUP)]
        groups.append(_top2(rows))
    score = [v1 + v2 for v1, _, v2, _ in groups]
    best = functools.reduce(jnp.maximum, score)
    gi = jnp.full_like(best, N_GROUPS - 1)
    for g in range(N_GROUPS - 2, -1, -1):
        gi = jnp.where(score[g] == best, float(g), gi)
    pick = lambda k: functools.reduce(
        lambda acc, g: jnp.where(gi == float(g), groups[g][k], acc), range(N_GROUPS - 1), groups[N_GROUPS - 1][k])
    e0 = gi * EXPERTS_PER_GROUP + pick(1)
    e1 = gi * EXPERTS_PER_GROUP + pick(3)
    s0 = jnp.zeros_like(best)
    s1 = jnp.zeros_like(best)
    for e in range(N_EXPERTS):
        s0 = jnp.where(e0 == float(e), s[e:e + 1], s0)
        s1 = jnp.where(e1 == float(e), s[e:e + 1], s1)
    tot = s0 + s1
    return e0, e1, s0 / tot, s1 / tot


def _outproj_body(*refs, lay, with_ctx):
    if with_ctx:
        (x_ref, al_ref, ac_ref, b_ref, yf_ref, yb_ref, rg_ref, rnw_ref, w_ref, g1_ref, nw_ref, sh_ref, sc_ref, rw_ref,
         xo_ref, hx_ref, lt_ref) = refs
        a = jnp.where(pl.program_id(0) < lay.n_lat_tiles, al_ref[...], ac_ref[...])
    else:
        (x_ref, al_ref, b_ref, yf_ref, yb_ref, rg_ref, rnw_ref, w_ref, g1_ref, nw_ref, sh_ref, sc_ref, rw_ref,
         xo_ref, hx_ref, lt_ref) = refs
        a = al_ref[...]
    D = lay.D
    r = _retention_output(yf_ref[...], yb_ref[...], rg_ref[...], rnw_ref[...])
    y = (_mm(a, w_ref[:DIFF_WIDTH, :])
         + _mm(b_ref[...], w_ref[DIFF_WIDTH:DIFF_WIDTH + POOL_WIDTH, :])
         + _mm(r, w_ref[DIFF_WIDTH + POOL_WIDTH:, :]))
    x = x_ref[...] + g1_ref[0] * y
    xo_ref[...] = x
    h = (_rms(x) * nw_ref[...]) * (1.0 + sc_ref[0]) + sh_ref[0]
    hx_ref[:, :D] = h
    hx_ref[:, D:] = jnp.zeros((TOKEN_TILE, LANES), F32)
    lt_ref[...] = jnp.transpose(_mm(h, rw_ref[...]))[:N_EXPERTS]


def _outproj(lay, layer, n_tok, x, a_lat, a_ctx, b, yf, yb, p, ret_nw, w_out, mods3, nw, rw_pad):
    D, TM = lay.D, TOKEN_TILE
    row = lambda i: (i, 0)
    const = lambda i: (0, 0)
    tile = lambda w: pl.BlockSpec((TM, w), row)
    n_lat = lay.n_lat_tiles
    in_specs = [tile(D), pl.BlockSpec((TM, DIFF_WIDTH), lambda i: (jnp.minimum(i, n_lat - 1), 0))]
    args = [x, a_lat]
    if a_ctx is not None:
        in_specs.append(pl.BlockSpec((TM, DIFF_WIDTH), lambda i: (jnp.maximum(i - n_lat, 0), 0)))
        args.append(a_ctx)
    in_specs += [tile(POOL_WIDTH), tile(RET_WIDTH), tile(RET_WIDTH),
                 pl.BlockSpec((TM, RET_WIDTH), lambda i: (i, OFF_RG // RET_WIDTH)), pl.BlockSpec((1, RET_WIDTH), const),
                 pl.BlockSpec((None, DIFF_WIDTH + POOL_WIDTH + RET_WIDTH, D), lambda i: (layer, 0, 0)),
                 lay.mod_spec(layer, 2), pl.BlockSpec((1, D), const), lay.mod_spec(layer, 3), lay.mod_spec(layer, 4),
                 pl.BlockSpec((D, LANES), const)]
    args += [b, yf, yb, p, ret_nw.reshape(1, RET_WIDTH), w_out, mods3, nw.reshape(1, D), mods3, mods3, rw_pad]
    return pl.pallas_call(
        functools.partial(_outproj_body, lay=lay, with_ctx=a_ctx is not None),
        grid=(n_tok // TM,), in_specs=in_specs,
        out_specs=[tile(D), tile(D + LANES), pl.BlockSpec((N_EXPERTS, TM), lambda i: (0, i))],
        out_shape=[jax.ShapeDtypeStruct((n_tok, D), F32), jax.ShapeDtypeStruct((n_tok, D + LANES), F32),
                   jax.ShapeDtypeStruct((N_EXPERTS, n_tok), F32)],
        compiler_params=_cparams("arbitrary"), name="outproj",
    )(*args)


def _router_body(lt_ref, rb_ref, tri_ref, hx_any, rec_ref, cr_ref, cnt_ref, run_ref):
    del hx_any
    n = lt_ref.shape[1]
    e0, e1, g0, g1 = _route(lt_ref[...], rb_ref[...])
    lo, hi = jnp.minimum(e0, e1), jnp.maximum(e0, e1)
    grp = jnp.floor(lo * (1.0 / EXPERTS_PER_GROUP))
    pa, pb = lo - grp * EXPERTS_PER_GROUP, hi - grp * EXPERTS_PER_GROUP
    cls = grp * PAIRS_PER_GROUP + pa * ((2 * EXPERTS_PER_GROUP - 1) - pa) * 0.5 + (pb - pa - 1.0)

    @pl.when(pl.program_id(0) == 0)
    def _():
        run_ref[...] = jnp.zeros_like(run_ref)

    run = run_ref[:, :1]
    onehot = jnp.where(cls == lax.broadcasted_iota(jnp.int32, (CLASS_ROWS, 1), 0).astype(F32), 1.0, 0.0)
    tri = tri_ref[...]
    ranks = []
    for c0 in range(0, n, TOKEN_TILE):
        oh = onehot[:, c0:c0 + TOKEN_TILE]
        incl = _mm_nt(oh, tri)
        ranks.append(jnp.sum(oh * (incl + run), axis=0, keepdims=True) - 1.0)
        run = run + incl[:, TOKEN_TILE - 1:TOKEN_TILE]
    rank = jnp.concatenate(ranks, axis=1)
    run_ref[...] = jnp.broadcast_to(run, run_ref.shape)
    cnt_ref[...] = jnp.broadcast_to(run, cnt_ref.shape)
    cr_ref[...] = jnp.concatenate([cls, rank, jnp.zeros((6, n), F32)], axis=0)
    rec_ref[...] = jnp.transpose(jnp.concatenate([e0, e1, g0, g1, cls, rank, jnp.zeros((LANES - 6, n), F32)], axis=0))


def _router(n_tok, D, hx, logits_t, rb_col, tri):
    blk = math.gcd(n_tok, ROUTER_BLOCK)
    return pl.pallas_call(
        _router_body,
        grid=(n_tok // blk,),
        in_specs=[pl.BlockSpec((N_EXPERTS, blk), lambda i: (0, i)), pl.BlockSpec((N_EXPERTS, 1), lambda i: (0, 0)),
                  pl.BlockSpec((TOKEN_TILE, TOKEN_TILE), lambda i: (0, 0)), pl.BlockSpec(memory_space=pl.ANY)],
        out_specs=[pl.BlockSpec((blk, LANES), lambda i: (i, D // LANES)), pl.BlockSpec((8, blk), lambda i: (0, i)),
                   pl.BlockSpec((CLASS_ROWS, LANES), lambda i: (0, 0))],
        out_shape=[jax.ShapeDtypeStruct((n_tok, D + LANES), F32), jax.ShapeDtypeStruct((8, n_tok), F32),
                   jax.ShapeDtypeStruct((CLASS_ROWS, LANES), F32)],
        scratch_shapes=[pltpu.VMEM((CLASS_ROWS, LANES), F32)],
        input_output_aliases={3: 0},
        compiler_params=_cparams("arbitrary"), name="router",
    )(logits_t, rb_col, tri, hx)


def _row_copies(idx_ref, hbm, vmem, sem, scatter):
    n = vmem.shape[0]
    def copy(u, row):
        src, dst = (vmem.at[pl.ds(u, 1)], hbm.at[pl.ds(row, 1)])
        return pltpu.make_async_copy(src, dst, sem) if scatter else pltpu.make_async_copy(dst, src, sem)

    def start(u, carry):
        copy(u, idx_ref[0, 0, u]).start()
        return carry

    def wait(u, carry):
        copy(0, 0).wait()
        return carry

    lax.fori_loop(0, n, start, 0, unroll=8)
    lax.fori_loop(0, n, wait, 0, unroll=8)


def _dispatch_body(pend_ref, padded_ref, nvalid_ref, pos_ref, hx_ref, xs_hbm, zbuf, zsem, sem, *, n_blk):
    TB = MOE_TILE
    i = pl.program_id(0)

    @pl.when(i == 0)
    def _():
        zbuf[...] = jnp.zeros_like(zbuf)
        fill = lambda row: pltpu.make_async_copy(zbuf, xs_hbm.at[pl.ds(row, TB)], zsem.at[0])
        for k in range(N_CLASSES):
            @pl.when(padded_ref[k] > 0)
            def _():
                fill(pl.multiple_of(pend_ref[k] - TB, TB)).start()
        for j in range(n_blk):
            @pl.when(j >= nvalid_ref[0])
            def _():
                fill(j * TB).start()
        for k in range(N_CLASSES):
            @pl.when(padded_ref[k] > 0)
            def _():
                fill(0).wait()
        for j in range(n_blk):
            @pl.when(j >= nvalid_ref[0])
            def _():
                fill(0).wait()

    _row_copies(pos_ref, xs_hbm, hx_ref, sem.at[0], True)


def _combine_body(pos_ref, ys_hbm, f_ref, sem):
    _row_copies(pos_ref, ys_hbm, f_ref, sem.at[0], False)


def _combine_final_body(pos_ref, ys_hbm, x_ref, g2_ref, nw_ref, o_ref, fbuf, sem):
    _row_copies(pos_ref, ys_hbm, fbuf, sem.at[0], False)
    o_ref[...] = _rms(x_ref[...] + g2_ref[0] * fbuf[...]) * nw_ref[...]


def _moe_body(elo_ref, ehi_ref, nvalid_ref, xs_ref, wg_lo, wu_lo, wd_lo, wg_hi, wu_hi, wd_hi, ys_ref, *, D):
    j = pl.program_id(0)

    @pl.when(j < nvalid_ref[0])
    def _():
        xb = xs_ref[:, :D].astype(MXU_DTYPE)
        info = xs_ref[:, D:]
        first_is_lo = info[:, REC_E0:REC_E0 + 1] <= info[:, REC_E1:REC_E1 + 1]
        g0, g1 = info[:, REC_G0:REC_G0 + 1], info[:, REC_G1:REC_G1 + 1]
        g_lo = jnp.where(first_is_lo, g0, g1)
        g_hi = jnp.where(first_is_lo, g1, g0)

        def ffn(wg, wu, wd):
            hg = jnp.dot(xb, wg[...], preferred_element_type=F32)
            hu = jnp.dot(xb, wu[...], preferred_element_type=F32)
            return _mm((hg * _sigmoid(hg)) * hu, wd[...])

        ys_ref[...] = g_lo * ffn(wg_lo, wu_lo, wd_lo) + g_hi * ffn(wg_hi, wu_hi, wd_hi)

    @pl.when(j >= nvalid_ref[0])
    def _():
        ys_ref[...] = jnp.zeros_like(ys_ref)


def _moe(layer, n_tok, D, hx, cls_rank, counts, wg, wu, wd, final=None):
    TB = MOE_TILE
    DE = wg.shape[-1]
    chunk = math.gcd(n_tok, PERM_CHUNK)
    if final is not None:
        chunk = math.gcd(chunk, final[3])
    cls = cls_rank[0].astype(jnp.int32)
    rank = cls_rank[1].astype(jnp.int32)
    cnt = counts[:N_CLASSES, 0].astype(jnp.int32)
    padded = (cnt + TB - 1) // TB * TB
    pend = jnp.cumsum(padded)
    pstart = pend - padded
    onehot = cls[:, None] == jnp.arange(N_CLASSES, dtype=jnp.int32)[None, :]
    pos = jnp.sum(jnp.where(onehot, pstart[None, :], 0), axis=1) + rank
    n_blk = -(-n_tok // TB) + N_CLASSES
    P = n_blk * TB
    nvalid = (pend[-1] // TB).astype(jnp.int32).reshape(1)
    blk = jnp.minimum(jnp.arange(n_blk, dtype=jnp.int32), nvalid - 1)
    blk_cls = jnp.minimum(jnp.sum(pend[None, :] <= (blk * TB)[:, None], axis=1), N_CLASSES - 1).astype(jnp.int32)
    pairs = [(u, v) for u in range(EXPERTS_PER_GROUP) for v in range(u + 1, EXPERTS_PER_GROUP)]
    pair_lo = jnp.array([u for u, _ in pairs], jnp.int32)
    pair_hi = jnp.array([v for _, v in pairs], jnp.int32)
    base = (blk_cls // PAIRS_PER_GROUP) * EXPERTS_PER_GROUP
    blk_lo = base + pair_lo[blk_cls % PAIRS_PER_GROUP]
    blk_hi = base + pair_hi[blk_cls % PAIRS_PER_GROUP]
    pos3 = pos.reshape(n_tok // chunk, 1, chunk)
    pos_spec = pl.BlockSpec((1, 1, chunk), lambda i, *_: (i, 0, 0), memory_space=pltpu.SMEM)
    any_spec = pl.BlockSpec(memory_space=pl.ANY)

    xs = pl.pallas_call(
        functools.partial(_dispatch_body, n_blk=n_blk),
        grid_spec=pltpu.PrefetchScalarGridSpec(
            num_scalar_prefetch=3, grid=(n_tok // chunk,),
            in_specs=[pos_spec, pl.BlockSpec((chunk, D + LANES), lambda i, *_: (i, 0))], out_specs=any_spec,
            scratch_shapes=[pltpu.VMEM((TB, D + LANES), F32), pltpu.SemaphoreType.DMA((1,)),
                            pltpu.SemaphoreType.DMA((1,))]),
        out_shape=jax.ShapeDtypeStruct((P, D + LANES), F32),
        compiler_params=_cparams("arbitrary"), name="moe_dispatch",
    )(pend.astype(jnp.int32), padded.astype(jnp.int32), nvalid, pos3, hx)

    w_spec = lambda shape, which: pl.BlockSpec(
        (None, None) + shape,
        (lambda j, elo, ehi, nv: (layer, elo[j], 0, 0)) if which == 0 else (lambda j, elo, ehi, nv: (layer, ehi[j], 0, 0)))
    ys = pl.pallas_call(
        functools.partial(_moe_body, D=D),
        grid_spec=pltpu.PrefetchScalarGridSpec(
            num_scalar_prefetch=3, grid=(n_blk,),
            in_specs=[pl.BlockSpec((TB, D + LANES), lambda j, elo, ehi, nv: (jnp.minimum(j, nv[0] - 1), 0)),
                      w_spec((D, DE), 0), w_spec((D, DE), 0), w_spec((DE, D), 0),
                      w_spec((D, DE), 1), w_spec((D, DE), 1), w_spec((DE, D), 1)],
            out_specs=pl.BlockSpec((TB, D), lambda j, elo, ehi, nv: (j, 0))),
        out_shape=jax.ShapeDtypeStruct((P, D), F32),
        compiler_params=_cparams("arbitrary"), name="moe_experts",
    )(blk_lo, blk_hi, nvalid, xs, wg, wu, wd, wg, wu, wd)

    rows = pl.BlockSpec((chunk, D), lambda i: (i, 0))
    if final is None:
        return pl.pallas_call(
            _combine_body,
            grid_spec=pltpu.PrefetchScalarGridSpec(
                num_scalar_prefetch=0, grid=(n_tok // chunk,), in_specs=[pos_spec, any_spec], out_specs=rows,
                scratch_shapes=[pltpu.SemaphoreType.DMA((1,))]),
            out_shape=jax.ShapeDtypeStruct((n_tok, D), F32),
            compiler_params=_cparams("arbitrary"), name="moe_combine",
        )(pos3, ys)
    x, mods3, g2_base, seq_len, nw = final
    g2_spec = pl.BlockSpec((1, 1, D), lambda i: (g2_base + (i // (seq_len // chunk)) * N_MOD, 0, 0))
    return pl.pallas_call(
        _combine_final_body,
        grid_spec=pltpu.PrefetchScalarGridSpec(
            num_scalar_prefetch=0, grid=(n_tok // chunk,),
            in_specs=[pos_spec, any_spec, rows, g2_spec, pl.BlockSpec((1, D), lambda i: (0, 0))], out_specs=rows,
            scratch_shapes=[pltpu.VMEM((chunk, D), F32), pltpu.SemaphoreType.DMA((1,))]),
        out_shape=jax.ShapeDtypeStruct((n_tok, D), F32),
        compiler_params=_cparams("arbitrary"), name="moe_combine_final",
    )(pos3, ys, x, mods3, nw.reshape(1, D))


def kernel(x, c, ctx, c_ctx, w_in, w_out, ada_w, ada_b, norm1_w, norm2_w, diff_lambda, diff_norm_w, pool_w,
           pool_scale, ret_a_f, ret_a_b, ret_norm_w, router_w, router_b, moe_w_gate, moe_w_up, moe_w_down,
           final_norm_w):
    B, L, D = x.shape
    Lc = ctx.shape[1]
    depth = w_in.shape[0]
    lay = _Layout(B, L, Lc, D)
    assert B + 1 <= MOD_ROWS

    xa = (x.reshape(B * L, D), ctx.reshape(B * Lc, D))
    c_all = jnp.zeros((MOD_ROWS, D), F32).at[:B].set(c).at[B].set(c_ctx)
    mods3 = _ada_table(c_all, ada_w, ada_b).reshape(depth * MOD_ROWS * N_MOD, 1, D)
    tables = _rope_tables(lay)
    rw_pad = jnp.zeros((D, LANES), MXU_DTYPE).at[:, :N_EXPERTS].set(router_w.astype(MXU_DTYPE))
    rb_col = router_b.astype(F32).reshape(N_EXPERTS, 1)
    tri = jnp.tril(jnp.ones((TOKEN_TILE, TOKEN_TILE), F32)).astype(MXU_DTYPE)
    w_in_b, w_out_b = w_in.astype(MXU_DTYPE), w_out.astype(MXU_DTYPE)
    wg_b, wu_b, wd_b = moe_w_gate.astype(MXU_DTYPE), moe_w_up.astype(MXU_DTYPE), moe_w_down.astype(MXU_DTYPE)

    f = None
    for l in range(depth):
        last = l == depth - 1
        lam_init = 0.8 - 0.6 * math.exp(-0.3 * l)
        dl = diff_lambda[l].astype(F32)
        lam = (jnp.exp(jnp.sum(dl[0] * dl[1])) - jnp.exp(jnp.sum(dl[2] * dl[3])) + lam_init).reshape(1)
        p, xa = _inproj(lay, l, xa, f, mods3, norm1_w[l], w_in_b, tables)
        a_lat, a_ctx = _diff_attention(lay, p, lam, diff_norm_w[l], 1.0 - lam_init, not last)
        eye = jnp.eye(len(POOL_WINDOWS), dtype=F32)
        w_bd = (eye[:, None, :, None] * pool_w[l][:, :, None, :]).reshape(POOL_WIDTH, POOL_WIDTH).astype(MXU_DTYPE)
        bp = _pool(lay, p, w_bd, pool_scale[l])
        lg_f = -jnp.exp(ret_a_f[l].astype(F32))
        lg_b = -jnp.exp(ret_a_b[l].astype(F32))
        lanes = lambda lg: jnp.repeat(lg, HEAD_DIM).reshape(1, RET_WIDTH)
        yf, yb = _retention(lay, p, lg_f, lanes(lg_f), lg_b, lanes(lg_b))
        n_tok = lay.NL if last else lay.T
        xa, hx, logits_t = _outproj(lay, l, n_tok, xa, a_lat, a_ctx, bp, yf, yb, p, ret_norm_w[l], w_out_b, mods3,
                                    norm2_w[l], rw_pad)
        hx, cls_rank, counts = _router(n_tok, D, hx, logits_t, rb_col, tri)
        final = (xa, mods3, l * MOD_ROWS * N_MOD + 5, L, final_norm_w) if last else None
        f = _moe(l, n_tok, D, hx, cls_rank, counts, wg_b, wu_b, wd_b, final)
    return f.reshape(B, L, D)
```

```python
import functools
import math

import jax
import jax.numpy as jnp
from jax import lax
from jax.experimental import pallas as pl
from jax.experimental.pallas import tpu as pltpu

F32 = jnp.float32
MXU_DTYPE = jnp.bfloat16
ACT_DTYPE = jnp.bfloat16

GRID_W = 64
HEAD_DIM = 64
DIFF_HEADS = 4
DIFF_V_DIM = 2 * HEAD_DIM
DIFF_WIDTH = DIFF_HEADS * DIFF_V_DIM
POOL_WINDOWS = (2, 4, 8, 16)
POOL_GROUP = 64
POOL_WIDTH = POOL_GROUP * len(POOL_WINDOWS)
RET_HEADS = 4
RET_WIDTH = RET_HEADS * HEAD_DIM
OFF_DQ = 0
OFF_DK = OFF_DQ + DIFF_WIDTH
OFF_DV = OFF_DK + DIFF_WIDTH
OFF_PU = OFF_DV + DIFF_WIDTH
OFF_RQ = OFF_PU + POOL_WIDTH
OFF_RK = OFF_RQ + RET_WIDTH
OFF_RV = OFF_RK + RET_WIDTH
OFF_RG = OFF_RV + RET_WIDTH
IN_WIDTH = OFF_RG + RET_WIDTH
ROPE_BASE = 10000.0
ROPE_FREQS = HEAD_DIM // 4
N_EXPERTS = 16
N_GROUPS = 4
EXPERTS_PER_GROUP = N_EXPERTS // N_GROUPS
PAIRS_PER_GROUP = EXPERTS_PER_GROUP * (EXPERTS_PER_GROUP - 1) // 2
N_CLASSES = N_GROUPS * PAIRS_PER_GROUP
N_MOD = 6
EPS = 1e-6

LANES = 128
SUBLANES = 8
MOD_ROWS = 16
TOKEN_TILE = 512
SEQ_TILE = 256
ATTN_SUBTILES = 4
ATTN_SUB_ROWS = 256
ATTN_KEY_CHUNK = 512
HALO = 16
MOE_TILE = 256
COL = 256
OUTPROJ_SUBTILES = 2
ROUTER_BLOCK = 2048
CLASS_ROWS = 32
PERM_CHUNK = 2048
REC_E0, REC_E1, REC_G0, REC_G1, REC_CLS, REC_RANK = range(6)
VMEM_LIMIT = 56 * 1024 * 1024


def _mm(a, b):
    return jnp.dot(a.astype(MXU_DTYPE), b.astype(MXU_DTYPE), preferred_element_type=F32)


def _mm_nt(a, b):
    return lax.dot_general(a.astype(MXU_DTYPE), b.astype(MXU_DTYPE), (((1,), (1,)), ((), ())),
                           preferred_element_type=F32)


def _mm_tn(a, b):
    return lax.dot_general(a.astype(MXU_DTYPE), b.astype(MXU_DTYPE), (((0,), (0,)), ((), ())),
                           preferred_element_type=F32)


def _split(x):
    hi = x.astype(MXU_DTYPE)
    lo = (x - hi.astype(F32)).astype(MXU_DTYPE)
    return hi, lo


def _mm_hi(a, b):
    ah, al = _split(a)
    bh, bl = _split(b)
    d = lambda u, v: jnp.dot(u, v, preferred_element_type=F32)
    return d(ah, bh) + (d(ah, bl) + d(al, bh))


def _sigmoid(x):
    return 1.0 / (1.0 + jnp.exp(-x))


def _rms(x):
    return x * lax.rsqrt(jnp.mean(x * x, axis=-1, keepdims=True) + EPS)


def _cparams(*sem):
    return pltpu.CompilerParams(dimension_semantics=sem, vmem_limit_bytes=VMEM_LIMIT)


def _ada_body(c_ref, w_ref, b_ref, o_ref):
    c = c_ref[...]
    o_ref[...] = _mm_hi(c * _sigmoid(c), w_ref[...]) + b_ref[...]


def _ada_table(c_all, ada_w, ada_b):
    depth, d, n = ada_w.shape
    nb = n // N_MOD
    return pl.pallas_call(
        _ada_body,
        grid=(depth, n // nb),
        in_specs=[pl.BlockSpec((MOD_ROWS, d), lambda l, j: (0, 0)),
                  pl.BlockSpec((None, d, nb), lambda l, j: (l, 0, j)),
                  pl.BlockSpec((None, 1, nb), lambda l, j: (l, 0, j))],
        out_specs=pl.BlockSpec((None, MOD_ROWS, nb), lambda l, j: (l, 0, j)),
        out_shape=jax.ShapeDtypeStruct((depth, MOD_ROWS, n), F32),
        compiler_params=_cparams("arbitrary", "arbitrary"),
        name="ada_table",
    )(c_all, ada_w, ada_b.reshape(depth, 1, n))


class _Layout:
    def __init__(self, B, L, Lc, D):
        self.B, self.L, self.Lc, self.D = B, L, Lc, D
        self.NL, self.NC = B * L, B * Lc
        self.T = self.NL + self.NC
        assert L % TOKEN_TILE == 0 and self.NC % TOKEN_TILE == 0
        assert L % SEQ_TILE == 0 and Lc == SEQ_TILE and L % GRID_W == 0
        self.n_lat_tiles = self.NL // TOKEN_TILE
        self.tiles_per_seq = L // TOKEN_TILE

    def mod_row(self, i):
        return jnp.where(i < self.n_lat_tiles, i // self.tiles_per_seq, self.B)

    def rope_block(self, i):
        return jnp.where(i < self.n_lat_tiles, i % self.tiles_per_seq, self.tiles_per_seq)

    def mod_spec(self, layer, k):
        base = layer * MOD_ROWS * N_MOD + k
        return pl.BlockSpec((1, 1, self.D), lambda i: (base + self.mod_row(i) * N_MOD, 0, 0))


def _rope_tables(lay):
    L = lay.L
    rows = L // GRID_W
    row = jnp.repeat(jnp.arange(rows), GRID_W).astype(F32)
    col = jnp.tile(jnp.arange(GRID_W), rows).astype(F32)
    inv = ROPE_BASE ** (-jnp.arange(ROPE_FREQS, dtype=F32) / ROPE_FREQS)
    ang_r = row[:, None] * inv
    ang_c = col[:, None] * inv
    ang = jnp.concatenate([ang_r, ang_r, ang_c, ang_c], axis=-1)
    cos, sin = jnp.cos(ang), jnp.sin(ang)
    first_half = (jnp.arange(HEAD_DIM) % (2 * ROPE_FREQS)) < ROPE_FREQS
    sin_a = jnp.where(first_half, -sin, 0.0)
    sin_b = jnp.where(first_half, 0.0, sin)
    rep = LANES // HEAD_DIM
    ident = [jnp.ones((TOKEN_TILE, LANES), F32), jnp.zeros((TOKEN_TILE, LANES), F32),
             jnp.zeros((TOKEN_TILE, LANES), F32)]
    return [jnp.concatenate([jnp.tile(t, (1, rep)), e], axis=0) for t, e in zip((cos, sin_a, sin_b), ident)]


_ROPE_COLS = tuple(range(OFF_DQ // COL, OFF_DV // COL)) + (OFF_RQ // COL, OFF_RK // COL)
LOG2E = 1.4426950408889634
_COL_SCALE = {j: HEAD_DIM ** -0.5 * LOG2E for j in range(OFF_DQ // COL, OFF_DK // COL)}
_COL_SCALE[OFF_RK // COL] = HEAD_DIM ** -0.5


def _inproj_body(*refs, with_f, n_lat_tiles):
    if with_f:
        (x_ref, f_ref, g2_ref, nw_ref, sh_ref, sc_ref, w_ref, cos_ref, sa_ref, sb_ref, p_ref, xo_ref) = refs
        x = x_ref[...] + g2_ref[0] * f_ref[...]
    else:
        (xl_ref, xc_ref, nw_ref, sh_ref, sc_ref, w_ref, cos_ref, sa_ref, sb_ref, p_ref, xo_ref) = refs
        x = jnp.where(pl.program_id(0) < n_lat_tiles, xl_ref[...], xc_ref[...])
    xo_ref[...] = x
    h = (_rms(x) * nw_ref[...]) * (1.0 + sc_ref[0]) + sh_ref[0]
    hb = h.astype(MXU_DTYPE)
    cos, sa, sb = cos_ref[...], sa_ref[...], sb_ref[...]
    for j in range(IN_WIDTH // COL):
        acc = jnp.dot(hb, w_ref[:, j * COL:(j + 1) * COL], preferred_element_type=F32)
        if j in _ROPE_COLS:
            parts = []
            for t in range(COL // LANES):
                a = acc[:, t * LANES:(t + 1) * LANES]
                parts.append(a * cos + pltpu.roll(a, LANES - ROPE_FREQS, 1) * sa + pltpu.roll(a, ROPE_FREQS, 1) * sb)
            acc = jnp.concatenate(parts, axis=1)
        if j in _COL_SCALE:
            acc = acc * _COL_SCALE[j]
        p_ref[:, j * COL:(j + 1) * COL] = acc.astype(p_ref.dtype)


def _inproj(lay, layer, x, f, mods3, nw, w_in, tables):
    D, TM = lay.D, TOKEN_TILE
    row = lambda i: (i, 0)
    const = lambda i: (0, 0)
    n_lat = lay.n_lat_tiles
    tbl = pl.BlockSpec((TM, LANES), lambda i: (lay.rope_block(i), 0))
    if f is not None:
        in_specs = [pl.BlockSpec((TM, D), row), pl.BlockSpec((TM, D), row), lay.mod_spec(layer - 1, 5)]
        args = [x, f, mods3]
    else:
        in_specs = [pl.BlockSpec((TM, D), lambda i: (jnp.minimum(i, n_lat - 1), 0)),
                    pl.BlockSpec((TM, D), lambda i: (jnp.maximum(i - n_lat, 0), 0))]
        args = list(x)
    in_specs += [pl.BlockSpec((1, D), const), lay.mod_spec(layer, 0), lay.mod_spec(layer, 1),
                 pl.BlockSpec((None, D, IN_WIDTH), lambda i: (layer, 0, 0)), tbl, tbl, tbl]
    args += [nw.reshape(1, D), mods3, mods3, w_in] + list(tables)
    return pl.pallas_call(
        functools.partial(_inproj_body, with_f=f is not None, n_lat_tiles=n_lat),
        grid=(lay.T // TM,), in_specs=in_specs,
        out_specs=[pl.BlockSpec((TM, IN_WIDTH), row), pl.BlockSpec((TM, D), row)],
        out_shape=[jax.ShapeDtypeStruct((lay.T, IN_WIDTH), ACT_DTYPE), jax.ShapeDtypeStruct((lay.T, D), F32)],
        compiler_params=_cparams("arbitrary"), name="inproj",
    )(*args)


def _map_masks(q):
    lane = lax.broadcasted_iota(jnp.int32, (1, DIFF_V_DIM), 1)
    zero = jnp.zeros_like(q)
    return [jnp.where(lane < HEAD_DIM, q, zero), jnp.where(lane >= HEAD_DIM, q, zero)]


def _scores(qm, k_refs):
    chunks, run = [], None
    for k in k_refs:
        for c0 in range(0, k.shape[0], ATTN_KEY_CHUNK):
            s = _mm_nt(qm, k[c0:c0 + ATTN_KEY_CHUNK, :])
            chunks.append(s)
            for t in range(s.shape[1] // LANES):
                piece = s[:, t * LANES:(t + 1) * LANES]
                run = piece if run is None else jnp.maximum(run, piece)
    return chunks, jnp.max(run, axis=-1, keepdims=True)


def _softmax_diff(maps, lam):
    es, ls = [], []
    for chunks, mx in maps:
        e = [jnp.exp2(t - mx) for t in chunks]
        ls.append(functools.reduce(jnp.add, [jnp.sum(t, axis=-1, keepdims=True) for t in e]))
        es.append([t.astype(MXU_DTYPE) for t in e])
    c = (lam * ls[0] / ls[1]).astype(MXU_DTYPE)
    return [e1 - e2 * c for e1, e2 in zip(es[0], es[1])], 1.0 / ls[0]


def _head_norm(o, w, post_scale):
    return (_rms(o) * w) * post_scale


def _attn_body(lam_ref, q_ref, *refs, n_seg, n_sub, post_scale):
    k_refs, v_refs = refs[:n_seg], refs[n_seg:2 * n_seg]
    w_ref, o_ref = refs[2 * n_seg], refs[2 * n_seg + 1]
    rows = q_ref.shape[0] // n_sub

    def finish(u, maps):
        p, r1 = _softmax_diff(maps, lam_ref[0])
        o, at = None, 0
        for v in v_refs:
            n_c = -(-v.shape[0] // ATTN_KEY_CHUNK)
            part = jnp.dot(jnp.concatenate(p[at:at + n_c], axis=1), v[...], preferred_element_type=F32)
            o = part if o is None else o + part
            at += n_c
        o_ref[u * rows:(u + 1) * rows, :] = _head_norm(o * r1, w_ref[...], post_scale).astype(o_ref.dtype)

    scores = [[_scores(qm, k_refs) for qm in _map_masks(q_ref[u * rows:(u + 1) * rows, :])] for u in range(n_sub)]
    for u, maps in enumerate(scores):
        finish(u, maps)


def _diff_attention(lay, p, lam, norm_w, post_scale, need_ctx):
    B, L, Lc, H = lay.B, lay.L, lay.Lc, DIFF_HEADS
    W = DIFF_V_DIM
    tq = ATTN_SUBTILES * ATTN_SUB_ROWS
    assert L % tq == 0
    nq = L // tq
    k_blk, v_blk = OFF_DK // W, OFF_DV // W
    ctx0 = lay.NL // Lc
    lam_spec = pl.BlockSpec(memory_space=pltpu.SMEM)
    w_spec = pl.BlockSpec((1, W), lambda *_: (0, 0))
    a_lat = pl.pallas_call(
        functools.partial(_attn_body, n_seg=2, n_sub=ATTN_SUBTILES, post_scale=post_scale),
        grid=(B, H, nq),
        in_specs=[lam_spec,
                  pl.BlockSpec((tq, W), lambda b, h, i: (b * nq + i, h)),
                  pl.BlockSpec((Lc, W), lambda b, h, i: (ctx0 + b, k_blk + h)),
                  pl.BlockSpec((L, W), lambda b, h, i: (b, k_blk + h)),
                  pl.BlockSpec((Lc, W), lambda b, h, i: (ctx0 + b, v_blk + h)),
                  pl.BlockSpec((L, W), lambda b, h, i: (b, v_blk + h)),
                  w_spec],
        out_specs=pl.BlockSpec((tq, W), lambda b, h, i: (b * nq + i, h)),
        out_shape=jax.ShapeDtypeStruct((lay.NL, DIFF_WIDTH), ACT_DTYPE),
        compiler_params=_cparams("arbitrary", "arbitrary", "arbitrary"),
        name="diff_attn",
    )(lam, p, p, p, p, p, norm_w.reshape(1, W))
    if not need_ctx:
        return a_lat, None
    a_ctx = pl.pallas_call(
        functools.partial(_attn_body, n_seg=1, n_sub=1, post_scale=post_scale),
        grid=(B, H),
        in_specs=[lam_spec,
                  pl.BlockSpec((Lc, W), lambda b, h: (ctx0 + b, h)),
                  pl.BlockSpec((Lc, W), lambda b, h: (ctx0 + b, k_blk + h)),
                  pl.BlockSpec((Lc, W), lambda b, h: (ctx0 + b, v_blk + h)),
                  w_spec],
        out_specs=pl.BlockSpec((Lc, W), lambda b, h: (b, h)),
        out_shape=jax.ShapeDtypeStruct((lay.NC, DIFF_WIDTH), ACT_DTYPE),
        compiler_params=_cparams("arbitrary", "arbitrary"),
        name="diff_attn_ctx",
    )(lam, p, p, p, norm_w.reshape(1, W))
    return a_lat, a_ctx


def _pool_body(prev_ref, cur_ref, next_ref, w_ref, scale_ref, o_ref, *, lay):
    i = pl.program_id(0)
    n_lat = lay.NL // SEQ_TILE
    per_seq = jnp.where(i < n_lat, lay.L // SEQ_TILE, lay.Lc // SEQ_TILE)
    idx = jnp.where(i < n_lat, i, i - n_lat) % per_seq
    seq_len = per_seq * SEQ_TILE
    has_prev = (idx > 0).astype(F32)
    has_next = (idx < per_seq - 1).astype(F32)
    cur = cur_ref[...].astype(F32)
    u = jnp.concatenate([prev_ref[...].astype(F32) * has_prev, cur, next_ref[...].astype(F32) * has_next], axis=0)
    sums = {1: u}
    w = 1
    while w < POOL_WINDOWS[-1]:
        s = sums[w]
        m = s.shape[0] - w
        sums[2 * w] = s[:m] + s[w:w + m]
        w *= 2
    pos = idx * SEQ_TILE + lax.broadcasted_iota(jnp.int32, (SEQ_TILE, 1), 0)
    group = lax.broadcasted_iota(jnp.int32, (1, POOL_WIDTH), 1) // POOL_GROUP
    mean = jnp.zeros((SEQ_TILE, POOL_WIDTH), F32)
    for g, win in enumerate(POOL_WINDOWS):
        start = HALO - win // 2
        cnt = jnp.minimum(pos + (win - win // 2), seq_len) - jnp.maximum(pos - win // 2, 0)
        mean = jnp.where(group == g, sums[win][start:start + SEQ_TILE] / cnt.astype(F32), mean)
    o_ref[...] = (_mm(mean - cur, w_ref[...]) * scale_ref[...]).astype(o_ref.dtype)


def _pool(lay, p, w_bd, scale):
    n = lay.T // SEQ_TILE
    col = OFF_PU // POOL_WIDTH
    per = SEQ_TILE // HALO
    last = lay.T // HALO - 1
    return pl.pallas_call(
        functools.partial(_pool_body, lay=lay),
        grid=(n,),
        in_specs=[pl.BlockSpec((HALO, POOL_WIDTH), lambda i: (jnp.maximum(i * per - 1, 0), col)),
                  pl.BlockSpec((SEQ_TILE, POOL_WIDTH), lambda i: (i, col)),
                  pl.BlockSpec((HALO, POOL_WIDTH), lambda i: (jnp.minimum((i + 1) * per, last), col)),
                  pl.BlockSpec((POOL_WIDTH, POOL_WIDTH), lambda i: (0, 0)),
                  pl.BlockSpec((1, POOL_WIDTH), lambda i: (0, 0))],
        out_specs=pl.BlockSpec((SEQ_TILE, POOL_WIDTH), lambda i: (i, 0)),
        out_shape=jax.ShapeDtypeStruct((lay.T, POOL_WIDTH), ACT_DTYPE),
        compiler_params=_cparams("arbitrary"), name="pool",
    )(p, p, p, w_bd, scale.reshape(1, POOL_WIDTH))


def _ret_tables(lgs_ref, lgv, d_ref, xi_ref, zeta_ref, reverse):
    C = SEQ_TILE
    ri = lax.broadcasted_iota(jnp.int32, (C, C), 0)
    ci = lax.broadcasted_iota(jnp.int32, (C, C), 1)
    dist = (ci - ri if reverse else ri - ci).astype(F32)
    keep = dist > 0 if reverse else dist >= 0
    for h in range(RET_HEADS):
        d_ref[h * C:(h + 1) * C, :] = jnp.where(keep, jnp.exp(lgs_ref[h] * jnp.maximum(dist, 0.0)), 0.0)
    t = lax.broadcasted_iota(jnp.int32, (C, 1), 0).astype(F32)
    xi_ref[...] = jnp.exp(lgv * ((C - t) if reverse else (t + 1.0)))
    zeta_ref[...] = jnp.exp(lgv * (t if reverse else (C - 1.0 - t)))


def _ret_chunk(q, k, v, lgv, s_ref, d_ref, xi_ref, zeta_ref):
    C = SEQ_TILE
    lane_head = lax.broadcasted_iota(jnp.int32, (1, RET_WIDTH), 1) // HEAD_DIM
    row_head = lax.broadcasted_iota(jnp.int32, (RET_WIDTH, 1), 0) // HEAD_DIM
    zero = jnp.zeros_like(q)
    qs = jnp.concatenate([jnp.where(lane_head == h, q, zero) for h in range(RET_HEADS)], axis=0)
    sd = _mm_nt(qs, k) * d_ref[...]
    yv = _mm(sd, v)
    y = functools.reduce(jnp.add, [jnp.where(lane_head == h, yv[h * C:(h + 1) * C], 0.0) for h in range(RET_HEADS)])
    state = s_ref[...]
    y = y + _mm(q.astype(F32) * xi_ref[...], state)
    kv = _mm_tn(k.astype(F32) * zeta_ref[...], v)
    s_ref[...] = state * jnp.exp(lgv * C) + jnp.where(row_head == lane_head, kv, 0.0)
    return y


def _ret_body(lgs_f, lgv_f, lgs_b, lgv_b, qf, kf, vf, qb, kb, vb, yf_ref, yb_ref,
              s_f, d_f, xi_f, zeta_f, s_b, d_b, xi_b, zeta_b):
    b, j = pl.program_id(0), pl.program_id(1)

    @pl.when((b == 0) & (j == 0))
    def _():
        _ret_tables(lgs_f, lgv_f[...], d_f, xi_f, zeta_f, False)
        _ret_tables(lgs_b, lgv_b[...], d_b, xi_b, zeta_b, True)

    @pl.when(j == 0)
    def _():
        s_f[...] = jnp.zeros_like(s_f)
        s_b[...] = jnp.zeros_like(s_b)

    yf_ref[...] = _ret_chunk(qf[...], kf[...], vf[...], lgv_f[...], s_f, d_f, xi_f, zeta_f)
    yb_ref[...] = _ret_chunk(qb[...], kb[...], vb[...], lgv_b[...], s_b, d_b, xi_b, zeta_b)


def _retention(lay, p, lgs_f, lgv_f, lgs_b, lgv_b):
    B, L = lay.B, lay.L
    C, W = SEQ_TILE, RET_WIDTH
    nch = L // C
    ctx0 = lay.NL // C
    rows_f = lambda b, j: jnp.where(j == 0, ctx0 + b, b * nch + j - 1)
    rows_b = lambda b, j: jnp.where(j == 0, ctx0 + b, b * nch + nch - j)
    spec = lambda rows, c: pl.BlockSpec((C, W), lambda b, j: (rows(b, j), c))
    smem = pl.BlockSpec(memory_space=pltpu.SMEM)
    vec = pl.BlockSpec((1, W), lambda b, j: (0, 0))
    cols = (OFF_RQ // W, OFF_RK // W, OFF_RV // W)
    tables = [pltpu.VMEM((W, W), F32), pltpu.VMEM((RET_HEADS * C, C), F32), pltpu.VMEM((C, W), F32), pltpu.VMEM((C, W), F32)]
    return pl.pallas_call(
        _ret_body,
        grid=(B, nch + 1),
        in_specs=[smem, vec, smem, vec] + [spec(rows_f, c) for c in cols] + [spec(rows_b, c) for c in cols],
        out_specs=[spec(rows_f, 0), spec(rows_b, 0)],
        out_shape=[jax.ShapeDtypeStruct((lay.T, W), F32), jax.ShapeDtypeStruct((lay.T, W), F32)],
        scratch_shapes=tables + tables,
        compiler_params=_cparams("arbitrary", "arbitrary"),
        name="retention",
    )(lgs_f, lgv_f, lgs_b, lgv_b, p, p, p, p, p, p)


def _retention_output(yf, yb, gate, w):
    y = yf + yb
    lane_head = lax.broadcasted_iota(jnp.int32, (1, RET_WIDTH), 1) // HEAD_DIM
    row_head = lax.broadcasted_iota(jnp.int32, (RET_WIDTH, 1), 0) // HEAD_DIM
    same = (row_head == lane_head).astype(F32) * (1.0 / HEAD_DIM)
    ms = _mm_hi(y * y, same)
    g = gate.astype(F32)
    return (g * _sigmoid(g)) * (y * lax.rsqrt(ms + EPS) * w)


def _top2(vals):
    n = len(vals)
    v1 = functools.reduce(jnp.maximum, vals)
    i1 = jnp.full_like(v1, n - 1)
    for e in range(n - 2, -1, -1):
        i1 = jnp.where(vals[e] == v1, float(e), i1)
    rest = [jnp.where(i1 == float(e), -jnp.inf, vals[e]) for e in range(n)]
    v2 = functools.reduce(jnp.maximum, rest)
    i2 = jnp.full_like(v1, n - 1)
    for e in range(n - 2, -1, -1):
        i2 = jnp.where(rest[e] == v2, float(e), i2)
    return v1, i1, v2, i2


def _route(logits_t, bias):
    s = _sigmoid(logits_t)
    sel = s + bias
    groups = []
    for g in range(N_GROUPS):
        rows = [sel[g * EXPERTS_PER_GROUP + e:g * EXPERTS_PER_GROUP + e + 1] for e in range(EXPERTS_PER_GROUP)]
        groups.append(_top2(rows))
    score = [v1 + v2 for v1, _, v2, _ in groups]
    best = functools.reduce(jnp.maximum, score)
    gi = jnp.full_like(best, N_GROUPS - 1)
    for g in range(N_GROUPS - 2, -1, -1):
        gi = jnp.where(score[g] == best, float(g), gi)
    pick = lambda k: functools.reduce(
        lambda acc, g: jnp.where(gi == float(g), groups[g][k], acc), range(N_GROUPS - 1), groups[N_GROUPS - 1][k])
    e0 = gi * EXPERTS_PER_GROUP + pick(1)
    e1 = gi * EXPERTS_PER_GROUP + pick(3)
    s0 = jnp.zeros_like(best)
    s1 = jnp.zeros_like(best)
    for e in range(N_EXPERTS):
        s0 = jnp.where(e0 == float(e), s[e:e + 1], s0)
        s1 = jnp.where(e1 == float(e), s[e:e + 1], s1)
    tot = s0 + s1
    return e0, e1, s0 / tot, s1 / tot


def _outproj_body(*refs, lay, with_ctx):
    if with_ctx:
        (x_ref, al_ref, ac_ref, b_ref, yf_ref, yb_ref, rg_ref, rnw_ref, w_ref, g1_ref, nw_ref, sh_ref, sc_ref, rw_ref,
         xo_ref, hx_ref, lt_ref) = refs
        a = jnp.where(pl.program_id(0) < lay.n_lat_tiles, al_ref[...], ac_ref[...])
    else:
        (x_ref, al_ref, b_ref, yf_ref, yb_ref, rg_ref, rnw_ref, w_ref, g1_ref, nw_ref, sh_ref, sc_ref, rw_ref,
         xo_ref, hx_ref, lt_ref) = refs
        a = al_ref[...]
    D = lay.D
    half = TOKEN_TILE // OUTPROJ_SUBTILES
    for u in range(OUTPROJ_SUBTILES):
        rows = slice(u * half, (u + 1) * half)
        r = _retention_output(yf_ref[rows, :], yb_ref[rows, :], rg_ref[rows, :], rnw_ref[...])
        y = (_mm(a[rows], w_ref[:DIFF_WIDTH, :])
             + _mm(b_ref[rows, :], w_ref[DIFF_WIDTH:DIFF_WIDTH + POOL_WIDTH, :])
             + _mm(r, w_ref[DIFF_WIDTH + POOL_WIDTH:, :]))
        x = x_ref[rows, :] + g1_ref[0] * y
        xo_ref[rows, :] = x
        h = (_rms(x) * nw_ref[...]) * (1.0 + sc_ref[0]) + sh_ref[0]
        hx_ref[rows, :D] = h
        hx_ref[rows, D:] = jnp.zeros((half, LANES), F32)
        lt_ref[:, rows] = jnp.transpose(_mm(h, rw_ref[...]))[:N_EXPERTS]


def _outproj(lay, layer, n_tok, x, a_lat, a_ctx, b, yf, yb, p, ret_nw, w_out, mods3, nw, rw_pad):
    D, TM = lay.D, TOKEN_TILE
    row = lambda i: (i, 0)
    const = lambda i: (0, 0)
    tile = lambda w: pl.BlockSpec((TM, w), row)
    n_lat = lay.n_lat_tiles
    in_specs = [tile(D), pl.BlockSpec((TM, DIFF_WIDTH), lambda i: (jnp.minimum(i, n_lat - 1), 0))]
    args = [x, a_lat]
    if a_ctx is not None:
        in_specs.append(pl.BlockSpec((TM, DIFF_WIDTH), lambda i: (jnp.maximum(i - n_lat, 0), 0)))
        args.append(a_ctx)
    in_specs += [tile(POOL_WIDTH), tile(RET_WIDTH), tile(RET_WIDTH),
                 pl.BlockSpec((TM, RET_WIDTH), lambda i: (i, OFF_RG // RET_WIDTH)), pl.BlockSpec((1, RET_WIDTH), const),
                 pl.BlockSpec((None, DIFF_WIDTH + POOL_WIDTH + RET_WIDTH, D), lambda i: (layer, 0, 0)),
                 lay.mod_spec(layer, 2), pl.BlockSpec((1, D), const), lay.mod_spec(layer, 3), lay.mod_spec(layer, 4),
                 pl.BlockSpec((D, LANES), const)]
    args += [b, yf, yb, p, ret_nw.reshape(1, RET_WIDTH), w_out, mods3, nw.reshape(1, D), mods3, mods3, rw_pad]
    return pl.pallas_call(
        functools.partial(_outproj_body, lay=lay, with_ctx=a_ctx is not None),
        grid=(n_tok // TM,), in_specs=in_specs,
        out_specs=[tile(D), tile(D + LANES), pl.BlockSpec((N_EXPERTS, TM), lambda i: (0, i))],
        out_shape=[jax.ShapeDtypeStruct((n_tok, D), F32), jax.ShapeDtypeStruct((n_tok, D + LANES), F32),
                   jax.ShapeDtypeStruct((N_EXPERTS, n_tok), F32)],
        compiler_params=_cparams("arbitrary"), name="outproj",
    )(*args)


def _router_body(lt_ref, rb_ref, tri_ref, hx_any, rec_ref, cr_ref, cnt_ref, run_ref):
    del hx_any
    n = lt_ref.shape[1]
    e0, e1, g0, g1 = _route(lt_ref[...], rb_ref[...])
    lo, hi = jnp.minimum(e0, e1), jnp.maximum(e0, e1)
    grp = jnp.floor(lo * (1.0 / EXPERTS_PER_GROUP))
    pa, pb = lo - grp * EXPERTS_PER_GROUP, hi - grp * EXPERTS_PER_GROUP
    cls = grp * PAIRS_PER_GROUP + pa * ((2 * EXPERTS_PER_GROUP - 1) - pa) * 0.5 + (pb - pa - 1.0)

    @pl.when(pl.program_id(0) == 0)
    def _():
        run_ref[...] = jnp.zeros_like(run_ref)

    run = run_ref[:, :1]
    onehot = jnp.where(cls == lax.broadcasted_iota(jnp.int32, (CLASS_ROWS, 1), 0).astype(F32), 1.0, 0.0)
    tri = tri_ref[...]
    ranks = []
    for c0 in range(0, n, TOKEN_TILE):
        oh = onehot[:, c0:c0 + TOKEN_TILE]
        incl = _mm_nt(oh, tri)
        ranks.append(jnp.sum(oh * (incl + run), axis=0, keepdims=True) - 1.0)
        run = run + incl[:, TOKEN_TILE - 1:TOKEN_TILE]
    rank = jnp.concatenate(ranks, axis=1)
    run_ref[...] = jnp.broadcast_to(run, run_ref.shape)
    cnt_ref[...] = jnp.broadcast_to(run, cnt_ref.shape)
    cr_ref[...] = jnp.concatenate([cls, rank, jnp.zeros((6, n), F32)], axis=0)
    rec_ref[...] = jnp.transpose(jnp.concatenate([e0, e1, g0, g1, cls, rank, jnp.zeros((LANES - 6, n), F32)], axis=0))


def _router(n_tok, D, hx, logits_t, rb_col, tri):
    blk = math.gcd(n_tok, ROUTER_BLOCK)
    return pl.pallas_call(
        _router_body,
        grid=(n_tok // blk,),
        in_specs=[pl.BlockSpec((N_EXPERTS, blk), lambda i: (0, i)), pl.BlockSpec((N_EXPERTS, 1), lambda i: (0, 0)),
                  pl.BlockSpec((TOKEN_TILE, TOKEN_TILE), lambda i: (0, 0)), pl.BlockSpec(memory_space=pl.ANY)],
        out_specs=[pl.BlockSpec((blk, LANES), lambda i: (i, D // LANES)), pl.BlockSpec((8, blk), lambda i: (0, i)),
                   pl.BlockSpec((CLASS_ROWS, LANES), lambda i: (0, 0))],
        out_shape=[jax.ShapeDtypeStruct((n_tok, D + LANES), F32), jax.ShapeDtypeStruct((8, n_tok), F32),
                   jax.ShapeDtypeStruct((CLASS_ROWS, LANES), F32)],
        scratch_shapes=[pltpu.VMEM((CLASS_ROWS, LANES), F32)],
        input_output_aliases={3: 0},
        compiler_params=_cparams("arbitrary"), name="router",
    )(logits_t, rb_col, tri, hx)


def _row_copies(idx_ref, hbm, vmem, sem, scatter):
    groups = vmem.shape[0]

    def copy(g, k, row):
        src, dst = (vmem.at[g, pl.ds(k, 1)], hbm.at[pl.ds(row, 1)])
        return pltpu.make_async_copy(src, dst, sem) if scatter else pltpu.make_async_copy(dst, src, sem)

    def start(g, carry):
        for k in range(SUBLANES):
            copy(g, k, idx_ref[0, 0, g * SUBLANES + k]).start()
        return carry

    def wait(g, carry):
        for k in range(SUBLANES):
            copy(0, 0, 0).wait()
        return carry

    lax.fori_loop(0, groups, start, 0)
    lax.fori_loop(0, groups, wait, 0)


def _dispatch_body(pend_ref, padded_ref, nvalid_ref, pos_ref, hx_ref, xs_hbm, zbuf, zsem, sem, *, n_blk):
    TB = MOE_TILE
    i = pl.program_id(0)

    @pl.when(i == 0)
    def _():
        zbuf[...] = jnp.zeros_like(zbuf)
        fill = lambda row: pltpu.make_async_copy(zbuf, xs_hbm.at[pl.ds(row, TB)], zsem.at[0])
        for k in range(N_CLASSES):
            @pl.when(padded_ref[k] > 0)
            def _():
                fill(pl.multiple_of(pend_ref[k] - TB, TB)).start()
        for j in range(n_blk):
            @pl.when(j >= nvalid_ref[0])
            def _():
                fill(j * TB).start()
        for k in range(N_CLASSES):
            @pl.when(padded_ref[k] > 0)
            def _():
                fill(0).wait()
        for j in range(n_blk):
            @pl.when(j >= nvalid_ref[0])
            def _():
                fill(0).wait()

    _row_copies(pos_ref, xs_hbm, hx_ref, sem.at[0], True)


def _combine_body(pos_ref, ys_hbm, f_ref, sem):
    _row_copies(pos_ref, ys_hbm, f_ref, sem.at[0], False)


def _combine_final_body(pos_ref, ys_hbm, x_ref, g2_ref, nw_ref, o_ref, fbuf, sem):
    _row_copies(pos_ref, ys_hbm, fbuf, sem.at[0], False)
    o_ref[...] = _rms(x_ref[...] + g2_ref[0] * fbuf[...]) * nw_ref[...]


def _moe_body(elo_ref, ehi_ref, nvalid_ref, xs_ref, wg_lo, wu_lo, wd_lo, wg_hi, wu_hi, wd_hi, ys_ref, *, D):
    j = pl.program_id(0)

    @pl.when(j < nvalid_ref[0])
    def _():
        xb = xs_ref[:, :D].astype(MXU_DTYPE)
        info = xs_ref[:, D:]
        first_is_lo = info[:, REC_E0:REC_E0 + 1] <= info[:, REC_E1:REC_E1 + 1]
        g0, g1 = info[:, REC_G0:REC_G0 + 1], info[:, REC_G1:REC_G1 + 1]
        g_lo = jnp.where(first_is_lo, g0, g1)
        g_hi = jnp.where(first_is_lo, g1, g0)

        def ffn(wg, wu, wd):
            hg = jnp.dot(xb, wg[...], preferred_element_type=F32)
            hu = jnp.dot(xb, wu[...], preferred_element_type=F32)
            return _mm((hg * _sigmoid(hg)) * hu, wd[...])

        ys_ref[...] = g_lo * ffn(wg_lo, wu_lo, wd_lo) + g_hi * ffn(wg_hi, wu_hi, wd_hi)

    @pl.when(j >= nvalid_ref[0])
    def _():
        ys_ref[...] = jnp.zeros_like(ys_ref)


def _moe(layer, n_tok, D, hx, cls_rank, counts, wg, wu, wd, final=None):
    TB = MOE_TILE
    DE = wg.shape[-1]
    chunk = math.gcd(n_tok, PERM_CHUNK)
    if final is not None:
        chunk = math.gcd(chunk, final[3])
    cls = cls_rank[0].astype(jnp.int32)
    rank = cls_rank[1].astype(jnp.int32)
    cnt = counts[:N_CLASSES, 0].astype(jnp.int32)
    padded = (cnt + TB - 1) // TB * TB
    pend = jnp.cumsum(padded)
    pstart = pend - padded
    onehot = cls[:, None] == jnp.arange(N_CLASSES, dtype=jnp.int32)[None, :]
    pos = jnp.sum(jnp.where(onehot, pstart[None, :], 0), axis=1) + rank
    n_blk = -(-n_tok // TB) + N_CLASSES
    P = n_blk * TB
    nvalid = (pend[-1] // TB).astype(jnp.int32).reshape(1)
    blk = jnp.minimum(jnp.arange(n_blk, dtype=jnp.int32), nvalid - 1)
    blk_cls = jnp.minimum(jnp.sum(pend[None, :] <= (blk * TB)[:, None], axis=1), N_CLASSES - 1).astype(jnp.int32)
    pairs = [(u, v) for u in range(EXPERTS_PER_GROUP) for v in range(u + 1, EXPERTS_PER_GROUP)]
    pair_lo = jnp.array([u for u, _ in pairs], jnp.int32)
    pair_hi = jnp.array([v for _, v in pairs], jnp.int32)
    base = (blk_cls // PAIRS_PER_GROUP) * EXPERTS_PER_GROUP
    blk_lo = base + pair_lo[blk_cls % PAIRS_PER_GROUP]
    blk_hi = base + pair_hi[blk_cls % PAIRS_PER_GROUP]
    pos3 = pos.reshape(n_tok // chunk, 1, chunk)
    pos_spec = pl.BlockSpec((1, 1, chunk), lambda i, *_: (i, 0, 0), memory_space=pltpu.SMEM)
    any_spec = pl.BlockSpec(memory_space=pl.ANY)

    xs = pl.pallas_call(
        functools.partial(_dispatch_body, n_blk=n_blk),
        grid_spec=pltpu.PrefetchScalarGridSpec(
            num_scalar_prefetch=3, grid=(n_tok // chunk,),
            in_specs=[pos_spec, pl.BlockSpec((chunk // SUBLANES, SUBLANES, D + LANES), lambda i, *_: (i, 0, 0))],
            out_specs=any_spec,
            scratch_shapes=[pltpu.VMEM((TB, D + LANES), F32), pltpu.SemaphoreType.DMA((1,)),
                            pltpu.SemaphoreType.DMA((1,))]),
        out_shape=jax.ShapeDtypeStruct((P, D + LANES), F32),
        compiler_params=_cparams("arbitrary"), name="moe_dispatch",
    )(pend.astype(jnp.int32), padded.astype(jnp.int32), nvalid, pos3, hx.reshape(n_tok // SUBLANES, SUBLANES, D + LANES))

    w_spec = lambda shape, which: pl.BlockSpec(
        (None, None) + shape,
        (lambda j, elo, ehi, nv: (layer, elo[j], 0, 0)) if which == 0 else (lambda j, elo, ehi, nv: (layer, ehi[j], 0, 0)))
    ys = pl.pallas_call(
        functools.partial(_moe_body, D=D),
        grid_spec=pltpu.PrefetchScalarGridSpec(
            num_scalar_prefetch=3, grid=(n_blk,),
            in_specs=[pl.BlockSpec((TB, D + LANES), lambda j, elo, ehi, nv: (jnp.minimum(j, nv[0] - 1), 0)),
                      w_spec((D, DE), 0), w_spec((D, DE), 0), w_spec((DE, D), 0),
                      w_spec((D, DE), 1), w_spec((D, DE), 1), w_spec((DE, D), 1)],
            out_specs=pl.BlockSpec((TB, D), lambda j, elo, ehi, nv: (j, 0))),
        out_shape=jax.ShapeDtypeStruct((P, D), F32),
        compiler_params=_cparams("arbitrary"), name="moe_experts",
    )(blk_lo, blk_hi, nvalid, xs, wg, wu, wd, wg, wu, wd)

    rows = pl.BlockSpec((chunk // SUBLANES, SUBLANES, D), lambda i: (i, 0, 0))
    grouped = jax.ShapeDtypeStruct((n_tok // SUBLANES, SUBLANES, D), F32)
    if final is None:
        return pl.pallas_call(
            _combine_body,
            grid_spec=pltpu.PrefetchScalarGridSpec(
                num_scalar_prefetch=0, grid=(n_tok // chunk,), in_specs=[pos_spec, any_spec], out_specs=rows,
                scratch_shapes=[pltpu.SemaphoreType.DMA((1,))]),
            out_shape=grouped,
            compiler_params=_cparams("arbitrary"), name="moe_combine",
        )(pos3, ys).reshape(n_tok, D)
    x, mods3, g2_base, seq_len, nw = final
    g2_spec = pl.BlockSpec((1, 1, D), lambda i: (g2_base + (i // (seq_len // chunk)) * N_MOD, 0, 0))
    return pl.pallas_call(
        _combine_final_body,
        grid_spec=pltpu.PrefetchScalarGridSpec(
            num_scalar_prefetch=0, grid=(n_tok // chunk,),
            in_specs=[pos_spec, any_spec, rows, g2_spec, pl.BlockSpec((1, D), lambda i: (0, 0))], out_specs=rows,
            scratch_shapes=[pltpu.VMEM((chunk // SUBLANES, SUBLANES, D), F32), pltpu.SemaphoreType.DMA((1,))]),
        out_shape=grouped,
        compiler_params=_cparams("arbitrary"), name="moe_combine_final",
    )(pos3, ys, x.reshape(n_tok // SUBLANES, SUBLANES, D), mods3, nw.reshape(1, D)).reshape(n_tok, D)


def kernel(x, c, ctx, c_ctx, w_in, w_out, ada_w, ada_b, norm1_w, norm2_w, diff_lambda, diff_norm_w, pool_w,
           pool_scale, ret_a_f, ret_a_b, ret_norm_w, router_w, router_b, moe_w_gate, moe_w_up, moe_w_down,
           final_norm_w):
    B, L, D = x.shape
    Lc = ctx.shape[1]
    depth = w_in.shape[0]
    lay = _Layout(B, L, Lc, D)
    assert B + 1 <= MOD_ROWS

    xa = (x.reshape(B * L, D), ctx.reshape(B * Lc, D))
    c_all = jnp.zeros((MOD_ROWS, D), F32).at[:B].set(c).at[B].set(c_ctx)
    mods3 = _ada_table(c_all, ada_w, ada_b).reshape(depth * MOD_ROWS * N_MOD, 1, D)
    tables = _rope_tables(lay)
    rw_pad = jnp.zeros((D, LANES), MXU_DTYPE).at[:, :N_EXPERTS].set(router_w.astype(MXU_DTYPE))
    rb_col = router_b.astype(F32).reshape(N_EXPERTS, 1)
    tri = jnp.tril(jnp.ones((TOKEN_TILE, TOKEN_TILE), F32)).astype(MXU_DTYPE)
    w_in_b, w_out_b = w_in.astype(MXU_DTYPE), w_out.astype(MXU_DTYPE)
    wg_b, wu_b, wd_b = moe_w_gate.astype(MXU_DTYPE), moe_w_up.astype(MXU_DTYPE), moe_w_down.astype(MXU_DTYPE)

    f = None
    for l in range(depth):
        last = l == depth - 1
        lam_init = 0.8 - 0.6 * math.exp(-0.3 * l)
        dl = diff_lambda[l].astype(F32)
        lam = (jnp.exp(jnp.sum(dl[0] * dl[1])) - jnp.exp(jnp.sum(dl[2] * dl[3])) + lam_init).reshape(1)
        p, xa = _inproj(lay, l, xa, f, mods3, norm1_w[l], w_in_b, tables)
        a_lat, a_ctx = _diff_attention(lay, p, lam, diff_norm_w[l], 1.0 - lam_init, not last)
        eye = jnp.eye(len(POOL_WINDOWS), dtype=F32)
        w_bd = (eye[:, None, :, None] * pool_w[l][:, :, None, :]).reshape(POOL_WIDTH, POOL_WIDTH).astype(MXU_DTYPE)
        bp = _pool(lay, p, w_bd, pool_scale[l])
        lg_f = -jnp.exp(ret_a_f[l].astype(F32))
        lg_b = -jnp.exp(ret_a_b[l].astype(F32))
        lanes = lambda lg: jnp.repeat(lg, HEAD_DIM).reshape(1, RET_WIDTH)
        yf, yb = _retention(lay, p, lg_f, lanes(lg_f), lg_b, lanes(lg_b))
        n_tok = lay.NL if last else lay.T
        xa, hx, logits_t = _outproj(lay, l, n_tok, xa, a_lat, a_ctx, bp, yf, yb, p, ret_norm_w[l], w_out_b, mods3,
                                    norm2_w[l], rw_pad)
        hx, cls_rank, counts = _router(n_tok, D, hx, logits_t, rb_col, tri)
        final = (xa, mods3, l * MOD_ROWS * N_MOD + 5, L, final_norm_w) if last else None
        f = _moe(l, n_tok, D, hx, cls_rank, counts, wg_b, wu_b, wd_b, final)
    return f.reshape(B, L, D)
```

```python
import functools
import math

import jax
import jax.numpy as jnp
from jax import lax
from jax.experimental import pallas as pl
from jax.experimental.pallas import tpu as pltpu

F32 = jnp.float32
MXU_DTYPE = jnp.bfloat16
ACT_DTYPE = jnp.bfloat16

GRID_W = 64
HEAD_DIM = 64
DIFF_HEADS = 4
DIFF_V_DIM = 2 * HEAD_DIM
DIFF_WIDTH = DIFF_HEADS * DIFF_V_DIM
POOL_WINDOWS = (2, 4, 8, 16)
POOL_GROUP = 64
POOL_WIDTH = POOL_GROUP * len(POOL_WINDOWS)
RET_HEADS = 4
RET_WIDTH = RET_HEADS * HEAD_DIM
OFF_DQ = 0
OFF_DK = OFF_DQ + DIFF_WIDTH
OFF_DV = OFF_DK + DIFF_WIDTH
OFF_PU = OFF_DV + DIFF_WIDTH
OFF_RQ = OFF_PU + POOL_WIDTH
OFF_RK = OFF_RQ + RET_WIDTH
OFF_RV = OFF_RK + RET_WIDTH
OFF_RG = OFF_RV + RET_WIDTH
IN_WIDTH = OFF_RG + RET_WIDTH
ROPE_BASE = 10000.0
ROPE_FREQS = HEAD_DIM // 4
N_EXPERTS = 16
N_GROUPS = 4
EXPERTS_PER_GROUP = N_EXPERTS // N_GROUPS
PAIRS_PER_GROUP = EXPERTS_PER_GROUP * (EXPERTS_PER_GROUP - 1) // 2
N_CLASSES = N_GROUPS * PAIRS_PER_GROUP
N_MOD = 6
EPS = 1e-6

LANES = 128
SUBLANES = 8
MOD_ROWS = 16
TOKEN_TILE = 512
SEQ_TILE = 256
ATTN_SUBTILES = 4
ATTN_SUB_ROWS = 256
ATTN_KEY_CHUNK = 512
HALO = 16
MOE_TILE = 256
COL = 256
OUTPROJ_SUBTILES = 2
ROUTER_BLOCK = 2048
CLASS_ROWS = 32
PERM_CHUNK = 2048
REC_E0, REC_E1, REC_G0, REC_G1, REC_CLS, REC_RANK = range(6)
VMEM_LIMIT = 56 * 1024 * 1024


def _mm(a, b):
    return jnp.dot(a.astype(MXU_DTYPE), b.astype(MXU_DTYPE), preferred_element_type=F32)


def _mm_nt(a, b):
    return lax.dot_general(a.astype(MXU_DTYPE), b.astype(MXU_DTYPE), (((1,), (1,)), ((), ())),
                           preferred_element_type=F32)


def _mm_tn(a, b):
    return lax.dot_general(a.astype(MXU_DTYPE), b.astype(MXU_DTYPE), (((0,), (0,)), ((), ())),
                           preferred_element_type=F32)


def _split(x):
    hi = x.astype(MXU_DTYPE)
    lo = (x - hi.astype(F32)).astype(MXU_DTYPE)
    return hi, lo


def _mm_hi(a, b):
    ah, al = _split(a)
    bh, bl = _split(b)
    d = lambda u, v: jnp.dot(u, v, preferred_element_type=F32)
    return d(ah, bh) + (d(ah, bl) + d(al, bh))


def _sigmoid(x):
    return 1.0 / (1.0 + jnp.exp(-x))


def _rms(x):
    return x * lax.rsqrt(jnp.mean(x * x, axis=-1, keepdims=True) + EPS)


def _cparams(*sem):
    return pltpu.CompilerParams(dimension_semantics=sem, vmem_limit_bytes=VMEM_LIMIT)


def _ada_body(c_ref, w_ref, b_ref, o_ref):
    c = c_ref[...]
    o_ref[...] = _mm_hi(c * _sigmoid(c), w_ref[...]) + b_ref[...]


def _ada_table(c_all, ada_w, ada_b):
    depth, d, n = ada_w.shape
    nb = n // N_MOD
    return pl.pallas_call(
        _ada_body,
        grid=(depth, n // nb),
        in_specs=[pl.BlockSpec((MOD_ROWS, d), lambda l, j: (0, 0)),
                  pl.BlockSpec((None, d, nb), lambda l, j: (l, 0, j)),
                  pl.BlockSpec((None, 1, nb), lambda l, j: (l, 0, j))],
        out_specs=pl.BlockSpec((None, MOD_ROWS, nb), lambda l, j: (l, 0, j)),
        out_shape=jax.ShapeDtypeStruct((depth, MOD_ROWS, n), F32),
        compiler_params=_cparams("arbitrary", "arbitrary"),
        name="ada_table",
    )(c_all, ada_w, ada_b.reshape(depth, 1, n))


class _Layout:
    def __init__(self, B, L, Lc, D):
        self.B, self.L, self.Lc, self.D = B, L, Lc, D
        self.NL, self.NC = B * L, B * Lc
        self.T = self.NL + self.NC
        assert L % TOKEN_TILE == 0 and self.NC % TOKEN_TILE == 0
        assert L % SEQ_TILE == 0 and Lc == SEQ_TILE and L % GRID_W == 0
        self.n_lat_tiles = self.NL // TOKEN_TILE
        self.tiles_per_seq = L // TOKEN_TILE

    def mod_row(self, i):
        return jnp.where(i < self.n_lat_tiles, i // self.tiles_per_seq, self.B)

    def rope_block(self, i):
        return jnp.where(i < self.n_lat_tiles, i % self.tiles_per_seq, self.tiles_per_seq)

    def mod_spec(self, layer, k):
        base = layer * MOD_ROWS * N_MOD + k
        return pl.BlockSpec((1, 1, self.D), lambda i: (base + self.mod_row(i) * N_MOD, 0, 0))


def _rope_tables(lay):
    L = lay.L
    rows = L // GRID_W
    row = jnp.repeat(jnp.arange(rows), GRID_W).astype(F32)
    col = jnp.tile(jnp.arange(GRID_W), rows).astype(F32)
    inv = ROPE_BASE ** (-jnp.arange(ROPE_FREQS, dtype=F32) / ROPE_FREQS)
    ang_r = row[:, None] * inv
    ang_c = col[:, None] * inv
    ang = jnp.concatenate([ang_r, ang_r, ang_c, ang_c], axis=-1)
    cos, sin = jnp.cos(ang), jnp.sin(ang)
    first_half = (jnp.arange(HEAD_DIM) % (2 * ROPE_FREQS)) < ROPE_FREQS
    sin_a = jnp.where(first_half, -sin, 0.0)
    sin_b = jnp.where(first_half, 0.0, sin)
    rep = LANES // HEAD_DIM
    ident = [jnp.ones((TOKEN_TILE, LANES), F32), jnp.zeros((TOKEN_TILE, LANES), F32),
             jnp.zeros((TOKEN_TILE, LANES), F32)]
    return [jnp.concatenate([jnp.tile(t, (1, rep)), e], axis=0) for t, e in zip((cos, sin_a, sin_b), ident)]


_ROPE_COLS = tuple(range(OFF_DQ // COL, OFF_DV // COL)) + (OFF_RQ // COL, OFF_RK // COL)
LOG2E = 1.4426950408889634
_COL_SCALE = {j: HEAD_DIM ** -0.5 * LOG2E for j in range(OFF_DQ // COL, OFF_DK // COL)}
_COL_SCALE[OFF_RK // COL] = HEAD_DIM ** -0.5


def _inproj_body(*refs, with_f, n_lat_tiles):
    if with_f:
        (x_ref, f_ref, g2_ref, nw_ref, sh_ref, sc_ref, w_ref, cos_ref, sa_ref, sb_ref, p_ref, xo_ref) = refs
        x = x_ref[...] + g2_ref[0] * f_ref[...]
    else:
        (xl_ref, xc_ref, nw_ref, sh_ref, sc_ref, w_ref, cos_ref, sa_ref, sb_ref, p_ref, xo_ref) = refs
        x = jnp.where(pl.program_id(0) < n_lat_tiles, xl_ref[...], xc_ref[...])
    xo_ref[...] = x
    h = (_rms(x) * nw_ref[...]) * (1.0 + sc_ref[0]) + sh_ref[0]
    hb = h.astype(MXU_DTYPE)
    cos, sa, sb = cos_ref[...], sa_ref[...], sb_ref[...]
    for j in range(IN_WIDTH // COL):
        acc = jnp.dot(hb, w_ref[:, j * COL:(j + 1) * COL], preferred_element_type=F32)
        if j in _ROPE_COLS:
            parts = []
            for t in range(COL // LANES):
                a = acc[:, t * LANES:(t + 1) * LANES]
                parts.append(a * cos + pltpu.roll(a, LANES - ROPE_FREQS, 1) * sa + pltpu.roll(a, ROPE_FREQS, 1) * sb)
            acc = jnp.concatenate(parts, axis=1)
        if j in _COL_SCALE:
            acc = acc * _COL_SCALE[j]
        p_ref[:, j * COL:(j + 1) * COL] = acc.astype(p_ref.dtype)


def _inproj(lay, layer, x, f, mods3, nw, w_in, tables):
    D, TM = lay.D, TOKEN_TILE
    row = lambda i: (i, 0)
    const = lambda i: (0, 0)
    n_lat = lay.n_lat_tiles
    tbl = pl.BlockSpec((TM, LANES), lambda i: (lay.rope_block(i), 0))
    if f is not None:
        in_specs = [pl.BlockSpec((TM, D), row), pl.BlockSpec((TM, D), row), lay.mod_spec(layer - 1, 5)]
        args = [x, f, mods3]
    else:
        in_specs = [pl.BlockSpec((TM, D), lambda i: (jnp.minimum(i, n_lat - 1), 0)),
                    pl.BlockSpec((TM, D), lambda i: (jnp.maximum(i - n_lat, 0), 0))]
        args = list(x)
    in_specs += [pl.BlockSpec((1, D), const), lay.mod_spec(layer, 0), lay.mod_spec(layer, 1),
                 pl.BlockSpec((None, D, IN_WIDTH), lambda i: (layer, 0, 0)), tbl, tbl, tbl]
    args += [nw.reshape(1, D), mods3, mods3, w_in] + list(tables)
    return pl.pallas_call(
        functools.partial(_inproj_body, with_f=f is not None, n_lat_tiles=n_lat),
        grid=(lay.T // TM,), in_specs=in_specs,
        out_specs=[pl.BlockSpec((TM, IN_WIDTH), row), pl.BlockSpec((TM, D), row)],
        out_shape=[jax.ShapeDtypeStruct((lay.T, IN_WIDTH), ACT_DTYPE), jax.ShapeDtypeStruct((lay.T, D), F32)],
        compiler_params=_cparams("arbitrary"), name="inproj",
    )(*args)


def _map_masks(q):
    lane = lax.broadcasted_iota(jnp.int32, (1, DIFF_V_DIM), 1)
    zero = jnp.zeros_like(q)
    return [jnp.where(lane < HEAD_DIM, q, zero), jnp.where(lane >= HEAD_DIM, q, zero)]


def _scores(qm, k_refs):
    chunks, run = [], None
    for k in k_refs:
        for c0 in range(0, k.shape[0], ATTN_KEY_CHUNK):
            s = _mm_nt(qm, k[c0:c0 + ATTN_KEY_CHUNK, :])
            chunks.append(s)
            for t in range(s.shape[1] // LANES):
                piece = s[:, t * LANES:(t + 1) * LANES]
                run = piece if run is None else jnp.maximum(run, piece)
    return chunks, jnp.max(run, axis=-1, keepdims=True)


def _softmax_diff(maps, lam):
    es, ls = [], []
    for chunks, mx in maps:
        e = [jnp.exp2(t - mx) for t in chunks]
        ls.append(functools.reduce(jnp.add, [jnp.sum(t, axis=-1, keepdims=True) for t in e]))
        es.append([t.astype(MXU_DTYPE) for t in e])
    c = (lam * ls[0] / ls[1]).astype(MXU_DTYPE)
    return [e1 - e2 * c for e1, e2 in zip(es[0], es[1])], 1.0 / ls[0]


def _head_norm(o, w, post_scale):
    return (_rms(o) * w) * post_scale


def _attn_body(lam_ref, q_ref, *refs, n_seg, n_sub, post_scale):
    k_refs, v_refs = refs[:n_seg], refs[n_seg:2 * n_seg]
    w_ref, o_ref = refs[2 * n_seg], refs[2 * n_seg + 1]
    rows = q_ref.shape[0] // n_sub

    def finish(u, maps):
        p, r1 = _softmax_diff(maps, lam_ref[0])
        o, at = None, 0
        for v in v_refs:
            n_c = -(-v.shape[0] // ATTN_KEY_CHUNK)
            part = jnp.dot(jnp.concatenate(p[at:at + n_c], axis=1), v[...], preferred_element_type=F32)
            o = part if o is None else o + part
            at += n_c
        o_ref[u * rows:(u + 1) * rows, :] = _head_norm(o * r1, w_ref[...], post_scale).astype(o_ref.dtype)

    scores = [[_scores(qm, k_refs) for qm in _map_masks(q_ref[u * rows:(u + 1) * rows, :])] for u in range(n_sub)]
    for u, maps in enumerate(scores):
        finish(u, maps)


def _diff_attention(lay, p, lam, norm_w, post_scale, need_ctx):
    B, L, Lc, H = lay.B, lay.L, lay.Lc, DIFF_HEADS
    W = DIFF_V_DIM
    tq = ATTN_SUBTILES * ATTN_SUB_ROWS
    assert L % tq == 0
    nq = L // tq
    k_blk, v_blk = OFF_DK // W, OFF_DV // W
    ctx0 = lay.NL // Lc
    lam_spec = pl.BlockSpec(memory_space=pltpu.SMEM)
    w_spec = pl.BlockSpec((1, W), lambda *_: (0, 0))
    a_lat = pl.pallas_call(
        functools.partial(_attn_body, n_seg=2, n_sub=ATTN_SUBTILES, post_scale=post_scale),
        grid=(B, H, nq),
        in_specs=[lam_spec,
                  pl.BlockSpec((tq, W), lambda b, h, i: (b * nq + i, h)),
                  pl.BlockSpec((Lc, W), lambda b, h, i: (ctx0 + b, k_blk + h)),
                  pl.BlockSpec((L, W), lambda b, h, i: (b, k_blk + h)),
                  pl.BlockSpec((Lc, W), lambda b, h, i: (ctx0 + b, v_blk + h)),
                  pl.BlockSpec((L, W), lambda b, h, i: (b, v_blk + h)),
                  w_spec],
        out_specs=pl.BlockSpec((tq, W), lambda b, h, i: (b * nq + i, h)),
        out_shape=jax.ShapeDtypeStruct((lay.NL, DIFF_WIDTH), ACT_DTYPE),
        compiler_params=_cparams("arbitrary", "arbitrary", "arbitrary"),
        name="diff_attn",
    )(lam, p, p, p, p, p, norm_w.reshape(1, W))
    if not need_ctx:
        return a_lat, None
    a_ctx = pl.pallas_call(
        functools.partial(_attn_body, n_seg=1, n_sub=1, post_scale=post_scale),
        grid=(B, H),
        in_specs=[lam_spec,
                  pl.BlockSpec((Lc, W), lambda b, h: (ctx0 + b, h)),
                  pl.BlockSpec((Lc, W), lambda b, h: (ctx0 + b, k_blk + h)),
                  pl.BlockSpec((Lc, W), lambda b, h: (ctx0 + b, v_blk + h)),
                  w_spec],
        out_specs=pl.BlockSpec((Lc, W), lambda b, h: (b, h)),
        out_shape=jax.ShapeDtypeStruct((lay.NC, DIFF_WIDTH), ACT_DTYPE),
        compiler_params=_cparams("arbitrary", "arbitrary"),
        name="diff_attn_ctx",
    )(lam, p, p, p, norm_w.reshape(1, W))
    return a_lat, a_ctx


def _pool_body(prev_ref, cur_ref, next_ref, w_ref, scale_ref, o_ref, *, lay):
    i = pl.program_id(0)
    n_lat = lay.NL // SEQ_TILE
    per_seq = jnp.where(i < n_lat, lay.L // SEQ_TILE, lay.Lc // SEQ_TILE)
    idx = jnp.where(i < n_lat, i, i - n_lat) % per_seq
    seq_len = per_seq * SEQ_TILE
    has_prev = (idx > 0).astype(F32)
    has_next = (idx < per_seq - 1).astype(F32)
    cur = cur_ref[...].astype(F32)
    u = jnp.concatenate([prev_ref[...].astype(F32) * has_prev, cur, next_ref[...].astype(F32) * has_next], axis=0)
    sums = {1: u}
    w = 1
    while w < POOL_WINDOWS[-1]:
        s = sums[w]
        m = s.shape[0] - w
        sums[2 * w] = s[:m] + s[w:w + m]
        w *= 2
    pos = idx * SEQ_TILE + lax.broadcasted_iota(jnp.int32, (SEQ_TILE, 1), 0)
    group = lax.broadcasted_iota(jnp.int32, (1, POOL_WIDTH), 1) // POOL_GROUP
    mean = jnp.zeros((SEQ_TILE, POOL_WIDTH), F32)
    for g, win in enumerate(POOL_WINDOWS):
        start = HALO - win // 2
        cnt = jnp.minimum(pos + (win - win // 2), seq_len) - jnp.maximum(pos - win // 2, 0)
        mean = jnp.where(group == g, sums[win][start:start + SEQ_TILE] / cnt.astype(F32), mean)
    o_ref[...] = (_mm(mean - cur, w_ref[...]) * scale_ref[...]).astype(o_ref.dtype)


def _pool(lay, p, w_bd, scale):
    n = lay.T // SEQ_TILE
    col = OFF_PU // POOL_WIDTH
    per = SEQ_TILE // HALO
    last = lay.T // HALO - 1
    return pl.pallas_call(
        functools.partial(_pool_body, lay=lay),
        grid=(n,),
        in_specs=[pl.BlockSpec((HALO, POOL_WIDTH), lambda i: (jnp.maximum(i * per - 1, 0), col)),
                  pl.BlockSpec((SEQ_TILE, POOL_WIDTH), lambda i: (i, col)),
                  pl.BlockSpec((HALO, POOL_WIDTH), lambda i: (jnp.minimum((i + 1) * per, last), col)),
                  pl.BlockSpec((POOL_WIDTH, POOL_WIDTH), lambda i: (0, 0)),
                  pl.BlockSpec((1, POOL_WIDTH), lambda i: (0, 0))],
        out_specs=pl.BlockSpec((SEQ_TILE, POOL_WIDTH), lambda i: (i, 0)),
        out_shape=jax.ShapeDtypeStruct((lay.T, POOL_WIDTH), ACT_DTYPE),
        compiler_params=_cparams("arbitrary"), name="pool",
    )(p, p, p, w_bd, scale.reshape(1, POOL_WIDTH))


def _ret_tables(lgs_ref, lgv, d_ref, xi_ref, zeta_ref, reverse):
    C = SEQ_TILE
    ri = lax.broadcasted_iota(jnp.int32, (C, C), 0)
    ci = lax.broadcasted_iota(jnp.int32, (C, C), 1)
    dist = (ci - ri if reverse else ri - ci).astype(F32)
    keep = dist > 0 if reverse else dist >= 0
    for h in range(RET_HEADS):
        d_ref[h * C:(h + 1) * C, :] = jnp.where(keep, jnp.exp(lgs_ref[h] * jnp.maximum(dist, 0.0)), 0.0)
    t = lax.broadcasted_iota(jnp.int32, (C, 1), 0).astype(F32)
    xi_ref[...] = jnp.exp(lgv * ((C - t) if reverse else (t + 1.0)))
    zeta_ref[...] = jnp.exp(lgv * (t if reverse else (C - 1.0 - t)))


def _ret_chunk(q, k, v, lgv, s_ref, d_ref, xi_ref, zeta_ref):
    C = SEQ_TILE
    lane_head = lax.broadcasted_iota(jnp.int32, (1, RET_WIDTH), 1) // HEAD_DIM
    row_head = lax.broadcasted_iota(jnp.int32, (RET_WIDTH, 1), 0) // HEAD_DIM
    zero = jnp.zeros_like(q)
    qs = jnp.concatenate([jnp.where(lane_head == h, q, zero) for h in range(RET_HEADS)], axis=0)
    sd = _mm_nt(qs, k) * d_ref[...]
    yv = _mm(sd, v)
    y = functools.reduce(jnp.add, [jnp.where(lane_head == h, yv[h * C:(h + 1) * C], 0.0) for h in range(RET_HEADS)])
    state = s_ref[...]
    y = y + _mm(q.astype(F32) * xi_ref[...], state)
    kv = _mm_tn(k.astype(F32) * zeta_ref[...], v)
    s_ref[...] = state * jnp.exp(lgv * C) + jnp.where(row_head == lane_head, kv, 0.0)
    return y


def _ret_body(lgs_f, lgv_f, lgs_b, lgv_b, qf, kf, vf, qb, kb, vb, yf_ref, yb_ref,
              s_f, d_f, xi_f, zeta_f, s_b, d_b, xi_b, zeta_b):
    b, j = pl.program_id(0), pl.program_id(1)

    @pl.when((b == 0) & (j == 0))
    def _():
        _ret_tables(lgs_f, lgv_f[...], d_f, xi_f, zeta_f, False)
        _ret_tables(lgs_b, lgv_b[...], d_b, xi_b, zeta_b, True)

    @pl.when(j == 0)
    def _():
        s_f[...] = jnp.zeros_like(s_f)
        s_b[...] = jnp.zeros_like(s_b)

    yf_ref[...] = _ret_chunk(qf[...], kf[...], vf[...], lgv_f[...], s_f, d_f, xi_f, zeta_f)
    yb_ref[...] = _ret_chunk(qb[...], kb[...], vb[...], lgv_b[...], s_b, d_b, xi_b, zeta_b)


def _retention(lay, p, lgs_f, lgv_f, lgs_b, lgv_b):
    B, L = lay.B, lay.L
    C, W = SEQ_TILE, RET_WIDTH
    nch = L // C
    ctx0 = lay.NL // C
    rows_f = lambda b, j: jnp.where(j == 0, ctx0 + b, b * nch + j - 1)
    rows_b = lambda b, j: jnp.where(j == 0, ctx0 + b, b * nch + nch - j)
    spec = lambda rows, c: pl.BlockSpec((C, W), lambda b, j: (rows(b, j), c))
    smem = pl.BlockSpec(memory_space=pltpu.SMEM)
    vec = pl.BlockSpec((1, W), lambda b, j: (0, 0))
    cols = (OFF_RQ // W, OFF_RK // W, OFF_RV // W)
    tables = [pltpu.VMEM((W, W), F32), pltpu.VMEM((RET_HEADS * C, C), F32), pltpu.VMEM((C, W), F32), pltpu.VMEM((C, W), F32)]
    return pl.pallas_call(
        _ret_body,
        grid=(B, nch + 1),
        in_specs=[smem, vec, smem, vec] + [spec(rows_f, c) for c in cols] + [spec(rows_b, c) for c in cols],
        out_specs=[spec(rows_f, 0), spec(rows_b, 0)],
        out_shape=[jax.ShapeDtypeStruct((lay.T, W), F32), jax.ShapeDtypeStruct((lay.T, W), F32)],
        scratch_shapes=tables + tables,
        compiler_params=_cparams("arbitrary", "arbitrary"),
        name="retention",
    )(lgs_f, lgv_f, lgs_b, lgv_b, p, p, p, p, p, p)


def _retention_output(yf, yb, gate, w):
    y = yf + yb
    lane_head = lax.broadcasted_iota(jnp.int32, (1, RET_WIDTH), 1) // HEAD_DIM
    row_head = lax.broadcasted_iota(jnp.int32, (RET_WIDTH, 1), 0) // HEAD_DIM
    same = (row_head == lane_head).astype(F32) * (1.0 / HEAD_DIM)
    ms = _mm_hi(y * y, same)
    g = gate.astype(F32)
    return (g * _sigmoid(g)) * (y * lax.rsqrt(ms + EPS) * w)


def _top2(vals):
    n = len(vals)
    v1 = functools.reduce(jnp.maximum, vals)
    i1 = jnp.full_like(v1, n - 1)
    for e in range(n - 2, -1, -1):
        i1 = jnp.where(vals[e] == v1, float(e), i1)
    rest = [jnp.where(i1 == float(e), -jnp.inf, vals[e]) for e in range(n)]
    v2 = functools.reduce(jnp.maximum, rest)
    i2 = jnp.full_like(v1, n - 1)
    for e in range(n - 2, -1, -1):
        i2 = jnp.where(rest[e] == v2, float(e), i2)
    return v1, i1, v2, i2


def _route(logits_t, bias):
    s = _sigmoid(logits_t)
    sel = s + bias
    groups = []
    for g in range(N_GROUPS):
        rows = [sel[g * EXPERTS_PER_GROUP + e:g * EXPERTS_PER_GROUP + e + 1] for e in range(EXPERTS_PER_GROUP)]
        groups.append(_top2(rows))
    score = [v1 + v2 for v1, _, v2, _ in groups]
    best = functools.reduce(jnp.maximum, score)
    gi = jnp.full_like(best, N_GROUPS - 1)
    for g in range(N_GROUPS - 2, -1, -1):
        gi = jnp.where(score[g] == best, float(g), gi)
    pick = lambda k: functools.reduce(
        lambda acc, g: jnp.where(gi == float(g), groups[g][k], acc), range(N_GROUPS - 1), groups[N_GROUPS - 1][k])
    e0 = gi * EXPERTS_PER_GROUP + pick(1)
    e1 = gi * EXPERTS_PER_GROUP + pick(3)
    s0 = jnp.zeros_like(best)
    s1 = jnp.zeros_like(best)
    for e in range(N_EXPERTS):
        s0 = jnp.where(e0 == float(e), s[e:e + 1], s0)
        s1 = jnp.where(e1 == float(e), s[e:e + 1], s1)
    tot = s0 + s1
    return e0, e1, s0 / tot, s1 / tot


def _outproj_body(*refs, lay, with_ctx):
    if with_ctx:
        (x_ref, al_ref, ac_ref, b_ref, yf_ref, yb_ref, rg_ref, rnw_ref, w_ref, g1_ref, nw_ref, sh_ref, sc_ref, rw_ref,
         xo_ref, hx_ref, lt_ref) = refs
        a = jnp.where(pl.program_id(0) < lay.n_lat_tiles, al_ref[...], ac_ref[...])
    else:
        (x_ref, al_ref, b_ref, yf_ref, yb_ref, rg_ref, rnw_ref, w_ref, g1_ref, nw_ref, sh_ref, sc_ref, rw_ref,
         xo_ref, hx_ref, lt_ref) = refs
        a = al_ref[...]
    D = lay.D
    half = TOKEN_TILE // OUTPROJ_SUBTILES
    for u in range(OUTPROJ_SUBTILES):
        rows = slice(u * half, (u + 1) * half)
        r = _retention_output(yf_ref[rows, :], yb_ref[rows, :], rg_ref[rows, :], rnw_ref[...])
        y = (_mm(a[rows], w_ref[:DIFF_WIDTH, :])
             + _mm(b_ref[rows, :], w_ref[DIFF_WIDTH:DIFF_WIDTH + POOL_WIDTH, :])
             + _mm(r, w_ref[DIFF_WIDTH + POOL_WIDTH:, :]))
        x = x_ref[rows, :] + g1_ref[0] * y
        xo_ref[rows, :] = x
        h = (_rms(x) * nw_ref[...]) * (1.0 + sc_ref[0]) + sh_ref[0]
        hx_ref[rows, :D] = h
        hx_ref[rows, D:] = jnp.zeros((half, LANES), F32)
        lt_ref[:, rows] = jnp.transpose(_mm(h, rw_ref[...]))[:N_EXPERTS]


def _outproj(lay, layer, n_tok, x, a_lat, a_ctx, b, yf, yb, p, ret_nw, w_out, mods3, nw, rw_pad):
    D, TM = lay.D, TOKEN_TILE
    row = lambda i: (i, 0)
    const = lambda i: (0, 0)
    tile = lambda w: pl.BlockSpec((TM, w), row)
    n_lat = lay.n_lat_tiles
    in_specs = [tile(D), pl.BlockSpec((TM, DIFF_WIDTH), lambda i: (jnp.minimum(i, n_lat - 1), 0))]
    args = [x, a_lat]
    if a_ctx is not None:
        in_specs.append(pl.BlockSpec((TM, DIFF_WIDTH), lambda i: (jnp.maximum(i - n_lat, 0), 0)))
        args.append(a_ctx)
    in_specs += [tile(POOL_WIDTH), tile(RET_WIDTH), tile(RET_WIDTH),
                 pl.BlockSpec((TM, RET_WIDTH), lambda i: (i, OFF_RG // RET_WIDTH)), pl.BlockSpec((1, RET_WIDTH), const),
                 pl.BlockSpec((None, DIFF_WIDTH + POOL_WIDTH + RET_WIDTH, D), lambda i: (layer, 0, 0)),
                 lay.mod_spec(layer, 2), pl.BlockSpec((1, D), const), lay.mod_spec(layer, 3), lay.mod_spec(layer, 4),
                 pl.BlockSpec((D, LANES), const)]
    args += [b, yf, yb, p, ret_nw.reshape(1, RET_WIDTH), w_out, mods3, nw.reshape(1, D), mods3, mods3, rw_pad]
    return pl.pallas_call(
        functools.partial(_outproj_body, lay=lay, with_ctx=a_ctx is not None),
        grid=(n_tok // TM,), in_specs=in_specs,
        out_specs=[tile(D), tile(D + LANES), pl.BlockSpec((N_EXPERTS, TM), lambda i: (0, i))],
        out_shape=[jax.ShapeDtypeStruct((n_tok, D), F32), jax.ShapeDtypeStruct((n_tok, D + LANES), F32),
                   jax.ShapeDtypeStruct((N_EXPERTS, n_tok), F32)],
        compiler_params=_cparams("arbitrary"), name="outproj",
    )(*args)


def _router_body(lt_ref, rb_ref, tri_ref, hx_any, rec_ref, cr_ref, cnt_ref, run_ref):
    del hx_any
    n = lt_ref.shape[1]
    e0, e1, g0, g1 = _route(lt_ref[...], rb_ref[...])
    lo, hi = jnp.minimum(e0, e1), jnp.maximum(e0, e1)
    grp = jnp.floor(lo * (1.0 / EXPERTS_PER_GROUP))
    pa, pb = lo - grp * EXPERTS_PER_GROUP, hi - grp * EXPERTS_PER_GROUP
    cls = grp * PAIRS_PER_GROUP + pa * ((2 * EXPERTS_PER_GROUP - 1) - pa) * 0.5 + (pb - pa - 1.0)

    @pl.when(pl.program_id(0) == 0)
    def _():
        run_ref[...] = jnp.zeros_like(run_ref)

    run = run_ref[:, :1]
    onehot = jnp.where(cls == lax.broadcasted_iota(jnp.int32, (CLASS_ROWS, 1), 0).astype(F32), 1.0, 0.0)
    tri = tri_ref[...]
    ranks = []
    for c0 in range(0, n, TOKEN_TILE):
        oh = onehot[:, c0:c0 + TOKEN_TILE]
        incl = _mm_nt(oh, tri)
        ranks.append(jnp.sum(oh * (incl + run), axis=0, keepdims=True) - 1.0)
        run = run + incl[:, TOKEN_TILE - 1:TOKEN_TILE]
    rank = jnp.concatenate(ranks, axis=1)
    run_ref[...] = jnp.broadcast_to(run, run_ref.shape)
    cnt_ref[...] = jnp.broadcast_to(run, cnt_ref.shape)
    cr_ref[...] = jnp.concatenate([cls, rank, jnp.zeros((6, n), F32)], axis=0)
    rec_ref[...] = jnp.transpose(jnp.concatenate([e0, e1, g0, g1, cls, rank, jnp.zeros((LANES - 6, n), F32)], axis=0))


def _router(n_tok, D, hx, logits_t, rb_col, tri):
    blk = math.gcd(n_tok, ROUTER_BLOCK)
    return pl.pallas_call(
        _router_body,
        grid=(n_tok // blk,),
        in_specs=[pl.BlockSpec((N_EXPERTS, blk), lambda i: (0, i)), pl.BlockSpec((N_EXPERTS, 1), lambda i: (0, 0)),
                  pl.BlockSpec((TOKEN_TILE, TOKEN_TILE), lambda i: (0, 0)), pl.BlockSpec(memory_space=pl.ANY)],
        out_specs=[pl.BlockSpec((blk, LANES), lambda i: (i, D // LANES)), pl.BlockSpec((8, blk), lambda i: (0, i)),
                   pl.BlockSpec((CLASS_ROWS, LANES), lambda i: (0, 0))],
        out_shape=[jax.ShapeDtypeStruct((n_tok, D + LANES), F32), jax.ShapeDtypeStruct((8, n_tok), F32),
                   jax.ShapeDtypeStruct((CLASS_ROWS, LANES), F32)],
        scratch_shapes=[pltpu.VMEM((CLASS_ROWS, LANES), F32)],
        input_output_aliases={3: 0},
        compiler_params=_cparams("arbitrary"), name="router",
    )(logits_t, rb_col, tri, hx)


def _row_copies(idx_ref, hbm, vmem, sem, scatter):
    groups = vmem.shape[0]

    def copy(g, k, row):
        src, dst = (vmem.at[g, pl.ds(k, 1)], hbm.at[pl.ds(row, 1)])
        return pltpu.make_async_copy(src, dst, sem) if scatter else pltpu.make_async_copy(dst, src, sem)

    def start(g, carry):
        for k in range(SUBLANES):
            copy(g, k, idx_ref[0, 0, g * SUBLANES + k]).start(priority=k % 2)
        return carry

    def wait(g, carry):
        for k in range(SUBLANES):
            copy(0, 0, 0).wait()
        return carry

    lax.fori_loop(0, groups, start, 0)
    lax.fori_loop(0, groups, wait, 0)


def _dispatch_body(pend_ref, padded_ref, nvalid_ref, pos_ref, hx_ref, xs_hbm, zbuf, zsem, sem, *, n_blk):
    TB = MOE_TILE
    i = pl.program_id(0)

    @pl.when(i == 0)
    def _():
        zbuf[...] = jnp.zeros_like(zbuf)
        fill = lambda row: pltpu.make_async_copy(zbuf, xs_hbm.at[pl.ds(row, TB)], zsem.at[0])
        for k in range(N_CLASSES):
            @pl.when(padded_ref[k] > 0)
            def _():
                fill(pl.multiple_of(pend_ref[k] - TB, TB)).start()
        for j in range(n_blk):
            @pl.when(j >= nvalid_ref[0])
            def _():
                fill(j * TB).start()
        for k in range(N_CLASSES):
            @pl.when(padded_ref[k] > 0)
            def _():
                fill(0).wait()
        for j in range(n_blk):
            @pl.when(j >= nvalid_ref[0])
            def _():
                fill(0).wait()

    _row_copies(pos_ref, xs_hbm, hx_ref, sem.at[0], True)


def _combine_body(pos_ref, ys_hbm, f_ref, sem):
    _row_copies(pos_ref, ys_hbm, f_ref, sem.at[0], False)


def _combine_final_body(pos_ref, ys_hbm, x_ref, g2_ref, nw_ref, o_ref, fbuf, sem):
    _row_copies(pos_ref, ys_hbm, fbuf, sem.at[0], False)
    o_ref[...] = _rms(x_ref[...] + g2_ref[0] * fbuf[...]) * nw_ref[...]


def _moe_body(elo_ref, ehi_ref, nvalid_ref, xs_ref, wg_lo, wu_lo, wd_lo, wg_hi, wu_hi, wd_hi, ys_ref, *, D):
    j = pl.program_id(0)

    @pl.when(j < nvalid_ref[0])
    def _():
        xb = xs_ref[:, :D].astype(MXU_DTYPE)
        info = xs_ref[:, D:]
        first_is_lo = info[:, REC_E0:REC_E0 + 1] <= info[:, REC_E1:REC_E1 + 1]
        g0, g1 = info[:, REC_G0:REC_G0 + 1], info[:, REC_G1:REC_G1 + 1]
        g_lo = jnp.where(first_is_lo, g0, g1)
        g_hi = jnp.where(first_is_lo, g1, g0)

        def ffn(wg, wu, wd):
            hg = jnp.dot(xb, wg[...], preferred_element_type=F32)
            hu = jnp.dot(xb, wu[...], preferred_element_type=F32)
            return _mm((hg * _sigmoid(hg)) * hu, wd[...])

        ys_ref[...] = g_lo * ffn(wg_lo, wu_lo, wd_lo) + g_hi * ffn(wg_hi, wu_hi, wd_hi)

    @pl.when(j >= nvalid_ref[0])
    def _():
        ys_ref[...] = jnp.zeros_like(ys_ref)


def _moe(layer, n_tok, D, hx, cls_rank, counts, wg, wu, wd, final=None):
    TB = MOE_TILE
    DE = wg.shape[-1]
    chunk = math.gcd(n_tok, PERM_CHUNK)
    if final is not None:
        chunk = math.gcd(chunk, final[3])
    cls = cls_rank[0].astype(jnp.int32)
    rank = cls_rank[1].astype(jnp.int32)
    cnt = counts[:N_CLASSES, 0].astype(jnp.int32)
    padded = (cnt + TB - 1) // TB * TB
    pend = jnp.cumsum(padded)
    pstart = pend - padded
    onehot = cls[:, None] == jnp.arange(N_CLASSES, dtype=jnp.int32)[None, :]
    pos = jnp.sum(jnp.where(onehot, pstart[None, :], 0), axis=1) + rank
    n_blk = -(-n_tok // TB) + N_CLASSES
    P = n_blk * TB
    nvalid = (pend[-1] // TB).astype(jnp.int32).reshape(1)
    blk = jnp.minimum(jnp.arange(n_blk, dtype=jnp.int32), nvalid - 1)
    blk_cls = jnp.minimum(jnp.sum(pend[None, :] <= (blk * TB)[:, None], axis=1), N_CLASSES - 1).astype(jnp.int32)
    pairs = [(u, v) for u in range(EXPERTS_PER_GROUP) for v in range(u + 1, EXPERTS_PER_GROUP)]
    pair_lo = jnp.array([u for u, _ in pairs], jnp.int32)
    pair_hi = jnp.array([v for _, v in pairs], jnp.int32)
    base = (blk_cls // PAIRS_PER_GROUP) * EXPERTS_PER_GROUP
    blk_lo = base + pair_lo[blk_cls % PAIRS_PER_GROUP]
    blk_hi = base + pair_hi[blk_cls % PAIRS_PER_GROUP]
    pos3 = pos.reshape(n_tok // chunk, 1, chunk)
    pos_spec = pl.BlockSpec((1, 1, chunk), lambda i, *_: (i, 0, 0), memory_space=pltpu.SMEM)
    any_spec = pl.BlockSpec(memory_space=pl.ANY)

    xs = pl.pallas_call(
        functools.partial(_dispatch_body, n_blk=n_blk),
        grid_spec=pltpu.PrefetchScalarGridSpec(
            num_scalar_prefetch=3, grid=(n_tok // chunk,),
            in_specs=[pos_spec, pl.BlockSpec((chunk // SUBLANES, SUBLANES, D + LANES), lambda i, *_: (i, 0, 0))],
            out_specs=any_spec,
            scratch_shapes=[pltpu.VMEM((TB, D + LANES), F32), pltpu.SemaphoreType.DMA((1,)),
                            pltpu.SemaphoreType.DMA((1,))]),
        out_shape=jax.ShapeDtypeStruct((P, D + LANES), F32),
        compiler_params=_cparams("arbitrary"), name="moe_dispatch",
    )(pend.astype(jnp.int32), padded.astype(jnp.int32), nvalid, pos3, hx.reshape(n_tok // SUBLANES, SUBLANES, D + LANES))

    w_spec = lambda shape, which: pl.BlockSpec(
        (None, None) + shape,
        (lambda j, elo, ehi, nv: (layer, elo[j], 0, 0)) if which == 0 else (lambda j, elo, ehi, nv: (layer, ehi[j], 0, 0)))
    ys = pl.pallas_call(
        functools.partial(_moe_body, D=D),
        grid_spec=pltpu.PrefetchScalarGridSpec(
            num_scalar_prefetch=3, grid=(n_blk,),
            in_specs=[pl.BlockSpec((TB, D + LANES), lambda j, elo, ehi, nv: (jnp.minimum(j, nv[0] - 1), 0)),
                      w_spec((D, DE), 0), w_spec((D, DE), 0), w_spec((DE, D), 0),
                      w_spec((D, DE), 1), w_spec((D, DE), 1), w_spec((DE, D), 1)],
            out_specs=pl.BlockSpec((TB, D), lambda j, elo, ehi, nv: (j, 0))),
        out_shape=jax.ShapeDtypeStruct((P, D), F32),
        compiler_params=_cparams("arbitrary"), name="moe_experts",
    )(blk_lo, blk_hi, nvalid, xs, wg, wu, wd, wg, wu, wd)

    rows = pl.BlockSpec((chunk // SUBLANES, SUBLANES, D), lambda i: (i, 0, 0))
    grouped = jax.ShapeDtypeStruct((n_tok // SUBLANES, SUBLANES, D), F32)
    if final is None:
        return pl.pallas_call(
            _combine_body,
            grid_spec=pltpu.PrefetchScalarGridSpec(
                num_scalar_prefetch=0, grid=(n_tok // chunk,), in_specs=[pos_spec, any_spec], out_specs=rows,
                scratch_shapes=[pltpu.SemaphoreType.DMA((1,))]),
            out_shape=grouped,
            compiler_params=_cparams("arbitrary"), name="moe_combine",
        )(pos3, ys).reshape(n_tok, D)
    x, mods3, g2_base, seq_len, nw = final
    g2_spec = pl.BlockSpec((1, 1, D), lambda i: (g2_base + (i // (seq_len // chunk)) * N_MOD, 0, 0))
    return pl.pallas_call(
        _combine_final_body,
        grid_spec=pltpu.PrefetchScalarGridSpec(
            num_scalar_prefetch=0, grid=(n_tok // chunk,),
            in_specs=[pos_spec, any_spec, rows, g2_spec, pl.BlockSpec((1, D), lambda i: (0, 0))], out_specs=rows,
            scratch_shapes=[pltpu.VMEM((chunk // SUBLANES, SUBLANES, D), F32), pltpu.SemaphoreType.DMA((1,))]),
        out_shape=grouped,
        compiler_params=_cparams("arbitrary"), name="moe_combine_final",
    )(pos3, ys, x.reshape(n_tok // SUBLANES, SUBLANES, D), mods3, nw.reshape(1, D)).reshape(n_tok, D)


def kernel(x, c, ctx, c_ctx, w_in, w_out, ada_w, ada_b, norm1_w, norm2_w, diff_lambda, diff_norm_w, pool_w,
           pool_scale, ret_a_f, ret_a_b, ret_norm_w, router_w, router_b, moe_w_gate, moe_w_up, moe_w_down,
           final_norm_w):
    B, L, D = x.shape
    Lc = ctx.shape[1]
    depth = w_in.shape[0]
    lay = _Layout(B, L, Lc, D)
    assert B + 1 <= MOD_ROWS

    xa = (x.reshape(B * L, D), ctx.reshape(B * Lc, D))
    c_all = jnp.zeros((MOD_ROWS, D), F32).at[:B].set(c).at[B].set(c_ctx)
    mods3 = _ada_table(c_all, ada_w, ada_b).reshape(depth * MOD_ROWS * N_MOD, 1, D)
    tables = _rope_tables(lay)
    rw_pad = jnp.zeros((D, LANES), MXU_DTYPE).at[:, :N_EXPERTS].set(router_w.astype(MXU_DTYPE))
    rb_col = router_b.astype(F32).reshape(N_EXPERTS, 1)
    tri = jnp.tril(jnp.ones((TOKEN_TILE, TOKEN_TILE), F32)).astype(MXU_DTYPE)
    w_in_b, w_out_b = w_in.astype(MXU_DTYPE), w_out.astype(MXU_DTYPE)
    wg_b, wu_b, wd_b = moe_w_gate.astype(MXU_DTYPE), moe_w_up.astype(MXU_DTYPE), moe_w_down.astype(MXU_DTYPE)

    f = None
    for l in range(depth):
        last = l == depth - 1
        lam_init = 0.8 - 0.6 * math.exp(-0.3 * l)
        dl = diff_lambda[l].astype(F32)
        lam = (jnp.exp(jnp.sum(dl[0] * dl[1])) - jnp.exp(jnp.sum(dl[2] * dl[3])) + lam_init).reshape(1)
        p, xa = _inproj(lay, l, xa, f, mods3, norm1_w[l], w_in_b, tables)
        a_lat, a_ctx = _diff_attention(lay, p, lam, diff_norm_w[l], 1.0 - lam_init, not last)
        eye = jnp.eye(len(POOL_WINDOWS), dtype=F32)
        w_bd = (eye[:, None, :, None] * pool_w[l][:, :, None, :]).reshape(POOL_WIDTH, POOL_WIDTH).astype(MXU_DTYPE)
        bp = _pool(lay, p, w_bd, pool_scale[l])
        lg_f = -jnp.exp(ret_a_f[l].astype(F32))
        lg_b = -jnp.exp(ret_a_b[l].astype(F32))
        lanes = lambda lg: jnp.repeat(lg, HEAD_DIM).reshape(1, RET_WIDTH)
        yf, yb = _retention(lay, p, lg_f, lanes(lg_f), lg_b, lanes(lg_b))
        n_tok = lay.NL if last else lay.T
        xa, hx, logits_t = _outproj(lay, l, n_tok, xa, a_lat, a_ctx, bp, yf, yb, p, ret_norm_w[l], w_out_b, mods3,
                                    norm2_w[l], rw_pad)
        hx, cls_rank, counts = _router(n_tok, D, hx, logits_t, rb_col, tri)
        final = (xa, mods3, l * MOD_ROWS * N_MOD + 5, L, final_norm_w) if last else None
        f = _moe(l, n_tok, D, hx, cls_rank, counts, wg_b, wu_b, wd_b, final)
    return f.reshape(B, L, D)
```
